```python
import jax, jax.numpy as jnp
from jax import lax
import numpy as np

D_MODEL = 2048
BATCH = 8
SEQ = 2048
DEPTH = 1

D_MIX = D_MODEL
GM_WIDTH = D_MIX // 2
GM_GROUPS = 8
GM_DG = GM_WIDTH // GM_GROUPS
CHUNK = 128
SB_WIDTH = D_MIX - GM_WIDTH
SB_HEADS = 8
SB_HEAD_DIM = SB_WIDTH // SB_HEADS
Q_BLOCK = 128
D_FF = -(-(8 * D_MODEL) // (3 * 256)) * 256
N_IN = 2 * GM_WIDTH + 3 * SB_WIDTH
N_MOD = 6
EPS = 1e-6

kernel_name = "hybrid_gmlp_stickbreaking_adaln_block"


def rmsnorm(x, g):
    xf = x.astype(jnp.float32)
    y = xf * lax.rsqrt(jnp.mean(xf * xf, axis=-1, keepdims=True) + EPS)
    return (y * g.astype(jnp.float32)).astype(x.dtype)


def group_rmsnorm(x, g, groups):
    xf = x.astype(jnp.float32).reshape(*x.shape[:-1], groups, -1)
    y = xf * lax.rsqrt(jnp.mean(xf * xf, axis=-1, keepdims=True) + EPS)
    return (y.reshape(x.shape) * g.astype(jnp.float32)).astype(x.dtype)


def group_layernorm(x, g, groups):
    xf = x.astype(jnp.float32).reshape(*x.shape[:-1], groups, -1)
    mu = jnp.mean(xf, axis=-1, keepdims=True)
    xc = xf - mu
    y = xc * lax.rsqrt(jnp.mean(xc * xc, axis=-1, keepdims=True) + EPS)
    return (y.reshape(x.shape) * g.astype(jnp.float32)).astype(x.dtype)


def chunked_spatial_gating(z, v_norm_g, w_s, b_s):
    B, S, _ = z.shape
    u, v = z[..., :GM_WIDTH], z[..., GM_WIDTH:]
    v = group_layernorm(v, v_norm_g, GM_GROUPS)
    v = v.reshape(B, S // CHUNK, CHUNK, GM_GROUPS, GM_DG)
    causal = jnp.tril(jnp.ones((CHUNK, CHUNK), dtype=bool))
    w = jnp.where(causal[None], w_s, 0).astype(v.dtype)
    mixed = jnp.einsum('gts,bnsgd->bntgd', w, v) + b_s.T.astype(v.dtype)[None, None, :, :, None]
    return u * mixed.reshape(B, S, GM_WIDTH)


def stick_breaking_attention(q, k, v):
    B, S, H, Dh = q.shape
    scale = Dh ** -0.5
    outs = []
    for i in range(S // Q_BLOCK):
        start, end = i * Q_BLOCK, (i + 1) * Q_BLOCK
        qs, ks, vs = q[:, start:end], k[:, :end], v[:, :end]
        z = jnp.einsum('bqhd,bkhd->bhqk', qs, ks).astype(jnp.float32) * scale
        t_pos = start + jnp.arange(Q_BLOCK)[:, None]
        s_pos = jnp.arange(end)[None, :]
        mask = s_pos < t_pos
        log_beta = jax.nn.log_sigmoid(z)
        log_1m = jnp.where(mask, jax.nn.log_sigmoid(-z), 0.0)
        tail = lax.cumsum(log_1m, axis=3, reverse=True) - log_1m
        a = jnp.where(mask, jnp.exp(log_beta + tail), 0.0)
        outs.append(jnp.einsum('bhqk,bkhd->bqhd', a.astype(vs.dtype), vs))
    return jnp.concatenate(outs, axis=1)


def setup_inputs(seed: int = 0) -> dict:
    key = jax.random.key(seed)
    ks = jax.random.split(key, 16)
    f32 = jnp.float32
    n = lambda k, shape: jax.random.normal(k, shape, dtype=f32)
    return {
        "x": n(ks[0], (BATCH, SEQ, D_MODEL)),
        "c": n(ks[1], (BATCH, D_MODEL)),
        "w_ada": n(ks[2], (DEPTH, D_MODEL, N_MOD * D_MODEL)) * (0.5 * D_MODEL ** -0.5),
        "b_ada": n(ks[3], (DEPTH, N_MOD * D_MODEL)) * 0.01,
        "norm1_g": 1.0 + 0.01 * n(ks[4], (DEPTH, D_MODEL)),
        "w_in": n(ks[5], (DEPTH, D_MODEL, N_IN)) * D_MODEL ** -0.5,
        "v_norm_g": 1.0 + 0.01 * n(ks[6], (DEPTH, GM_WIDTH)),
        "w_spatial": n(ks[7], (DEPTH, GM_GROUPS, CHUNK, CHUNK)) * CHUNK ** -0.5,
        "b_spatial": 1.0 + 0.01 * n(ks[8], (DEPTH, GM_GROUPS, CHUNK)),
        "out_norm_g": 1.0 + 0.01 * n(ks[9], (DEPTH, D_MIX)),
        "w_out": n(ks[10], (DEPTH, D_MIX, D_MODEL)) * D_MIX ** -0.5,
        "norm2_g": 1.0 + 0.01 * n(ks[11], (DEPTH, D_MODEL)),
        "w_gate": n(ks[12], (DEPTH, D_MODEL, D_FF)) * D_MODEL ** -0.5,
        "w_up": n(ks[13], (DEPTH, D_MODEL, D_FF)) * D_MODEL ** -0.5,
        "w_down": n(ks[14], (DEPTH, D_FF, D_MODEL)) * D_FF ** -0.5,
        "final_g": 1.0 + 0.01 * n(ks[15], (D_MODEL,)),
    }


def reference(x, c, w_ada, b_ada, norm1_g, w_in, v_norm_g, w_spatial, b_spatial,
              out_norm_g, w_out, norm2_g, w_gate, w_up, w_down, final_g):
    B, S, _ = x.shape
    c_act = jax.nn.silu(c)
    for l in range(DEPTH):
        mod = c_act @ w_ada[l] + b_ada[l]
        shift1, scale1, gate1, shift2, scale2, gate2 = [m[:, None, :] for m in jnp.split(mod, N_MOD, axis=-1)]

        h = rmsnorm(x, norm1_g[l]) * (1.0 + scale1) + shift1
        proj = h @ w_in[l]
        z_gm = jax.nn.gelu(proj[..., :2 * GM_WIDTH], approximate=False)
        o_gm = chunked_spatial_gating(z_gm, v_norm_g[l], w_spatial[l], b_spatial[l])
        qkv = proj[..., 2 * GM_WIDTH:].reshape(B, S, 3, SB_HEADS, SB_HEAD_DIM)
        o_sb = stick_breaking_attention(qkv[:, :, 0], qkv[:, :, 1], qkv[:, :, 2]).reshape(B, S, SB_WIDTH)
        o = jnp.concatenate([o_gm, o_sb], axis=-1)
        o = group_rmsnorm(o, out_norm_g[l], GM_GROUPS + SB_HEADS)
        x = x + gate1 * (o @ w_out[l])

        h = rmsnorm(x, norm2_g[l]) * (1.0 + scale2) + shift2
        f = (jax.nn.silu(h @ w_gate[l]) * (h @ w_up[l])) @ w_down[l]
        x = x + gate2 * f
    return rmsnorm(x, final_g)
```

```python
import functools

import jax
import jax.numpy as jnp
from jax import lax
from jax.experimental import pallas as pl
from jax.experimental.pallas import tpu as pltpu

EPS = 1e-6
N_MOD = 6
SB_HEADS = 8
LANE = 128
VMEM_LIMIT = 56 * 1024 * 1024

F32 = jnp.float32
BF16 = jnp.bfloat16


def _params(*sem):
    return pltpu.CompilerParams(dimension_semantics=sem, vmem_limit_bytes=VMEM_LIMIT)


def _rms(x):
    return x * lax.rsqrt(jnp.mean(x * x, axis=-1, keepdims=True) + EPS)


def _mod_kernel(c_ref, w_ref, b_ref, o_ref):
    c = c_ref[...]
    c_act = (c * jax.nn.sigmoid(c)).astype(BF16)
    o_ref[...] = jnp.dot(c_act, w_ref[...].astype(BF16), preferred_element_type=F32) + b_ref[...]


def _modulation(c, w_ada, b_ada, tn=1024):
    B, D = c.shape
    N = w_ada.shape[1]
    return pl.pallas_call(
        _mod_kernel,
        out_shape=jax.ShapeDtypeStruct((B, N), F32),
        grid=(N // tn,),
        in_specs=[pl.BlockSpec((B, D), lambda j: (0, 0)),
                  pl.BlockSpec((D, tn), lambda j: (0, j)),
                  pl.BlockSpec((1, tn), lambda j: (0, j))],
        out_specs=pl.BlockSpec((B, tn), lambda j: (0, j)),
        compiler_params=_params("parallel"),
        name="modulation",
    )(c, w_ada, b_ada.reshape(1, N))


def _gelu(x):
    return 0.5 * x * (1.0 + lax.erf(x * 0.7071067811865476))


def _in_kernel(x_ref, sc_ref, sh_ref, g_ref, w_ref, vg_ref, o_ref, h_ref, acc_ref,
               *, n_gm, n_seg, q_scale, rc):
    j = pl.program_id(1)
    tm, tn = acc_ref.shape
    ncb = tn // LANE

    @pl.when(j == 0)
    def _():
        h = _rms(x_ref[...]) * g_ref[...]
        h_ref[...] = (h * (1.0 + sc_ref[0]) + sh_ref[0]).astype(BF16)

    acc_ref[...] = jnp.dot(h_ref[...], w_ref[...], preferred_element_type=F32)

    def epilogue(fn):
        def body(r, carry):
            rows = pl.ds(pl.multiple_of(r * rc, rc), rc)
            for cb in range(ncb):
                blk = acc_ref[rows, cb * LANE:(cb + 1) * LANE]
                o_ref[cb, rows, :] = fn(blk, cb).astype(BF16)
            return carry
        lax.fori_loop(0, tm // rc, body, 0)

    @pl.when(j < n_gm)
    def _():
        epilogue(lambda blk, cb: _gelu(blk))

    @pl.when(jnp.logical_and(j >= n_gm, j < 2 * n_gm))
    def _():
        def fn(blk, cb):
            z = _gelu(blk)
            zc = z - jnp.mean(z, axis=-1, keepdims=True)
            return _rms(zc) * vg_ref[0, :, cb * LANE:(cb + 1) * LANE]
        epilogue(fn)

    @pl.when(jnp.logical_and(j >= 2 * n_gm, j < 2 * n_gm + n_seg))
    def _():
        epilogue(lambda blk, cb: blk * q_scale)

    @pl.when(j >= 2 * n_gm + n_seg)
    def _():
        epilogue(lambda blk, cb: blk)


def _in_proj(x2d, scale, shift, norm_g, w_in, v_norm_g, *, seq, gm_width, sb_width, head_dim,
             tm=1024, tn=1024, rc=128):
    T, D = x2d.shape
    N = w_in.shape[1]
    assert gm_width % tn == 0 and sb_width % tn == 0 and seq % tm == 0
    n_gm, n_seg = gm_width // tn, sb_width // tn
    per_b = seq // tm
    kern = functools.partial(_in_kernel, n_gm=n_gm, n_seg=n_seg, q_scale=head_dim ** -0.5, rc=rc)
    return pl.pallas_call(
        kern,
        out_shape=jax.ShapeDtypeStruct((N // LANE, T, LANE), BF16),
        grid=(T // tm, N // tn),
        in_specs=[pl.BlockSpec((tm, D), lambda i, j: (i, 0)),
                  pl.BlockSpec((1, 1, D), lambda i, j: (i // per_b, 0, 0)),
                  pl.BlockSpec((1, 1, D), lambda i, j: (i // per_b, 0, 0)),
                  pl.BlockSpec((1, D), lambda i, j: (0, 0)),
                  pl.BlockSpec((D, tn), lambda i, j: (0, j)),
                  pl.BlockSpec((1, 1, tn), lambda i, j: (jnp.clip(j - n_gm, 0, n_gm - 1), 0, 0))],
        out_specs=pl.BlockSpec((tn // LANE, tm, LANE), lambda i, j: (j, i, 0)),
        scratch_shapes=[pltpu.VMEM((tm, D), BF16), pltpu.VMEM((tm, tn), F32)],
        compiler_params=_params("parallel", "arbitrary"),
        name="in_proj",
    )(x2d, scale, shift, norm_g.reshape(1, D), w_in, v_norm_g.reshape(n_gm, 1, tn))


def _gmlp_kernel(u_ref, v_ref, w_ref, b_ref, g_ref, o_ref, *, chunk):
    groups, tr, _ = u_ref.shape
    n_ch = tr // chunk
    causal = (lax.broadcasted_iota(jnp.int32, (chunk, chunk), 0)
              >= lax.broadcasted_iota(jnp.int32, (chunk, chunk), 1))
    for g in range(groups):
        w = jnp.where(causal, w_ref[g], 0.0).astype(BF16)
        v_cat = jnp.concatenate([v_ref[g, c * chunk:(c + 1) * chunk, :] for c in range(n_ch)], axis=1)
        mixed = jnp.dot(w, v_cat, preferred_element_type=F32) + b_ref[g]
        for c in range(n_ch):
            rows = slice(c * chunk, (c + 1) * chunk)
            o = u_ref[g, rows, :].astype(F32) * mixed[:, c * LANE:(c + 1) * LANE]
            o_ref[g, rows, :] = (_rms(o) * g_ref[g]).astype(BF16)


def _gmlp(proj, w_s, b_s, gain, *, tr=512):
    _, T, _ = proj.shape
    groups, chunk, _ = w_s.shape
    blk = pl.BlockSpec((groups, tr, LANE), lambda r: (0, r, 0))
    return pl.pallas_call(
        functools.partial(_gmlp_kernel, chunk=chunk),
        out_shape=jax.ShapeDtypeStruct((groups, T, LANE), BF16),
        grid=(T // tr,),
        in_specs=[blk,
                  pl.BlockSpec((groups, tr, LANE), lambda r: (1, r, 0)),
                  pl.BlockSpec((groups, chunk, chunk), lambda r: (0, 0, 0)),
                  pl.BlockSpec((groups, chunk, 1), lambda r: (0, 0, 0)),
                  pl.BlockSpec((groups, 1, LANE), lambda r: (0, 0, 0))],
        out_specs=blk,
        compiler_params=_params("parallel"),
        name="gmlp",
    )(proj, proj, w_s, b_s.reshape(groups, chunk, 1), gain.reshape(groups, 1, LANE))


def _attn_kernel(q_ref, k_ref, v_ref, g_ref, o_ref, *, tq):
    S = q_ref.shape[1]
    ri = lax.broadcasted_iota(jnp.int32, (tq, tq), 0)
    ci = lax.broadcasted_iota(jnp.int32, (tq, tq), 1)
    below = ri > ci
    suffix = jnp.where(below, 1.0, 0.0).astype(BF16)
    gain = g_ref[0]

    def block(q, j, carry, acc, diag):
        ks = pl.ds(pl.multiple_of(j * tq, tq), tq)
        z = lax.dot_general(q, k_ref[0, ks, :], (((1,), (1,)), ((), ())), preferred_element_type=F32)
        l1p = jnp.log(1.0 + jnp.exp(-jnp.abs(z)))
        m = jnp.minimum(z, 0.0)
        log_beta = m - l1p
        log_1m = (m - z) - l1p
        if diag:
            log_1m = jnp.where(below, log_1m, 0.0)
        hi = log_1m.astype(BF16)
        lo = (log_1m - hi.astype(F32)).astype(BF16)
        tail = (jnp.dot(hi, suffix, preferred_element_type=F32)
                + jnp.dot(lo, suffix, preferred_element_type=F32))
        a = jnp.exp(log_beta + tail + carry)
        if diag:
            a = jnp.where(below, a, 0.0)
        acc = acc + jnp.dot(a.astype(BF16), v_ref[0, ks, :], preferred_element_type=F32)
        carry = carry + (tail[:, :1] + log_1m[:, :1])
        return carry, acc

    def qblock(i, _):
        qs = pl.ds(pl.multiple_of(i * tq, tq), tq)
        q = q_ref[0, qs, :]
        carry, acc = block(q, i, jnp.zeros((tq, 1), F32), jnp.zeros((tq, LANE), F32), True)
        carry, acc = lax.fori_loop(0, i, lambda jj, st: block(q, i - 1 - jj, st[0], st[1], False),
                                   (carry, acc))
        o_ref[0, qs, :] = (_rms(acc) * gain).astype(BF16)
        return 0

    lax.fori_loop(0, S // tq, qblock, 0)


def _attention(proj, gain, *, batch, seq, heads, q_off, tq=256):
    kern = functools.partial(_attn_kernel, tq=tq)

    def spec(off):
        return pl.BlockSpec((1, seq, LANE), lambda b, h: (off + h, b, 0))

    return pl.pallas_call(
        kern,
        out_shape=jax.ShapeDtypeStruct((heads, batch * seq, LANE), BF16),
        grid=(batch, heads),
        in_specs=[spec(q_off), spec(q_off + heads), spec(q_off + 2 * heads),
                  pl.BlockSpec((1, 1, LANE), lambda b, h: (h, 0, 0))],
        out_specs=pl.BlockSpec((1, seq, LANE), lambda b, h: (h, b, 0)),
        compiler_params=_params("parallel", "parallel"),
        name="sb_attention",
    )(proj, proj, proj, gain.reshape(heads, 1, LANE))


def _out_kernel(x_ref, a_ref, b_ref, w_ref, gate_ref, o_ref):
    pieces = [a_ref[c] for c in range(a_ref.shape[0])] + [b_ref[c] for c in range(b_ref.shape[0])]
    o = jnp.concatenate(pieces, axis=-1)
    y = jnp.dot(o, w_ref[...], preferred_element_type=F32)
    o_ref[...] = x_ref[...] + gate_ref[0] * y


def _out_proj(x2d, o_gm, o_sb, w_out, gate, *, seq, tm=512):
    T, D = x2d.shape
    per_b = seq // tm
    return pl.pallas_call(
        _out_kernel,
        out_shape=jax.ShapeDtypeStruct((T, D), F32),
        grid=(T // tm,),
        in_specs=[pl.BlockSpec((tm, D), lambda i: (i, 0)),
                  pl.BlockSpec((o_gm.shape[0], tm, LANE), lambda i: (0, i, 0)),
                  pl.BlockSpec((o_sb.shape[0], tm, LANE), lambda i: (0, i, 0)),
                  pl.BlockSpec(w_out.shape, lambda i: (0, 0)),
                  pl.BlockSpec((1, 1, D), lambda i: (i // per_b, 0, 0))],
        out_specs=pl.BlockSpec((tm, D), lambda i: (i, 0)),
        compiler_params=_params("parallel"),
        name="out_proj",
    )(x2d, o_gm, o_sb, w_out, gate)


def _ffn_kernel(x_ref, sc_ref, sh_ref, gate_ref, g_ref, wg_ref, wu_ref, wd_ref, fg_ref,
                o_ref, h_ref, acc_ref, *, final_norm):
    j = pl.program_id(1)

    @pl.when(j == 0)
    def _():
        h = _rms(x_ref[...]) * g_ref[...]
        h_ref[...] = (h * (1.0 + sc_ref[0]) + sh_ref[0]).astype(BF16)
        acc_ref[...] = jnp.zeros_like(acc_ref)

    h = h_ref[...]
    gt = jnp.dot(h, wg_ref[...], preferred_element_type=F32)
    up = jnp.dot(h, wu_ref[...], preferred_element_type=F32)
    a = (gt * jax.nn.sigmoid(gt) * up).astype(BF16)
    acc_ref[...] += jnp.dot(a, wd_ref[...], preferred_element_type=F32)

    @pl.when(j == pl.num_programs(1) - 1)
    def _():
        y = x_ref[...] + gate_ref[0] * acc_ref[...]
        if final_norm:
            y = _rms(y) * fg_ref[...]
        o_ref[...] = y


def _ffn(x2d, scale, shift, gate, norm_g, w_gate, w_up, w_down, final_g, *, seq, final_norm,
         tm=512, tf=512):
    T, D = x2d.shape
    FF = w_gate.shape[1]
    per_b = seq // tm
    mod_spec = pl.BlockSpec((1, 1, D), lambda i, j: (i // per_b, 0, 0))
    vec_spec = pl.BlockSpec((1, D), lambda i, j: (0, 0))
    return pl.pallas_call(
        functools.partial(_ffn_kernel, final_norm=final_norm),
        out_shape=jax.ShapeDtypeStruct((T, D), F32),
        grid=(T // tm, FF // tf),
        in_specs=[pl.BlockSpec((tm, D), lambda i, j: (i, 0)),
                  mod_spec, mod_spec, mod_spec, vec_spec,
                  pl.BlockSpec((D, tf), lambda i, j: (0, j)),
                  pl.BlockSpec((D, tf), lambda i, j: (0, j)),
                  pl.BlockSpec((tf, D), lambda i, j: (j, 0)),
                  vec_spec],
        out_specs=pl.BlockSpec((tm, D), lambda i, j: (i, 0)),
        scratch_shapes=[pltpu.VMEM((tm, D), BF16), pltpu.VMEM((tm, D), F32)],
        compiler_params=_params("parallel", "arbitrary"),
        name="ffn",
    )(x2d, scale, shift, gate, norm_g.reshape(1, D), w_gate, w_up, w_down, final_g.reshape(1, D))


def kernel(x, c, w_ada, b_ada, norm1_g, w_in, v_norm_g, w_spatial, b_spatial, out_norm_g, w_out,
           norm2_g, w_gate, w_up, w_down, final_g):
    B, S, D = x.shape
    depth = w_ada.shape[0]
    gm_width = v_norm_g.shape[-1]
    sb_width = (w_in.shape[-1] - 2 * gm_width) // 3
    head_dim = sb_width // SB_HEADS
    assert head_dim == LANE and gm_width // w_spatial.shape[1] == LANE

    xf = x.reshape(B * S, D)
    for l in range(depth):
        mod = _modulation(c, w_ada[l], b_ada[l]).reshape(B, N_MOD, 1, D)
        shift1, scale1, gate1, shift2, scale2, gate2 = [mod[:, m] for m in range(N_MOD)]

        proj = _in_proj(xf, scale1, shift1, norm1_g[l], w_in[l].astype(BF16), v_norm_g[l],
                        seq=S, gm_width=gm_width, sb_width=sb_width, head_dim=head_dim)
        o_gm = _gmlp(proj, w_spatial[l], b_spatial[l], out_norm_g[l, :gm_width])
        o_sb = _attention(proj, out_norm_g[l, gm_width:], batch=B, seq=S, heads=SB_HEADS,
                          q_off=2 * gm_width // LANE)
        xf = _out_proj(xf, o_gm, o_sb, w_out[l].astype(BF16), gate1, seq=S)
        xf = _ffn(xf, scale2, shift2, gate2, norm2_g[l], w_gate[l].astype(BF16),
                  w_up[l].astype(BF16), w_down[l].astype(BF16), final_g,
                  seq=S, final_norm=(l == depth - 1))
    return xf.reshape(B, S, D)
```

```python
import functools

import jax
import jax.numpy as jnp
from jax import lax
from jax.experimental import pallas as pl
from jax.experimental.pallas import tpu as pltpu

EPS = 1e-6
N_MOD = 6
SB_HEADS = 8
LANE = 128
VMEM_LIMIT = 56 * 1024 * 1024
LOG2E = 1.4426950408889634
DEAD_LOG2 = -150.0

F32 = jnp.float32
BF16 = jnp.bfloat16


def _params(*sem):
    return pltpu.CompilerParams(dimension_semantics=sem, vmem_limit_bytes=VMEM_LIMIT)


def _rms(x):
    return x * lax.rsqrt(jnp.mean(x * x, axis=-1, keepdims=True) + EPS)


def _mod_kernel(c_ref, w_ref, b_ref, o_ref):
    c = c_ref[...]
    c_act = (c * jax.nn.sigmoid(c)).astype(BF16)
    o_ref[...] = jnp.dot(c_act, w_ref[...].astype(BF16), preferred_element_type=F32) + b_ref[...]


def _modulation(c, w_ada, b_ada, tn=1024):
    B, D = c.shape
    N = w_ada.shape[1]
    return pl.pallas_call(
        _mod_kernel,
        out_shape=jax.ShapeDtypeStruct((B, N), F32),
        grid=(N // tn,),
        in_specs=[pl.BlockSpec((B, D), lambda j: (0, 0)),
                  pl.BlockSpec((D, tn), lambda j: (0, j)),
                  pl.BlockSpec((1, tn), lambda j: (0, j))],
        out_specs=pl.BlockSpec((B, tn), lambda j: (0, j)),
        compiler_params=_params("parallel"),
        name="modulation",
    )(c, w_ada, b_ada.reshape(1, N))


def _gelu(x):
    return 0.5 * x * (1.0 + lax.erf(x * 0.7071067811865476))


def _in_kernel(x_ref, sc_ref, sh_ref, g_ref, w_ref, vg_ref, o_ref, h_ref, acc_ref,
               *, n_gm, n_seg, q_scale, rc):
    j = pl.program_id(1)
    tm, tn = acc_ref.shape
    ncb = tn // LANE

    @pl.when(j == 0)
    def _():
        h = _rms(x_ref[...]) * g_ref[...]
        h_ref[...] = (h * (1.0 + sc_ref[0]) + sh_ref[0]).astype(BF16)

    acc_ref[...] = jnp.dot(h_ref[...], w_ref[...], preferred_element_type=F32)

    def epilogue(fn):
        def body(r, carry):
            rows = pl.ds(pl.multiple_of(r * rc, rc), rc)
            for cb in range(ncb):
                blk = acc_ref[rows, cb * LANE:(cb + 1) * LANE]
                o_ref[cb, rows, :] = fn(blk, cb).astype(BF16)
            return carry
        lax.fori_loop(0, tm // rc, body, 0)

    @pl.when(j < n_gm)
    def _():
        epilogue(lambda blk, cb: _gelu(blk))

    @pl.when(jnp.logical_and(j >= n_gm, j < 2 * n_gm))
    def _():
        def fn(blk, cb):
            z = _gelu(blk)
            zc = z - jnp.mean(z, axis=-1, keepdims=True)
            return _rms(zc) * vg_ref[0, :, cb * LANE:(cb + 1) * LANE]
        epilogue(fn)

    @pl.when(jnp.logical_and(j >= 2 * n_gm, j < 2 * n_gm + n_seg))
    def _():
        epilogue(lambda blk, cb: blk * q_scale)

    @pl.when(j >= 2 * n_gm + n_seg)
    def _():
        epilogue(lambda blk, cb: blk)


def _in_proj(x2d, scale, shift, norm_g, w_in, v_norm_g, *, seq, gm_width, sb_width, head_dim,
             tm=1024, tn=1024, rc=128):
    T, D = x2d.shape
    N = w_in.shape[1]
    assert gm_width % tn == 0 and sb_width % tn == 0 and seq % tm == 0
    n_gm, n_seg = gm_width // tn, sb_width // tn
    per_b = seq // tm
    kern = functools.partial(_in_kernel, n_gm=n_gm, n_seg=n_seg, q_scale=LOG2E * head_dim ** -0.5, rc=rc)
    return pl.pallas_call(
        kern,
        out_shape=jax.ShapeDtypeStruct((N // LANE, T, LANE), BF16),
        grid=(T // tm, N // tn),
        in_specs=[pl.BlockSpec((tm, D), lambda i, j: (i, 0)),
                  pl.BlockSpec((1, 1, D), lambda i, j: (i // per_b, 0, 0)),
                  pl.BlockSpec((1, 1, D), lambda i, j: (i // per_b, 0, 0)),
                  pl.BlockSpec((1, D), lambda i, j: (0, 0)),
                  pl.BlockSpec((D, tn), lambda i, j: (0, j)),
                  pl.BlockSpec((1, 1, tn), lambda i, j: (jnp.clip(j - n_gm, 0, n_gm - 1), 0, 0))],
        out_specs=pl.BlockSpec((tn // LANE, tm, LANE), lambda i, j: (j, i, 0)),
        scratch_shapes=[pltpu.VMEM((tm, D), BF16), pltpu.VMEM((tm, tn), F32)],
        compiler_params=_params("parallel", "arbitrary"),
        name="in_proj",
    )(x2d, scale, shift, norm_g.reshape(1, D), w_in, v_norm_g.reshape(n_gm, 1, tn))


def _gmlp_kernel(u_ref, v_ref, w_ref, b_ref, g_ref, o_ref, *, chunk):
    groups, tr, _ = u_ref.shape
    n_ch = tr // chunk
    causal = (lax.broadcasted_iota(jnp.int32, (chunk, chunk), 0)
              >= lax.broadcasted_iota(jnp.int32, (chunk, chunk), 1))
    for g in range(groups):
        w = jnp.where(causal, w_ref[g], 0.0).astype(BF16)
        v_cat = jnp.concatenate([v_ref[g, c * chunk:(c + 1) * chunk, :] for c in range(n_ch)], axis=1)
        mixed = jnp.dot(w, v_cat, preferred_element_type=F32) + b_ref[g]
        for c in range(n_ch):
            rows = slice(c * chunk, (c + 1) * chunk)
            o = u_ref[g, rows, :].astype(F32) * mixed[:, c * LANE:(c + 1) * LANE]
            o_ref[g, rows, :] = (_rms(o) * g_ref[g]).astype(BF16)


def _gmlp(proj, w_s, b_s, gain, *, tr=512):
    _, T, _ = proj.shape
    groups, chunk, _ = w_s.shape
    blk = pl.BlockSpec((groups, tr, LANE), lambda r: (0, r, 0))
    return pl.pallas_call(
        functools.partial(_gmlp_kernel, chunk=chunk),
        out_shape=jax.ShapeDtypeStruct((groups, T, LANE), BF16),
        grid=(T // tr,),
        in_specs=[blk,
                  pl.BlockSpec((groups, tr, LANE), lambda r: (1, r, 0)),
                  pl.BlockSpec((groups, chunk, chunk), lambda r: (0, 0, 0)),
                  pl.BlockSpec((groups, chunk, 1), lambda r: (0, 0, 0)),
                  pl.BlockSpec((groups, 1, LANE), lambda r: (0, 0, 0))],
        out_specs=blk,
        compiler_params=_params("parallel"),
        name="gmlp",
    )(proj, proj, w_s, b_s.reshape(groups, chunk, 1), gain.reshape(groups, 1, LANE))


def _attn_kernel(q_ref, k_ref, v_ref, g_ref, o_ref, acc_ref, car_ref, *, tq):
    S = q_ref.shape[1]
    nq = S // tq
    ri = lax.broadcasted_iota(jnp.int32, (tq, tq), 0)
    ci = lax.broadcasted_iota(jnp.int32, (tq, tq), 1)
    below = ri > ci
    suffix = jnp.where(below, 1.0, 0.0).astype(BF16)

    def logits(q, k):
        z = lax.dot_general(q, k, (((1,), (1,)), ((), ())), preferred_element_type=F32)
        m = jnp.minimum(z, 0.0)
        d = m - z
        l = jnp.log(1.0 + jnp.exp2(m + d)) * LOG2E
        return m - l, d - l

    car_up = acc_up = None
    for j in reversed(range(nq)):
        two = j + 1 < nq
        kj = slice(j * tq, (j + 1) * tq)
        q = q_ref[0, j * tq:(j + 2) * tq, :] if two else q_ref[0, kj, :]
        lb, lm = logits(q, k_ref[0, kj, :])
        lm_diag = jnp.where(below, lm[:tq], 0.0)
        lm = jnp.concatenate([lm_diag, lm[tq:]], axis=0) if two else lm_diag
        tail = jnp.dot(lm.astype(BF16), suffix, preferred_element_type=F32)
        x = lb + tail
        a = jnp.where(below, jnp.exp2(x[:tq]), 0.0)
        if two:
            a = jnp.concatenate([a, jnp.exp2(x[tq:] + car_up)], axis=0)
        pv = jnp.dot(a.astype(BF16), v_ref[0, kj, :], preferred_element_type=F32)
        rowsum = tail[:, :1] + lm[:, :1]
        if two:
            up = slice((j + 1) * tq, (j + 2) * tq)
            acc_ref[up, :] = acc_up + pv[tq:]
            car_ref[up, :] = car_up + rowsum[tq:]
        car_up, acc_up = rowsum[:tq], pv[:tq]
    acc_ref[:tq, :] = acc_up
    car_ref[:tq, :] = car_up

    if nq > 2:
        @pl.when(jnp.max(car_ref[2 * tq:, :]) > DEAD_LOG2)
        def _():
            def qblock(i, _):
                rows = pl.ds(pl.multiple_of(i * tq, tq), tq)
                q = q_ref[0, rows, :]

                def alive(st):
                    return jnp.logical_and(st[0] >= 0, jnp.max(st[1]) > DEAD_LOG2)

                def visit(st):
                    j, car, acc = st
                    ks = pl.ds(pl.multiple_of(j * tq, tq), tq)
                    lb, lm = logits(q, k_ref[0, ks, :])
                    tail = jnp.dot(lm.astype(BF16), suffix, preferred_element_type=F32)
                    a = jnp.exp2(lb + tail + car)
                    acc = acc + jnp.dot(a.astype(BF16), v_ref[0, ks, :], preferred_element_type=F32)
                    return j - 1, car + (tail[:, :1] + lm[:, :1]), acc

                _, car, acc = lax.while_loop(alive, visit, (i - 2, car_ref[rows, :], acc_ref[rows, :]))
                acc_ref[rows, :] = acc
                car_ref[rows, :] = car
                return 0
            lax.fori_loop(2, nq, qblock, 0)

    gain = g_ref[0]
    for i in range(nq):
        rows = slice(i * tq, (i + 1) * tq)
        o_ref[0, rows, :] = (_rms(acc_ref[rows, :]) * gain).astype(BF16)


def _attention(proj, gain, *, batch, seq, heads, q_off, tq=256):
    kern = functools.partial(_attn_kernel, tq=tq)

    def spec(off):
        return pl.BlockSpec((1, seq, LANE), lambda b, h: (off + h, b, 0))

    return pl.pallas_call(
        kern,
        out_shape=jax.ShapeDtypeStruct((heads, batch * seq, LANE), BF16),
        grid=(batch, heads),
        in_specs=[spec(q_off), spec(q_off + heads), spec(q_off + 2 * heads),
                  pl.BlockSpec((1, 1, LANE), lambda b, h: (h, 0, 0))],
        out_specs=pl.BlockSpec((1, seq, LANE), lambda b, h: (h, b, 0)),
        scratch_shapes=[pltpu.VMEM((seq, LANE), F32), pltpu.VMEM((seq, 1), F32)],
        compiler_params=_params("parallel", "parallel"),
        name="sb_attention",
    )(proj, proj, proj, gain.reshape(heads, 1, LANE))


def _out_kernel(x_ref, a_ref, b_ref, w_ref, gate_ref, o_ref):
    pieces = [a_ref[c] for c in range(a_ref.shape[0])] + [b_ref[c] for c in range(b_ref.shape[0])]
    o = jnp.concatenate(pieces, axis=-1)
    y = jnp.dot(o, w_ref[...], preferred_element_type=F32)
    o_ref[...] = x_ref[...] + gate_ref[0] * y


def _out_proj(x2d, o_gm, o_sb, w_out, gate, *, seq, tm=512):
    T, D = x2d.shape
    per_b = seq // tm
    return pl.pallas_call(
        _out_kernel,
        out_shape=jax.ShapeDtypeStruct((T, D), F32),
        grid=(T // tm,),
        in_specs=[pl.BlockSpec((tm, D), lambda i: (i, 0)),
                  pl.BlockSpec((o_gm.shape[0], tm, LANE), lambda i: (0, i, 0)),
                  pl.BlockSpec((o_sb.shape[0], tm, LANE), lambda i: (0, i, 0)),
                  pl.BlockSpec(w_out.shape, lambda i: (0, 0)),
                  pl.BlockSpec((1, 1, D), lambda i: (i // per_b, 0, 0))],
        out_specs=pl.BlockSpec((tm, D), lambda i: (i, 0)),
        compiler_params=_params("parallel"),
        name="out_proj",
    )(x2d, o_gm, o_sb, w_out, gate)


def _ffn_kernel(x_ref, sc_ref, sh_ref, gate_ref, g_ref, wg_ref, wu_ref, wd_ref, fg_ref,
                o_ref, h_ref, acc_ref, *, final_norm):
    j = pl.program_id(1)

    @pl.when(j == 0)
    def _():
        h = _rms(x_ref[...]) * g_ref[...]
        h_ref[...] = (h * (1.0 + sc_ref[0]) + sh_ref[0]).astype(BF16)
        acc_ref[...] = jnp.zeros_like(acc_ref)

    h = h_ref[...]
    gt = jnp.dot(h, wg_ref[...], preferred_element_type=F32)
    up = jnp.dot(h, wu_ref[...], preferred_element_type=F32)
    a = (gt * jax.nn.sigmoid(gt) * up).astype(BF16)
    acc_ref[...] += jnp.dot(a, wd_ref[...], preferred_element_type=F32)

    @pl.when(j == pl.num_programs(1) - 1)
    def _():
        y = x_ref[...] + gate_ref[0] * acc_ref[...]
        if final_norm:
            y = _rms(y) * fg_ref[...]
        o_ref[...] = y


def _ffn(x2d, scale, shift, gate, norm_g, w_gate, w_up, w_down, final_g, *, seq, final_norm,
         tm=512, tf=512):
    T, D = x2d.shape
    FF = w_gate.shape[1]
    per_b = seq // tm
    mod_spec = pl.BlockSpec((1, 1, D), lambda i, j: (i // per_b, 0, 0))
    vec_spec = pl.BlockSpec((1, D), lambda i, j: (0, 0))
    return pl.pallas_call(
        functools.partial(_ffn_kernel, final_norm=final_norm),
        out_shape=jax.ShapeDtypeStruct((T, D), F32),
        grid=(T // tm, FF // tf),
        in_specs=[pl.BlockSpec((tm, D), lambda i, j: (i, 0)),
                  mod_spec, mod_spec, mod_spec, vec_spec,
                  pl.BlockSpec((D, tf), lambda i, j: (0, j)),
                  pl.BlockSpec((D, tf), lambda i, j: (0, j)),
                  pl.BlockSpec((tf, D), lambda i, j: (j, 0)),
                  vec_spec],
        out_specs=pl.BlockSpec((tm, D), lambda i, j: (i, 0)),
        scratch_shapes=[pltpu.VMEM((tm, D), BF16), pltpu.VMEM((tm, D), F32)],
        compiler_params=_params("parallel", "arbitrary"),
        name="ffn",
    )(x2d, scale, shift, gate, norm_g.reshape(1, D), w_gate, w_up, w_down, final_g.reshape(1, D))


def kernel(x, c, w_ada, b_ada, norm1_g, w_in, v_norm_g, w_spatial, b_spatial, out_norm_g, w_out,
           norm2_g, w_gate, w_up, w_down, final_g):
    B, S, D = x.shape
    depth = w_ada.shape[0]
    gm_width = v_norm_g.shape[-1]
    sb_width = (w_in.shape[-1] - 2 * gm_width) // 3
    head_dim = sb_width // SB_HEADS
    assert head_dim == LANE and gm_width // w_spatial.shape[1] == LANE

    xf = x.reshape(B * S, D)
    for l in range(depth):
        mod = _modulation(c, w_ada[l], b_ada[l]).reshape(B, N_MOD, 1, D)
        shift1, scale1, gate1, shift2, scale2, gate2 = [mod[:, m] for m in range(N_MOD)]

        proj = _in_proj(xf, scale1, shift1, norm1_g[l], w_in[l].astype(BF16), v_norm_g[l],
                        seq=S, gm_width=gm_width, sb_width=sb_width, head_dim=head_dim)
        o_gm = _gmlp(proj, w_spatial[l], b_spatial[l], out_norm_g[l, :gm_width])
        o_sb = _attention(proj, out_norm_g[l, gm_width:], batch=B, seq=S, heads=SB_HEADS,
                          q_off=2 * gm_width // LANE)
        xf = _out_proj(xf, o_gm, o_sb, w_out[l].astype(BF16), gate1, seq=S)
        xf = _ffn(xf, scale2, shift2, gate2, norm2_g[l], w_gate[l].astype(BF16),
                  w_up[l].astype(BF16), w_down[l].astype(BF16), final_g,
                  seq=S, final_norm=(l == depth - 1))
    return xf.reshape(B, S, D)
```

```python
import functools

import jax
import jax.numpy as jnp
from jax import lax
from jax.experimental import pallas as pl
from jax.experimental.pallas import tpu as pltpu

EPS = 1e-6
N_MOD = 6
SB_HEADS = 8
LANE = 128
VMEM_LIMIT = 56 * 1024 * 1024
LOG2E = 1.4426950408889634
DEAD_LOG2 = -150.0

F32 = jnp.float32
BF16 = jnp.bfloat16


def _params(*sem):
    return pltpu.CompilerParams(dimension_semantics=sem, vmem_limit_bytes=VMEM_LIMIT)


def _rms(x):
    return x * lax.rsqrt(jnp.mean(x * x, axis=-1, keepdims=True) + EPS)


def _mod_kernel(c_ref, w_ref, b_ref, o_ref):
    c = c_ref[...]
    c_act = (c * jax.nn.sigmoid(c)).astype(BF16)
    o_ref[...] = jnp.dot(c_act, w_ref[...].astype(BF16), preferred_element_type=F32) + b_ref[...]


def _modulation(c, w_ada, b_ada, tn=1024):
    B, D = c.shape
    N = w_ada.shape[1]
    return pl.pallas_call(
        _mod_kernel,
        out_shape=jax.ShapeDtypeStruct((B, N), F32),
        grid=(N // tn,),
        in_specs=[pl.BlockSpec((B, D), lambda j: (0, 0)),
                  pl.BlockSpec((D, tn), lambda j: (0, j)),
                  pl.BlockSpec((1, tn), lambda j: (0, j))],
        out_specs=pl.BlockSpec((B, tn), lambda j: (0, j)),
        compiler_params=_params("parallel"),
        name="modulation",
    )(c, w_ada, b_ada.reshape(1, N))


def _gelu(x):
    return 0.5 * x * (1.0 + lax.erf(x * 0.7071067811865476))


def _in_kernel(x_ref, sc_ref, sh_ref, g_ref, w_ref, vg_ref, o_ref, h_ref, acc_ref,
               *, n_gm, n_seg, q_scale, rc, pc):
    j = pl.program_id(1)
    tm, tn = acc_ref.shape
    ncb = tn // LANE

    @pl.when(j == 0)
    def _():
        gs = g_ref[...] * (1.0 + sc_ref[0])
        sh = sh_ref[0]

        def chunk(r, carry):
            rows = pl.ds(pl.multiple_of(r * pc, pc), pc)
            h_ref[rows, :] = (_rms(x_ref[rows, :]) * gs + sh).astype(BF16)
            return carry
        lax.fori_loop(0, tm // pc, chunk, 0, unroll=4)

    acc_ref[...] = jnp.dot(h_ref[...], w_ref[...], preferred_element_type=F32)

    def epilogue(fn):
        def body(r, carry):
            rows = pl.ds(pl.multiple_of(r * rc, rc), rc)
            for cb in range(ncb):
                blk = acc_ref[rows, cb * LANE:(cb + 1) * LANE]
                o_ref[cb, rows, :] = fn(blk, cb).astype(BF16)
            return carry
        lax.fori_loop(0, tm // rc, body, 0)

    @pl.when(j < n_gm)
    def _():
        epilogue(lambda blk, cb: _gelu(blk))

    @pl.when(jnp.logical_and(j >= n_gm, j < 2 * n_gm))
    def _():
        def fn(blk, cb):
            z = _gelu(blk)
            zc = z - jnp.mean(z, axis=-1, keepdims=True)
            return _rms(zc) * vg_ref[0, :, cb * LANE:(cb + 1) * LANE]
        epilogue(fn)

    @pl.when(jnp.logical_and(j >= 2 * n_gm, j < 2 * n_gm + n_seg))
    def _():
        epilogue(lambda blk, cb: blk * q_scale)

    @pl.when(j >= 2 * n_gm + n_seg)
    def _():
        epilogue(lambda blk, cb: blk)


def _in_proj(x2d, scale, shift, norm_g, w_in, v_norm_g, *, seq, gm_width, sb_width, head_dim,
             tm=1024, tn=1024, rc=128, pc=16):
    T, D = x2d.shape
    N = w_in.shape[1]
    assert gm_width % tn == 0 and sb_width % tn == 0 and seq % tm == 0
    n_gm, n_seg = gm_width // tn, sb_width // tn
    per_b = seq // tm
    kern = functools.partial(_in_kernel, n_gm=n_gm, n_seg=n_seg, q_scale=LOG2E * head_dim ** -0.5, rc=rc,
                             pc=pc)
    return pl.pallas_call(
        kern,
        out_shape=jax.ShapeDtypeStruct((N // LANE, T, LANE), BF16),
        grid=(T // tm, N // tn),
        in_specs=[pl.BlockSpec((tm, D), lambda i, j: (i, 0)),
                  pl.BlockSpec((1, 1, D), lambda i, j: (i // per_b, 0, 0)),
                  pl.BlockSpec((1, 1, D), lambda i, j: (i // per_b, 0, 0)),
                  pl.BlockSpec((1, D), lambda i, j: (0, 0)),
                  pl.BlockSpec((D, tn), lambda i, j: (0, j)),
                  pl.BlockSpec((1, 1, tn), lambda i, j: (jnp.clip(j - n_gm, 0, n_gm - 1), 0, 0))],
        out_specs=pl.BlockSpec((tn // LANE, tm, LANE), lambda i, j: (j, i, 0)),
        scratch_shapes=[pltpu.VMEM((tm, D), BF16), pltpu.VMEM((tm, tn), F32)],
        compiler_params=_params("parallel", "arbitrary"),
        name="in_proj",
    )(x2d, scale, shift, norm_g.reshape(1, D), w_in, v_norm_g.reshape(n_gm, 1, tn))


def _gmlp_kernel(u_ref, v_ref, w_ref, b_ref, g_ref, o_ref, *, chunk):
    groups, tr, _ = u_ref.shape
    n_ch = tr // chunk
    causal = (lax.broadcasted_iota(jnp.int32, (chunk, chunk), 0)
              >= lax.broadcasted_iota(jnp.int32, (chunk, chunk), 1))
    for g in range(groups):
        w = jnp.where(causal, w_ref[g], 0.0).astype(BF16)
        v_cat = jnp.concatenate([v_ref[g, c * chunk:(c + 1) * chunk, :] for c in range(n_ch)], axis=1)
        mixed = jnp.dot(w, v_cat, preferred_element_type=F32) + b_ref[g]
        for c in range(n_ch):
            rows = slice(c * chunk, (c + 1) * chunk)
            o = u_ref[g, rows, :].astype(F32) * mixed[:, c * LANE:(c + 1) * LANE]
            o_ref[g, rows, :] = (_rms(o) * g_ref[g]).astype(BF16)


def _gmlp(proj, w_s, b_s, gain, *, tr=512):
    _, T, _ = proj.shape
    groups, chunk, _ = w_s.shape
    blk = pl.BlockSpec((groups, tr, LANE), lambda r: (0, r, 0))
    return pl.pallas_call(
        functools.partial(_gmlp_kernel, chunk=chunk),
        out_shape=jax.ShapeDtypeStruct((groups, T, LANE), BF16),
        grid=(T // tr,),
        in_specs=[blk,
                  pl.BlockSpec((groups, tr, LANE), lambda r: (1, r, 0)),
                  pl.BlockSpec((groups, chunk, chunk), lambda r: (0, 0, 0)),
                  pl.BlockSpec((groups, chunk, 1), lambda r: (0, 0, 0)),
                  pl.BlockSpec((groups, 1, LANE), lambda r: (0, 0, 0))],
        out_specs=blk,
        compiler_params=_params("parallel"),
        name="gmlp",
    )(proj, proj, w_s, b_s.reshape(groups, chunk, 1), gain.reshape(groups, 1, LANE))


def _attn_kernel(q_ref, k_ref, v_ref, g_ref, o_ref, acc_ref, car_ref, *, tq):
    S = q_ref.shape[1]
    nq = S // tq
    ri = lax.broadcasted_iota(jnp.int32, (tq, tq), 0)
    ci = lax.broadcasted_iota(jnp.int32, (tq, tq), 1)
    below = ri > ci
    suffix = jnp.where(below, 1.0, 0.0).astype(BF16)

    def logits(q, k):
        z = lax.dot_general(q, k, (((1,), (1,)), ((), ())), preferred_element_type=F32)
        m = jnp.minimum(z, 0.0)
        d = m - z
        l = jnp.log(1.0 + jnp.exp2(m + d)) * LOG2E
        return m - l, d - l

    car_up = acc_up = None
    for j in reversed(range(nq)):
        two = j + 1 < nq
        kj = slice(j * tq, (j + 1) * tq)
        q = q_ref[0, j * tq:(j + 2) * tq, :] if two else q_ref[0, kj, :]
        lb, lm = logits(q, k_ref[0, kj, :])
        lm_diag = jnp.where(below, lm[:tq], 0.0)
        lm = jnp.concatenate([lm_diag, lm[tq:]], axis=0) if two else lm_diag
        tail = jnp.dot(lm.astype(BF16), suffix, preferred_element_type=F32)
        x = lb + tail
        a = jnp.where(below, jnp.exp2(x[:tq]), 0.0)
        if two:
            a = jnp.concatenate([a, jnp.exp2(x[tq:] + car_up)], axis=0)
        pv = jnp.dot(a.astype(BF16), v_ref[0, kj, :], preferred_element_type=F32)
        rowsum = tail[:, :1] + lm[:, :1]
        if two:
            up = slice((j + 1) * tq, (j + 2) * tq)
            acc_ref[up, :] = acc_up + pv[tq:]
            car_ref[up, :] = car_up + rowsum[tq:]
        car_up, acc_up = rowsum[:tq], pv[:tq]
    acc_ref[:tq, :] = acc_up
    car_ref[:tq, :] = car_up

    if nq > 2:
        @pl.when(jnp.max(car_ref[2 * tq:, :]) > DEAD_LOG2)
        def _():
            def qblock(i, _):
                rows = pl.ds(pl.multiple_of(i * tq, tq), tq)
                q = q_ref[0, rows, :]

                def alive(st):
                    return jnp.logical_and(st[0] >= 0, jnp.max(st[1]) > DEAD_LOG2)

                def visit(st):
                    j, car, acc = st
                    ks = pl.ds(pl.multiple_of(j * tq, tq), tq)
                    lb, lm = logits(q, k_ref[0, ks, :])
                    tail = jnp.dot(lm.astype(BF16), suffix, preferred_element_type=F32)
                    a = jnp.exp2(lb + tail + car)
                    acc = acc + jnp.dot(a.astype(BF16), v_ref[0, ks, :], preferred_element_type=F32)
                    return j - 1, car + (tail[:, :1] + lm[:, :1]), acc

                _, car, acc = lax.while_loop(alive, visit, (i - 2, car_ref[rows, :], acc_ref[rows, :]))
                acc_ref[rows, :] = acc
                car_ref[rows, :] = car
                return 0
            lax.fori_loop(2, nq, qblock, 0)

    gain = g_ref[0]
    for i in range(nq):
        rows = slice(i * tq, (i + 1) * tq)
        o_ref[0, rows, :] = (_rms(acc_ref[rows, :]) * gain).astype(BF16)


def _attention(proj, gain, *, batch, seq, heads, q_off, tq=256):
    kern = functools.partial(_attn_kernel, tq=tq)

    def spec(off):
        return pl.BlockSpec((1, seq, LANE), lambda b, h: (off + h, b, 0))

    return pl.pallas_call(
        kern,
        out_shape=jax.ShapeDtypeStruct((heads, batch * seq, LANE), BF16),
        grid=(batch, heads),
        in_specs=[spec(q_off), spec(q_off + heads), spec(q_off + 2 * heads),
                  pl.BlockSpec((1, 1, LANE), lambda b, h: (h, 0, 0))],
        out_specs=pl.BlockSpec((1, seq, LANE), lambda b, h: (h, b, 0)),
        scratch_shapes=[pltpu.VMEM((seq, LANE), F32), pltpu.VMEM((seq, 1), F32)],
        compiler_params=_params("parallel", "parallel"),
        name="sb_attention",
    )(proj, proj, proj, gain.reshape(heads, 1, LANE))


def _out_kernel(x_ref, a_ref, b_ref, w_ref, gate_ref, o_ref):
    pieces = [a_ref[c] for c in range(a_ref.shape[0])] + [b_ref[c] for c in range(b_ref.shape[0])]
    o = jnp.concatenate(pieces, axis=-1)
    y = jnp.dot(o, w_ref[...], preferred_element_type=F32)
    o_ref[...] = x_ref[...] + gate_ref[0] * y


def _out_proj(x2d, o_gm, o_sb, w_out, gate, *, seq, tm=512):
    T, D = x2d.shape
    per_b = seq // tm
    return pl.pallas_call(
        _out_kernel,
        out_shape=jax.ShapeDtypeStruct((T, D), F32),
        grid=(T // tm,),
        in_specs=[pl.BlockSpec((tm, D), lambda i: (i, 0)),
                  pl.BlockSpec((o_gm.shape[0], tm, LANE), lambda i: (0, i, 0)),
                  pl.BlockSpec((o_sb.shape[0], tm, LANE), lambda i: (0, i, 0)),
                  pl.BlockSpec(w_out.shape, lambda i: (0, 0)),
                  pl.BlockSpec((1, 1, D), lambda i: (i // per_b, 0, 0))],
        out_specs=pl.BlockSpec((tm, D), lambda i: (i, 0)),
        compiler_params=_params("parallel"),
        name="out_proj",
    )(x2d, o_gm, o_sb, w_out, gate)


def _ffn_kernel(x_hbm, sc_ref, sh_ref, gate_ref, g_ref, wg_ref, wu_ref, wd_ref, fg_ref,
                o_ref, xbuf, h_ref, sem, *, final_norm, rc):
    i, j = pl.program_id(0), pl.program_id(1)
    tm = xbuf.shape[0]

    def x_copy(tile):
        return pltpu.make_async_copy(x_hbm.at[pl.ds(tile * tm, tm), :], xbuf, sem)

    @pl.when(j == 0)
    def _():
        @pl.when(i == 0)
        def _():
            x_copy(0).start()
        x_copy(i).wait()
        gs = g_ref[...] * (1.0 + sc_ref[0])
        sh = sh_ref[0]

        def chunk(r, carry):
            rows = pl.ds(pl.multiple_of(r * rc, rc), rc)
            x = xbuf[rows, :]
            h_ref[rows, :] = (_rms(x) * gs + sh).astype(BF16)
            o_ref[rows, :] = x
            return carry
        lax.fori_loop(0, tm // rc, chunk, 0, unroll=4)

    @pl.when(jnp.logical_and(j == 1, i + 1 < pl.num_programs(0)))
    def _():
        x_copy(i + 1).start()

    h = h_ref[...]
    gt = jnp.dot(h, wg_ref[...], preferred_element_type=F32)
    up = jnp.dot(h, wu_ref[...], preferred_element_type=F32)
    a = (gt * jax.nn.sigmoid(gt) * up).astype(BF16)
    o_ref[...] += gate_ref[0] * jnp.dot(a, wd_ref[...], preferred_element_type=F32)

    if final_norm:
        @pl.when(j == pl.num_programs(1) - 1)
        def _():
            o_ref[...] = _rms(o_ref[...]) * fg_ref[...]


def _ffn(x2d, scale, shift, gate, norm_g, w_gate, w_up, w_down, final_g, *, seq, final_norm,
         tm=1024, tf=512, rc=16):
    T, D = x2d.shape
    FF = w_gate.shape[1]
    assert FF // tf >= 2
    per_b = seq // tm
    mod_spec = pl.BlockSpec((1, 1, D), lambda i, j: (i // per_b, 0, 0))
    vec_spec = pl.BlockSpec((1, D), lambda i, j: (0, 0))
    return pl.pallas_call(
        functools.partial(_ffn_kernel, final_norm=final_norm, rc=rc),
        out_shape=jax.ShapeDtypeStruct((T, D), F32),
        grid=(T // tm, FF // tf),
        in_specs=[pl.BlockSpec(memory_space=pl.ANY),
                  mod_spec, mod_spec, mod_spec, vec_spec,
                  pl.BlockSpec((D, tf), lambda i, j: (0, j)),
                  pl.BlockSpec((D, tf), lambda i, j: (0, j)),
                  pl.BlockSpec((tf, D), lambda i, j: (j, 0)),
                  vec_spec],
        out_specs=pl.BlockSpec((tm, D), lambda i, j: (i, 0)),
        scratch_shapes=[pltpu.VMEM((tm, D), F32), pltpu.VMEM((tm, D), BF16),
                        pltpu.SemaphoreType.DMA(())],
        compiler_params=_params("arbitrary", "arbitrary"),
        name="ffn",
    )(x2d, scale, shift, gate, norm_g.reshape(1, D), w_gate, w_up, w_down, final_g.reshape(1, D))


def kernel(x, c, w_ada, b_ada, norm1_g, w_in, v_norm_g, w_spatial, b_spatial, out_norm_g, w_out,
           norm2_g, w_gate, w_up, w_down, final_g):
    B, S, D = x.shape
    depth = w_ada.shape[0]
    gm_width = v_norm_g.shape[-1]
    sb_width = (w_in.shape[-1] - 2 * gm_width) // 3
    head_dim = sb_width // SB_HEADS
    assert head_dim == LANE and gm_width // w_spatial.shape[1] == LANE

    xf = x.reshape(B * S, D)
    for l in range(depth):
        mod = _modulation(c, w_ada[l], b_ada[l]).reshape(B, N_MOD, 1, D)
        shift1, scale1, gate1, shift2, scale2, gate2 = [mod[:, m] for m in range(N_MOD)]

        proj = _in_proj(xf, scale1, shift1, norm1_g[l], w_in[l].astype(BF16), v_norm_g[l],
                        seq=S, gm_width=gm_width, sb_width=sb_width, head_dim=head_dim)
        o_gm = _gmlp(proj, w_spatial[l], b_spatial[l], out_norm_g[l, :gm_width])
        o_sb = _attention(proj, out_norm_g[l, gm_width:], batch=B, seq=S, heads=SB_HEADS,
                          q_off=2 * gm_width // LANE)
        xf = _out_proj(xf, o_gm, o_sb, w_out[l].astype(BF16), gate1, seq=S)
        xf = _ffn(xf, scale2, shift2, gate2, norm2_g[l], w_gate[l].astype(BF16),
                  w_up[l].astype(BF16), w_down[l].astype(BF16), final_g,
                  seq=S, final_norm=(l == depth - 1))
    return xf.reshape(B, S, D)
```

```python
import functools

import jax
import jax.numpy as jnp
from jax import lax
from jax.experimental import pallas as pl
from jax.experimental.pallas import tpu as pltpu

EPS = 1e-6
N_MOD = 6
SB_HEADS = 8
LANE = 128
VMEM_LIMIT = 56 * 1024 * 1024
LOG2E = 1.4426950408889634
DEAD_LOG2 = -150.0

F32 = jnp.float32
BF16 = jnp.bfloat16


def _params(*sem):
    return pltpu.CompilerParams(dimension_semantics=sem, vmem_limit_bytes=VMEM_LIMIT)


def _rms(x):
    return x * lax.rsqrt(jnp.mean(x * x, axis=-1, keepdims=True) + EPS)


def _mod_kernel(c_ref, w_ref, b_ref, o_ref):
    c = c_ref[...]
    c_act = (c * jax.nn.sigmoid(c)).astype(BF16)
    o_ref[...] = jnp.dot(c_act, w_ref[...].astype(BF16), preferred_element_type=F32) + b_ref[...]


def _modulation(c, w_ada, b_ada, tn=1024):
    B, D = c.shape
    N = w_ada.shape[1]
    return pl.pallas_call(
        _mod_kernel,
        out_shape=jax.ShapeDtypeStruct((B, N), F32),
        grid=(N // tn,),
        in_specs=[pl.BlockSpec((B, D), lambda j: (0, 0)),
                  pl.BlockSpec((D, tn), lambda j: (0, j)),
                  pl.BlockSpec((1, tn), lambda j: (0, j))],
        out_specs=pl.BlockSpec((B, tn), lambda j: (0, j)),
        compiler_params=_params("parallel"),
        name="modulation",
    )(c, w_ada, b_ada.reshape(1, N))


def _gelu(x):
    return 0.5 * x * (1.0 + lax.erf(x * 0.7071067811865476))


def _in_kernel(x_ref, sc_ref, sh_ref, g_ref, w_ref, vg_ref, o_ref, h_ref, acc_ref,
               *, n_gm, n_seg, q_scale, rc, pc):
    j = pl.program_id(1)
    tm, tn = acc_ref.shape
    ncb = tn // LANE

    @pl.when(j == 0)
    def _():
        gs = g_ref[...] * (1.0 + sc_ref[0])
        sh = sh_ref[0]

        def chunk(r, carry):
            rows = pl.ds(pl.multiple_of(r * pc, pc), pc)
            h_ref[rows, :] = (_rms(x_ref[rows, :]) * gs + sh).astype(BF16)
            return carry
        lax.fori_loop(0, tm // pc, chunk, 0, unroll=4)

    acc_ref[...] = jnp.dot(h_ref[...], w_ref[...].astype(BF16), preferred_element_type=F32)

    def epilogue(fn):
        def body(r, carry):
            rows = pl.ds(pl.multiple_of(r * rc, rc), rc)
            for cb in range(ncb):
                blk = acc_ref[rows, cb * LANE:(cb + 1) * LANE]
                o_ref[cb, rows, :] = fn(blk, cb).astype(BF16)
            return carry
        lax.fori_loop(0, tm // rc, body, 0)

    @pl.when(j < n_gm)
    def _():
        epilogue(lambda blk, cb: _gelu(blk))

    @pl.when(jnp.logical_and(j >= n_gm, j < 2 * n_gm))
    def _():
        def fn(blk, cb):
            z = _gelu(blk)
            zc = z - jnp.mean(z, axis=-1, keepdims=True)
            return _rms(zc) * vg_ref[0, :, cb * LANE:(cb + 1) * LANE]
        epilogue(fn)

    @pl.when(jnp.logical_and(j >= 2 * n_gm, j < 2 * n_gm + n_seg))
    def _():
        epilogue(lambda blk, cb: blk * q_scale)

    @pl.when(j >= 2 * n_gm + n_seg)
    def _():
        epilogue(lambda blk, cb: blk)


def _in_proj(x2d, scale, shift, norm_g, w_in, v_norm_g, *, seq, gm_width, sb_width, head_dim,
             tm=1024, tn=1024, rc=128, pc=16):
    T, D = x2d.shape
    N = w_in.shape[1]
    assert gm_width % tn == 0 and sb_width % tn == 0 and seq % tm == 0
    n_gm, n_seg = gm_width // tn, sb_width // tn
    per_b = seq // tm
    kern = functools.partial(_in_kernel, n_gm=n_gm, n_seg=n_seg, q_scale=LOG2E * head_dim ** -0.5, rc=rc,
                             pc=pc)
    return pl.pallas_call(
        kern,
        out_shape=jax.ShapeDtypeStruct((N // LANE, T, LANE), BF16),
        grid=(T // tm, N // tn),
        in_specs=[pl.BlockSpec((tm, D), lambda i, j: (i, 0)),
                  pl.BlockSpec((1, 1, D), lambda i, j: (i // per_b, 0, 0)),
                  pl.BlockSpec((1, 1, D), lambda i, j: (i // per_b, 0, 0)),
                  pl.BlockSpec((1, D), lambda i, j: (0, 0)),
                  pl.BlockSpec((D, tn), lambda i, j: (0, j)),
                  pl.BlockSpec((1, 1, tn), lambda i, j: (jnp.clip(j - n_gm, 0, n_gm - 1), 0, 0))],
        out_specs=pl.BlockSpec((tn // LANE, tm, LANE), lambda i, j: (j, i, 0)),
        scratch_shapes=[pltpu.VMEM((tm, D), BF16), pltpu.VMEM((tm, tn), F32)],
        compiler_params=_params("parallel", "arbitrary"),
        name="in_proj",
    )(x2d, scale, shift, norm_g.reshape(1, D), w_in, v_norm_g.reshape(n_gm, 1, tn))


def _gmlp_kernel(u_ref, v_ref, w_ref, b_ref, g_ref, o_ref, *, chunk):
    groups, tr, _ = u_ref.shape
    n_ch = tr // chunk
    causal = (lax.broadcasted_iota(jnp.int32, (chunk, chunk), 0)
              >= lax.broadcasted_iota(jnp.int32, (chunk, chunk), 1))
    for g in range(groups):
        w = jnp.where(causal, w_ref[g], 0.0).astype(BF16)
        v_cat = jnp.concatenate([v_ref[g, c * chunk:(c + 1) * chunk, :] for c in range(n_ch)], axis=1)
        mixed = jnp.dot(w, v_cat, preferred_element_type=F32) + b_ref[g]
        for c in range(n_ch):
            rows = slice(c * chunk, (c + 1) * chunk)
            o = u_ref[g, rows, :].astype(F32) * mixed[:, c * LANE:(c + 1) * LANE]
            o_ref[g, rows, :] = (_rms(o) * g_ref[g]).astype(BF16)


def _gmlp(proj, w_s, b_s, gain, *, tr=512):
    _, T, _ = proj.shape
    groups, chunk, _ = w_s.shape
    blk = pl.BlockSpec((groups, tr, LANE), lambda r: (0, r, 0))
    return pl.pallas_call(
        functools.partial(_gmlp_kernel, chunk=chunk),
        out_shape=jax.ShapeDtypeStruct((groups, T, LANE), BF16),
        grid=(T // tr,),
        in_specs=[blk,
                  pl.BlockSpec((groups, tr, LANE), lambda r: (1, r, 0)),
                  pl.BlockSpec((groups, chunk, chunk), lambda r: (0, 0, 0)),
                  pl.BlockSpec((groups, chunk, 1), lambda r: (0, 0, 0)),
                  pl.BlockSpec((groups, 1, LANE), lambda r: (0, 0, 0))],
        out_specs=blk,
        compiler_params=_params("parallel"),
        name="gmlp",
    )(proj, proj, w_s, b_s.reshape(groups, chunk, 1), gain.reshape(groups, 1, LANE))


def _attn_kernel(q_ref, k_ref, v_ref, g_ref, o_ref, acc_ref, car_ref, *, tq):
    S = q_ref.shape[1]
    nq = S // tq
    ri = lax.broadcasted_iota(jnp.int32, (tq, tq), 0)
    ci = lax.broadcasted_iota(jnp.int32, (tq, tq), 1)
    below = ri > ci
    suffix = jnp.where(below, 1.0, 0.0).astype(BF16)

    def logits(q, k):
        z = lax.dot_general(q, k, (((1,), (1,)), ((), ())), preferred_element_type=F32)
        m = jnp.minimum(z, 0.0)
        d = m - z
        l = jnp.log(1.0 + jnp.exp2(m + d)) * LOG2E
        return m - l, d - l

    car_up = acc_up = None
    for j in reversed(range(nq)):
        two = j + 1 < nq
        kj = slice(j * tq, (j + 1) * tq)
        q = q_ref[0, j * tq:(j + 2) * tq, :] if two else q_ref[0, kj, :]
        lb, lm = logits(q, k_ref[0, kj, :])
        lm_diag = jnp.where(below, lm[:tq], 0.0)
        lm = jnp.concatenate([lm_diag, lm[tq:]], axis=0) if two else lm_diag
        tail = jnp.dot(lm.astype(BF16), suffix, preferred_element_type=F32)
        x = lb + tail
        a = jnp.where(below, jnp.exp2(x[:tq]), 0.0)
        if two:
            a = jnp.concatenate([a, jnp.exp2(x[tq:] + car_up)], axis=0)
        pv = jnp.dot(a.astype(BF16), v_ref[0, kj, :], preferred_element_type=F32)
        rowsum = tail[:, :1] + lm[:, :1]
        if two:
            up = slice((j + 1) * tq, (j + 2) * tq)
            acc_ref[up, :] = acc_up + pv[tq:]
            car_ref[up, :] = car_up + rowsum[tq:]
        car_up, acc_up = rowsum[:tq], pv[:tq]
    acc_ref[:tq, :] = acc_up
    car_ref[:tq, :] = car_up

    if nq > 2:
        @pl.when(jnp.max(car_ref[2 * tq:, :]) > DEAD_LOG2)
        def _():
            def qblock(i, _):
                rows = pl.ds(pl.multiple_of(i * tq, tq), tq)
                q = q_ref[0, rows, :]

                def alive(st):
                    return jnp.logical_and(st[0] >= 0, jnp.max(st[1]) > DEAD_LOG2)

                def visit(st):
                    j, car, acc = st
                    ks = pl.ds(pl.multiple_of(j * tq, tq), tq)
                    lb, lm = logits(q, k_ref[0, ks, :])
                    tail = jnp.dot(lm.astype(BF16), suffix, preferred_element_type=F32)
                    a = jnp.exp2(lb + tail + car)
                    acc = acc + jnp.dot(a.astype(BF16), v_ref[0, ks, :], preferred_element_type=F32)
                    return j - 1, car + (tail[:, :1] + lm[:, :1]), acc

                _, car, acc = lax.while_loop(alive, visit, (i - 2, car_ref[rows, :], acc_ref[rows, :]))
                acc_ref[rows, :] = acc
                car_ref[rows, :] = car
                return 0
            lax.fori_loop(2, nq, qblock, 0)

    gain = g_ref[0]
    for i in range(nq):
        rows = slice(i * tq, (i + 1) * tq)
        o_ref[0, rows, :] = (_rms(acc_ref[rows, :]) * gain).astype(BF16)


def _attention(proj, gain, *, batch, seq, heads, q_off, tq=256):
    kern = functools.partial(_attn_kernel, tq=tq)

    def spec(off):
        return pl.BlockSpec((1, seq, LANE), lambda b, h: (off + h, b, 0))

    return pl.pallas_call(
        kern,
        out_shape=jax.ShapeDtypeStruct((heads, batch * seq, LANE), BF16),
        grid=(batch, heads),
        in_specs=[spec(q_off), spec(q_off + heads), spec(q_off + 2 * heads),
                  pl.BlockSpec((1, 1, LANE), lambda b, h: (h, 0, 0))],
        out_specs=pl.BlockSpec((1, seq, LANE), lambda b, h: (h, b, 0)),
        scratch_shapes=[pltpu.VMEM((seq, LANE), F32), pltpu.VMEM((seq, 1), F32)],
        compiler_params=_params("parallel", "parallel"),
        name="sb_attention",
    )(proj, proj, proj, gain.reshape(heads, 1, LANE))


def _out_kernel(x_ref, a_ref, b_ref, w_ref, gate_ref, o_ref):
    pieces = [a_ref[c] for c in range(a_ref.shape[0])] + [b_ref[c] for c in range(b_ref.shape[0])]
    o = jnp.concatenate(pieces, axis=-1)
    y = jnp.dot(o, w_ref[...], preferred_element_type=F32)
    o_ref[...] = x_ref[...] + gate_ref[0] * y


def _out_proj(x2d, o_gm, o_sb, w_out, gate, *, seq, tm=512):
    T, D = x2d.shape
    per_b = seq // tm
    return pl.pallas_call(
        _out_kernel,
        out_shape=jax.ShapeDtypeStruct((T, D), F32),
        grid=(T // tm,),
        in_specs=[pl.BlockSpec((tm, D), lambda i: (i, 0)),
                  pl.BlockSpec((o_gm.shape[0], tm, LANE), lambda i: (0, i, 0)),
                  pl.BlockSpec((o_sb.shape[0], tm, LANE), lambda i: (0, i, 0)),
                  pl.BlockSpec(w_out.shape, lambda i: (0, 0)),
                  pl.BlockSpec((1, 1, D), lambda i: (i // per_b, 0, 0))],
        out_specs=pl.BlockSpec((tm, D), lambda i: (i, 0)),
        compiler_params=_params("parallel"),
        name="out_proj",
    )(x2d, o_gm, o_sb, w_out, gate)


def _ffn_kernel(*refs, final_norm, n_chunks, cast):
    refs = list(refs)
    prev_hbm = None if cast else refs.pop(0)
    (x_hbm, xres_ref, xnxt_ref, sc_ref, sh_ref, scn_ref, shn_ref, gate_ref, g_ref,
     wg_ref, wu_ref, wd_ref, fg_ref, o_ref) = refs[:14]
    rest = refs[14:]
    if cast:
        wg_out, wu_out, wd_out = rest[:3]
        rest = rest[3:]
    h_ref, stage, sem = rest

    i, j = pl.program_id(0), pl.program_id(1)
    tm = o_ref.shape[0]
    cr = tm // n_chunks
    first = 0 if cast else 1

    def modulated_norm(x, sc, sh):
        return (_rms(x) * (g_ref[...] * (1.0 + sc)) + sh).astype(BF16)

    def step():
        slot = lax.rem(i, 2)

        @pl.when(j == 0)
        def _():
            @pl.when(i == first)
            def _():
                for c in range(n_chunks):
                    cp = pltpu.make_async_copy(x_hbm.at[pl.ds(first * tm + c * cr, cr), :], stage, sem)
                    cp.start()
                    cp.wait()
                    h_ref[first % 2, c * cr:(c + 1) * cr, :] = modulated_norm(stage[...], sc_ref[0],
                                                                               sh_ref[0])

            o_ref[:cr, :] = jnp.zeros((cr, o_ref.shape[1]), F32)

        wg, wu, wd = wg_ref[...], wu_ref[...], wd_ref[...]
        if cast:
            wg, wu, wd = wg.astype(BF16), wu.astype(BF16), wd.astype(BF16)
            wg_out[...], wu_out[...], wd_out[...] = wg, wu, wd

        gt = jnp.dot(h_ref[slot], wg, preferred_element_type=F32)

        c = jnp.clip(j - 1, 0, n_chunks - 1)
        rows = pl.ds(pl.multiple_of(c * cr, cr), cr)
        live = jnp.logical_and(j >= 1, j <= n_chunks)
        o_ref[rows, :] += jnp.where(live, 1.0, 0.0) * xres_ref[...]
        h_ref[1 - slot, rows, :] = modulated_norm(xnxt_ref[...], scn_ref[0], shn_ref[0])

        up = jnp.dot(h_ref[slot], wu, preferred_element_type=F32)
        a = (gt * jax.nn.sigmoid(gt) * up).astype(BF16)
        part = gate_ref[0] * jnp.dot(a, wd, preferred_element_type=F32)
        o_ref[...] = jnp.where(j == 0, 0.0, o_ref[...]) + part

        if final_norm:
            @pl.when(j == pl.num_programs(1) - 1)
            def _():
                o_ref[...] = _rms(o_ref[...]) * fg_ref[...]

    if cast:
        step()
    else:
        @pl.when(jnp.logical_and(i == 0, j == 0))
        def _():
            cp = pltpu.make_async_copy(prev_hbm, o_ref, sem)
            cp.start()
            cp.wait()

        pl.when(i > 0)(step)


def _ffn_call(prev, x2d, scale, shift, gate, norm_g, w_gate, w_up, w_down, final_g, *, seq,
              final_norm, tf, tm=1024, n_chunks=8):
    T, D = x2d.shape
    FF = w_gate.shape[1]
    assert FF // tf > n_chunks and tm % n_chunks == 0 and seq % tm == 0
    cast = prev is None
    per_b = seq // tm
    cr = tm // n_chunks
    n_tiles = 1 if cast else T // tm
    last = T // tm - 1

    def chunk_of(tile, j):
        return tile * n_chunks + jnp.clip(j - 1, 0, n_chunks - 1)

    def nxt(i):
        return jnp.minimum(i + 1, last)

    def wj(i, j):
        return j if cast else jnp.where(i == 0, 0, j)

    mod_spec = pl.BlockSpec((1, 1, D), lambda i, j: (i // per_b, 0, 0))
    modn_spec = pl.BlockSpec((1, 1, D), lambda i, j: (nxt(i) // per_b, 0, 0))
    vec_spec = pl.BlockSpec((1, D), lambda i, j: (0, 0))
    wcol_spec = pl.BlockSpec((D, tf), lambda i, j: (0, wj(i, j)))
    wrow_spec = pl.BlockSpec((tf, D), lambda i, j: (wj(i, j), 0))
    in_specs = [pl.BlockSpec(memory_space=pl.ANY),
                pl.BlockSpec((cr, D), lambda i, j: (chunk_of(i, j), 0)),
                pl.BlockSpec((cr, D), lambda i, j: (chunk_of(nxt(i), j), 0)),
                mod_spec, mod_spec, modn_spec, modn_spec, mod_spec, vec_spec,
                wcol_spec, wcol_spec, wrow_spec, vec_spec]
    args = [x2d, x2d, x2d, scale, shift, scale, shift, gate, norm_g.reshape(1, D),
            w_gate, w_up, w_down, final_g.reshape(1, D)]
    out_shape = [jax.ShapeDtypeStruct((n_tiles * tm, D), F32)]
    out_specs = [pl.BlockSpec((tm, D), lambda i, j: (i, 0))]
    if cast:
        out_shape += [jax.ShapeDtypeStruct(w.shape, BF16) for w in (w_gate, w_up, w_down)]
        out_specs += [wcol_spec, wcol_spec, wrow_spec]
    else:
        in_specs.insert(0, pl.BlockSpec(memory_space=pl.ANY))
        args.insert(0, prev)
    return pl.pallas_call(
        functools.partial(_ffn_kernel, final_norm=final_norm, n_chunks=n_chunks, cast=cast),
        out_shape=out_shape,
        grid=(n_tiles, FF // tf),
        in_specs=in_specs,
        out_specs=out_specs,
        scratch_shapes=[pltpu.VMEM((2, tm, D), BF16), pltpu.VMEM((cr, D), F32),
                        pltpu.SemaphoreType.DMA(())],
        compiler_params=_params("arbitrary", "arbitrary"),
        name="ffn_cast" if cast else "ffn",
    )(*args)


def _ffn(x2d, scale, shift, gate, norm_g, w_gate, w_up, w_down, final_g, *, seq, final_norm,
         tm=1024):
    common = dict(seq=seq, final_norm=final_norm, tm=tm)
    out, wg, wu, wd = _ffn_call(None, x2d, scale, shift, gate, norm_g, w_gate, w_up, w_down, final_g,
                                tf=256, **common)
    if x2d.shape[0] > tm:
        out, = _ffn_call(out, x2d, scale, shift, gate, norm_g, wg, wu, wd, final_g, tf=512, **common)
    return out


def kernel(x, c, w_ada, b_ada, norm1_g, w_in, v_norm_g, w_spatial, b_spatial, out_norm_g, w_out,
           norm2_g, w_gate, w_up, w_down, final_g):
    B, S, D = x.shape
    depth = w_ada.shape[0]
    gm_width = v_norm_g.shape[-1]
    sb_width = (w_in.shape[-1] - 2 * gm_width) // 3
    head_dim = sb_width // SB_HEADS
    assert head_dim == LANE and gm_width // w_spatial.shape[1] == LANE

    xf = x.reshape(B * S, D)
    for l in range(depth):
        mod = _modulation(c, w_ada[l], b_ada[l]).reshape(B, N_MOD, 1, D)
        shift1, scale1, gate1, shift2, scale2, gate2 = [mod[:, m] for m in range(N_MOD)]

        proj = _in_proj(xf, scale1, shift1, norm1_g[l], w_in[l], v_norm_g[l],
                        seq=S, gm_width=gm_width, sb_width=sb_width, head_dim=head_dim)
        o_gm = _gmlp(proj, w_spatial[l], b_spatial[l], out_norm_g[l, :gm_width])
        o_sb = _attention(proj, out_norm_g[l, gm_width:], batch=B, seq=S, heads=SB_HEADS,
                          q_off=2 * gm_width // LANE)
        xf = _out_proj(xf, o_gm, o_sb, w_out[l].astype(BF16), gate1, seq=S)
        xf = _ffn(xf, scale2, shift2, gate2, norm2_g[l], w_gate[l], w_up[l], w_down[l], final_g,
                  seq=S, final_norm=(l == depth - 1))
    return xf.reshape(B, S, D)
```

```python
import functools

import jax
import jax.numpy as jnp
from jax import lax
from jax.experimental import pallas as pl
from jax.experimental.pallas import tpu as pltpu

EPS = 1e-6
N_MOD = 6
SB_HEADS = 8
LANE = 128
VMEM_LIMIT = 56 * 1024 * 1024
LOG2E = 1.4426950408889634
DEAD_LOG2 = -150.0

F32 = jnp.float32
BF16 = jnp.bfloat16


def _params(*sem):
    return pltpu.CompilerParams(dimension_semantics=sem, vmem_limit_bytes=VMEM_LIMIT)


def _rms(x):
    return x * lax.rsqrt(jnp.mean(x * x, axis=-1, keepdims=True) + EPS)


def _mod_kernel(c_ref, w_ref, b_ref, o_ref):
    c = c_ref[...]
    c_act = (c * jax.nn.sigmoid(c)).astype(BF16)
    o_ref[...] = jnp.dot(c_act, w_ref[...].astype(BF16), preferred_element_type=F32) + b_ref[...]


def _modulation(c, w_ada, b_ada, tn=1024):
    B, D = c.shape
    N = w_ada.shape[1]
    return pl.pallas_call(
        _mod_kernel,
        out_shape=jax.ShapeDtypeStruct((B, N), F32),
        grid=(N // tn,),
        in_specs=[pl.BlockSpec((B, D), lambda j: (0, 0)),
                  pl.BlockSpec((D, tn), lambda j: (0, j)),
                  pl.BlockSpec((1, tn), lambda j: (0, j))],
        out_specs=pl.BlockSpec((B, tn), lambda j: (0, j)),
        compiler_params=_params("parallel"),
        name="modulation",
    )(c, w_ada, b_ada.reshape(1, N))


def _gelu(x):
    return 0.5 * x * (1.0 + lax.erf(x * 0.7071067811865476))


def _in_kernel(x_ref, sc_ref, sh_ref, g_ref, w_ref, vg_ref, o_ref, h_ref,
               *, n_gm, n_seg, q_scale, pc):
    j = pl.program_id(1)
    tm = h_ref.shape[0]
    ncb = o_ref.shape[0]

    @pl.when(j == 0)
    def _():
        gs = g_ref[...] * (1.0 + sc_ref[0])
        sh = sh_ref[0]

        def chunk(r, carry):
            rows = pl.ds(pl.multiple_of(r * pc, pc), pc)
            h_ref[rows, :] = (_rms(x_ref[rows, :]) * gs + sh).astype(BF16)
            return carry
        lax.fori_loop(0, tm // pc, chunk, 0, unroll=4)

    def epilogue(fn):
        acc = jnp.dot(h_ref[...], w_ref[...].astype(BF16), preferred_element_type=F32)
        for cb in range(ncb):
            o_ref[cb] = fn(acc[:, cb * LANE:(cb + 1) * LANE], cb).astype(BF16)

    @pl.when(j < n_gm)
    def _():
        epilogue(lambda blk, cb: _gelu(blk))

    @pl.when(jnp.logical_and(j >= n_gm, j < 2 * n_gm))
    def _():
        def fn(blk, cb):
            z = _gelu(blk)
            zc = z - jnp.mean(z, axis=-1, keepdims=True)
            return _rms(zc) * vg_ref[0, :, cb * LANE:(cb + 1) * LANE]
        epilogue(fn)

    @pl.when(jnp.logical_and(j >= 2 * n_gm, j < 2 * n_gm + n_seg))
    def _():
        epilogue(lambda blk, cb: blk * q_scale)

    @pl.when(j >= 2 * n_gm + n_seg)
    def _():
        epilogue(lambda blk, cb: blk)


def _in_proj(x2d, scale, shift, norm_g, w_in, v_norm_g, *, seq, gm_width, sb_width, head_dim,
             tm=1024, tn=1024, pc=16):
    T, D = x2d.shape
    N = w_in.shape[1]
    assert gm_width % tn == 0 and sb_width % tn == 0 and seq % tm == 0
    n_gm, n_seg = gm_width // tn, sb_width // tn
    per_b = seq // tm
    kern = functools.partial(_in_kernel, n_gm=n_gm, n_seg=n_seg, q_scale=LOG2E * head_dim ** -0.5, pc=pc)
    return pl.pallas_call(
        kern,
        out_shape=jax.ShapeDtypeStruct((N // LANE, T, LANE), BF16),
        grid=(T // tm, N // tn),
        in_specs=[pl.BlockSpec((tm, D), lambda i, j: (i, 0)),
                  pl.BlockSpec((1, 1, D), lambda i, j: (i // per_b, 0, 0)),
                  pl.BlockSpec((1, 1, D), lambda i, j: (i // per_b, 0, 0)),
                  pl.BlockSpec((1, D), lambda i, j: (0, 0)),
                  pl.BlockSpec((D, tn), lambda i, j: (0, j)),
                  pl.BlockSpec((1, 1, tn), lambda i, j: (jnp.clip(j - n_gm, 0, n_gm - 1), 0, 0))],
        out_specs=pl.BlockSpec((tn // LANE, tm, LANE), lambda i, j: (j, i, 0)),
        scratch_shapes=[pltpu.VMEM((tm, D), BF16)],
        compiler_params=_params("parallel", "arbitrary"),
        name="in_proj",
    )(x2d, scale, shift, norm_g.reshape(1, D), w_in, v_norm_g.reshape(n_gm, 1, tn))


def _gmlp_kernel(u_ref, v_ref, w_ref, b_ref, g_ref, o_ref, *, chunk):
    groups, tr, _ = u_ref.shape
    n_ch = tr // chunk
    causal = (lax.broadcasted_iota(jnp.int32, (chunk, chunk), 0)
              >= lax.broadcasted_iota(jnp.int32, (chunk, chunk), 1))
    for g in range(groups):
        w = jnp.where(causal, w_ref[g], 0.0).astype(BF16)
        v_cat = jnp.concatenate([v_ref[g, c * chunk:(c + 1) * chunk, :] for c in range(n_ch)], axis=1)
        mixed = jnp.dot(w, v_cat, preferred_element_type=F32) + b_ref[g]
        for c in range(n_ch):
            rows = slice(c * chunk, (c + 1) * chunk)
            o = u_ref[g, rows, :].astype(F32) * mixed[:, c * LANE:(c + 1) * LANE]
            o_ref[g, rows, :] = (_rms(o) * g_ref[g]).astype(BF16)


def _gmlp(proj, w_s, b_s, gain, *, tr=512):
    _, T, _ = proj.shape
    groups, chunk, _ = w_s.shape
    blk = pl.BlockSpec((groups, tr, LANE), lambda r: (0, r, 0))
    return pl.pallas_call(
        functools.partial(_gmlp_kernel, chunk=chunk),
        out_shape=jax.ShapeDtypeStruct((groups, T, LANE), BF16),
        grid=(T // tr,),
        in_specs=[blk,
                  pl.BlockSpec((groups, tr, LANE), lambda r: (1, r, 0)),
                  pl.BlockSpec((groups, chunk, chunk), lambda r: (0, 0, 0)),
                  pl.BlockSpec((groups, chunk, 1), lambda r: (0, 0, 0)),
                  pl.BlockSpec((groups, 1, LANE), lambda r: (0, 0, 0))],
        out_specs=blk,
        compiler_params=_params("parallel"),
        name="gmlp",
    )(proj, proj, w_s, b_s.reshape(groups, chunk, 1), gain.reshape(groups, 1, LANE))


def _attn_kernel(q_ref, k_ref, v_ref, g_ref, o_ref, acc_ref, car_ref, *, tq):
    S = q_ref.shape[1]
    nq = S // tq
    ri = lax.broadcasted_iota(jnp.int32, (tq, tq), 0)
    ci = lax.broadcasted_iota(jnp.int32, (tq, tq), 1)
    below = ri > ci
    suffix = jnp.where(below, 1.0, 0.0).astype(BF16)

    def logits(q, k):
        z = lax.dot_general(q, k, (((1,), (1,)), ((), ())), preferred_element_type=F32)
        m = jnp.minimum(z, 0.0)
        d = m - z
        l = jnp.log(1.0 + jnp.exp2(m + d)) * LOG2E
        return m - l, d - l

    car_up = acc_up = None
    for j in reversed(range(nq)):
        two = j + 1 < nq
        kj = slice(j * tq, (j + 1) * tq)
        q = q_ref[0, j * tq:(j + 2) * tq, :] if two else q_ref[0, kj, :]
        lb, lm = logits(q, k_ref[0, kj, :])
        lm_diag = jnp.where(below, lm[:tq], 0.0)
        lm = jnp.concatenate([lm_diag, lm[tq:]], axis=0) if two else lm_diag
        tail = jnp.dot(lm.astype(BF16), suffix, preferred_element_type=F32)
        x = lb + tail
        a = jnp.where(below, jnp.exp2(x[:tq]), 0.0)
        if two:
            a = jnp.concatenate([a, jnp.exp2(x[tq:] + car_up)], axis=0)
        pv = jnp.dot(a.astype(BF16), v_ref[0, kj, :], preferred_element_type=F32)
        rowsum = tail[:, :1] + lm[:, :1]
        if two:
            up = slice((j + 1) * tq, (j + 2) * tq)
            acc_ref[up, :] = acc_up + pv[tq:]
            car_ref[up, :] = car_up + rowsum[tq:]
        car_up, acc_up = rowsum[:tq], pv[:tq]
    acc_ref[:tq, :] = acc_up
    car_ref[:tq, :] = car_up

    if nq > 2:
        @pl.when(jnp.max(car_ref[2 * tq:, :]) > DEAD_LOG2)
        def _():
            def qblock(i, _):
                rows = pl.ds(pl.multiple_of(i * tq, tq), tq)
                q = q_ref[0, rows, :]

                def alive(st):
                    return jnp.logical_and(st[0] >= 0, jnp.max(st[1]) > DEAD_LOG2)

                def visit(st):
                    j, car, acc = st
                    ks = pl.ds(pl.multiple_of(j * tq, tq), tq)
                    lb, lm = logits(q, k_ref[0, ks, :])
                    tail = jnp.dot(lm.astype(BF16), suffix, preferred_element_type=F32)
                    a = jnp.exp2(lb + tail + car)
                    acc = acc + jnp.dot(a.astype(BF16), v_ref[0, ks, :], preferred_element_type=F32)
                    return j - 1, car + (tail[:, :1] + lm[:, :1]), acc

                _, car, acc = lax.while_loop(alive, visit, (i - 2, car_ref[rows, :], acc_ref[rows, :]))
                acc_ref[rows, :] = acc
                car_ref[rows, :] = car
                return 0
            lax.fori_loop(2, nq, qblock, 0)

    gain = g_ref[0]
    for i in range(nq):
        rows = slice(i * tq, (i + 1) * tq)
        o_ref[0, rows, :] = (_rms(acc_ref[rows, :]) * gain).astype(BF16)


def _attention(proj, gain, *, batch, seq, heads, q_off, tq=256):
    kern = functools.partial(_attn_kernel, tq=tq)

    def spec(off):
        return pl.BlockSpec((1, seq, LANE), lambda b, h: (off + h, b, 0))

    return pl.pallas_call(
        kern,
        out_shape=jax.ShapeDtypeStruct((heads, batch * seq, LANE), BF16),
        grid=(batch, heads),
        in_specs=[spec(q_off), spec(q_off + heads), spec(q_off + 2 * heads),
                  pl.BlockSpec((1, 1, LANE), lambda b, h: (h, 0, 0))],
        out_specs=pl.BlockSpec((1, seq, LANE), lambda b, h: (h, b, 0)),
        scratch_shapes=[pltpu.VMEM((seq, LANE), F32), pltpu.VMEM((seq, 1), F32)],
        compiler_params=_params("parallel", "parallel"),
        name="sb_attention",
    )(proj, proj, proj, gain.reshape(heads, 1, LANE))


def _out_kernel(x_ref, a_ref, b_ref, w_ref, gate_ref, o_ref):
    pieces = [a_ref[c] for c in range(a_ref.shape[0])] + [b_ref[c] for c in range(b_ref.shape[0])]
    o = jnp.concatenate(pieces, axis=-1)
    y = jnp.dot(o, w_ref[...].astype(BF16), preferred_element_type=F32)
    o_ref[...] = x_ref[...] + gate_ref[0] * y


def _out_proj(x2d, o_gm, o_sb, w_out, gate, *, seq, tm=512):
    T, D = x2d.shape
    per_b = seq // tm
    return pl.pallas_call(
        _out_kernel,
        out_shape=jax.ShapeDtypeStruct((T, D), F32),
        grid=(T // tm,),
        in_specs=[pl.BlockSpec((tm, D), lambda i: (i, 0)),
                  pl.BlockSpec((o_gm.shape[0], tm, LANE), lambda i: (0, i, 0)),
                  pl.BlockSpec((o_sb.shape[0], tm, LANE), lambda i: (0, i, 0)),
                  pl.BlockSpec(w_out.shape, lambda i: (0, 0), pipeline_mode=pl.Buffered(1)),
                  pl.BlockSpec((1, 1, D), lambda i: (i // per_b, 0, 0))],
        out_specs=pl.BlockSpec((tm, D), lambda i: (i, 0)),
        compiler_params=_params("parallel"),
        name="out_proj",
    )(x2d, o_gm, o_sb, w_out, gate)


def _ffn_kernel(x_hbm, sc_ref, sh_ref, gate_ref, g_ref, wg_ref, wu_ref, wd_ref, fg_ref,
                o_ref, xbuf, h_ref, sem, *, final_norm, rc):
    i, j = pl.program_id(0), pl.program_id(1)
    tm = xbuf.shape[0]

    def x_copy(tile):
        return pltpu.make_async_copy(x_hbm.at[pl.ds(tile * tm, tm), :], xbuf, sem)

    @pl.when(j == 0)
    def _():
        @pl.when(i == 0)
        def _():
            x_copy(0).start()
        x_copy(i).wait()
        gs = g_ref[...] * (1.0 + sc_ref[0])
        sh = sh_ref[0]

        def chunk(r, carry):
            rows = pl.ds(pl.multiple_of(r * rc, rc), rc)
            x = xbuf[rows, :]
            h_ref[rows, :] = (_rms(x) * gs + sh).astype(BF16)
            o_ref[rows, :] = x
            return carry
        lax.fori_loop(0, tm // rc, chunk, 0, unroll=4)

    @pl.when(jnp.logical_and(j == 1, i + 1 < pl.num_programs(0)))
    def _():
        x_copy(i + 1).start()

    h = h_ref[...]
    gt = jnp.dot(h, wg_ref[...], preferred_element_type=F32)
    up = jnp.dot(h, wu_ref[...], preferred_element_type=F32)
    a = (gt * jax.nn.sigmoid(gt) * up).astype(BF16)
    o_ref[...] += gate_ref[0] * jnp.dot(a, wd_ref[...], preferred_element_type=F32)

    if final_norm:
        @pl.when(j == pl.num_programs(1) - 1)
        def _():
            o_ref[...] = _rms(o_ref[...]) * fg_ref[...]


def _ffn(x2d, scale, shift, gate, norm_g, w_gate, w_up, w_down, final_g, *, seq, final_norm,
         tm=1024, tf=512, rc=16):
    T, D = x2d.shape
    FF = w_gate.shape[1]
    assert FF // tf >= 2
    per_b = seq // tm
    mod_spec = pl.BlockSpec((1, 1, D), lambda i, j: (i // per_b, 0, 0))
    vec_spec = pl.BlockSpec((1, D), lambda i, j: (0, 0))
    return pl.pallas_call(
        functools.partial(_ffn_kernel, final_norm=final_norm, rc=rc),
        out_shape=jax.ShapeDtypeStruct((T, D), F32),
        grid=(T // tm, FF // tf),
        in_specs=[pl.BlockSpec(memory_space=pl.ANY),
                  mod_spec, mod_spec, mod_spec, vec_spec,
                  pl.BlockSpec((D, tf), lambda i, j: (0, j)),
                  pl.BlockSpec((D, tf), lambda i, j: (0, j)),
                  pl.BlockSpec((tf, D), lambda i, j: (j, 0)),
                  vec_spec],
        out_specs=pl.BlockSpec((tm, D), lambda i, j: (i, 0)),
        scratch_shapes=[pltpu.VMEM((tm, D), F32), pltpu.VMEM((tm, D), BF16),
                        pltpu.SemaphoreType.DMA(())],
        compiler_params=_params("arbitrary", "arbitrary"),
        name="ffn",
    )(x2d, scale, shift, gate, norm_g.reshape(1, D), w_gate, w_up, w_down, final_g.reshape(1, D))


def kernel(x, c, w_ada, b_ada, norm1_g, w_in, v_norm_g, w_spatial, b_spatial, out_norm_g, w_out,
           norm2_g, w_gate, w_up, w_down, final_g):
    B, S, D = x.shape
    depth = w_ada.shape[0]
    gm_width = v_norm_g.shape[-1]
    sb_width = (w_in.shape[-1] - 2 * gm_width) // 3
    head_dim = sb_width // SB_HEADS
    assert head_dim == LANE and gm_width // w_spatial.shape[1] == LANE

    xf = x.reshape(B * S, D)
    for l in range(depth):
        mod = _modulation(c, w_ada[l], b_ada[l]).reshape(B, N_MOD, 1, D)
        shift1, scale1, gate1, shift2, scale2, gate2 = [mod[:, m] for m in range(N_MOD)]

        proj = _in_proj(xf, scale1, shift1, norm1_g[l], w_in[l], v_norm_g[l],
                        seq=S, gm_width=gm_width, sb_width=sb_width, head_dim=head_dim)
        o_gm = _gmlp(proj, w_spatial[l], b_spatial[l], out_norm_g[l, :gm_width])
        o_sb = _attention(proj, out_norm_g[l, gm_width:], batch=B, seq=S, heads=SB_HEADS,
                          q_off=2 * gm_width // LANE)
        xf = _out_proj(xf, o_gm, o_sb, w_out[l], gate1, seq=S)
        xf = _ffn(xf, scale2, shift2, gate2, norm2_g[l], w_gate[l].astype(BF16),
                  w_up[l].astype(BF16), w_down[l].astype(BF16), final_g,
                  seq=S, final_norm=(l == depth - 1))
    return xf.reshape(B, S, D)
```

```python
import functools

import jax
import jax.numpy as jnp
from jax import lax
from jax.experimental import pallas as pl
from jax.experimental.pallas import tpu as pltpu

EPS = 1e-6
N_MOD = 6
SB_HEADS = 8
LANE = 128
VMEM_LIMIT = 56 * 1024 * 1024
LOG2E = 1.4426950408889634
DEAD_LOG2 = -150.0

F32 = jnp.float32
BF16 = jnp.bfloat16


def _params(*sem):
    return pltpu.CompilerParams(dimension_semantics=sem, vmem_limit_bytes=VMEM_LIMIT)


def _rms(x):
    return x * lax.rsqrt(jnp.mean(x * x, axis=-1, keepdims=True) + EPS)


def _mod_kernel(c_ref, w_ref, b_ref, o_ref):
    c = c_ref[...]
    c_act = (c * jax.nn.sigmoid(c)).astype(BF16)
    o_ref[...] = jnp.dot(c_act, w_ref[...].astype(BF16), preferred_element_type=F32) + b_ref[...]


def _modulation(c, w_ada, b_ada, tn=1024):
    B, D = c.shape
    N = w_ada.shape[1]
    return pl.pallas_call(
        _mod_kernel,
        out_shape=jax.ShapeDtypeStruct((B, N), F32),
        grid=(N // tn,),
        in_specs=[pl.BlockSpec((B, D), lambda j: (0, 0)),
                  pl.BlockSpec((D, tn), lambda j: (0, j)),
                  pl.BlockSpec((1, tn), lambda j: (0, j))],
        out_specs=pl.BlockSpec((B, tn), lambda j: (0, j)),
        compiler_params=_params("parallel"),
        name="modulation",
    )(c, w_ada, b_ada.reshape(1, N))


def _gelu(x):
    return 0.5 * x * (1.0 + lax.erf(x * 0.7071067811865476))


def _in_kernel(x_ref, sc_ref, sh_ref, g_ref, w_ref, vg_ref, o_ref, h_ref,
               *, n_gm, n_seg, q_scale, pc):
    j = pl.program_id(1)
    tm = h_ref.shape[0]
    ncb = o_ref.shape[0]

    @pl.when(j == 0)
    def _():
        gs = g_ref[...] * (1.0 + sc_ref[0])
        sh = sh_ref[0]

        def chunk(r, carry):
            rows = pl.ds(pl.multiple_of(r * pc, pc), pc)
            h_ref[rows, :] = (_rms(x_ref[rows, :]) * gs + sh).astype(BF16)
            return carry
        lax.fori_loop(0, tm // pc, chunk, 0, unroll=8)

    def epilogue(fn):
        acc = jnp.dot(h_ref[...], w_ref[...].astype(BF16), preferred_element_type=F32)
        for cb in range(ncb):
            o_ref[cb] = fn(acc[:, cb * LANE:(cb + 1) * LANE], cb).astype(BF16)

    @pl.when(j < n_gm)
    def _():
        epilogue(lambda blk, cb: _gelu(blk))

    @pl.when(jnp.logical_and(j >= n_gm, j < 2 * n_gm))
    def _():
        def fn(blk, cb):
            z = _gelu(blk)
            zc = z - jnp.mean(z, axis=-1, keepdims=True)
            return _rms(zc) * vg_ref[0, :, cb * LANE:(cb + 1) * LANE]
        epilogue(fn)

    @pl.when(jnp.logical_and(j >= 2 * n_gm, j < 2 * n_gm + n_seg))
    def _():
        epilogue(lambda blk, cb: blk * q_scale)

    @pl.when(j >= 2 * n_gm + n_seg)
    def _():
        epilogue(lambda blk, cb: blk)


def _in_proj(x2d, scale, shift, norm_g, w_in, v_norm_g, *, seq, gm_width, sb_width, head_dim,
             tm=1024, tn=1024, pc=16):
    T, D = x2d.shape
    N = w_in.shape[1]
    assert gm_width % tn == 0 and sb_width % tn == 0 and seq % tm == 0
    n_gm, n_seg = gm_width // tn, sb_width // tn
    per_b = seq // tm
    kern = functools.partial(_in_kernel, n_gm=n_gm, n_seg=n_seg, q_scale=LOG2E * head_dim ** -0.5, pc=pc)
    return pl.pallas_call(
        kern,
        out_shape=jax.ShapeDtypeStruct((N // LANE, T, LANE), BF16),
        grid=(T // tm, N // tn),
        in_specs=[pl.BlockSpec((tm, D), lambda i, j: (i, 0)),
                  pl.BlockSpec((1, 1, D), lambda i, j: (i // per_b, 0, 0)),
                  pl.BlockSpec((1, 1, D), lambda i, j: (i // per_b, 0, 0)),
                  pl.BlockSpec((1, D), lambda i, j: (0, 0)),
                  pl.BlockSpec((D, tn), lambda i, j: (0, j)),
                  pl.BlockSpec((1, 1, tn), lambda i, j: (jnp.clip(j - n_gm, 0, n_gm - 1), 0, 0))],
        out_specs=pl.BlockSpec((tn // LANE, tm, LANE), lambda i, j: (j, i, 0)),
        scratch_shapes=[pltpu.VMEM((tm, D), BF16)],
        compiler_params=_params("parallel", "arbitrary"),
        name="in_proj",
    )(x2d, scale, shift, norm_g.reshape(1, D), w_in, v_norm_g.reshape(n_gm, 1, tn))


def _gmlp_kernel(u_ref, v_ref, w_ref, b_ref, g_ref, o_ref, *, chunk):
    groups, tr, _ = u_ref.shape
    n_ch = tr // chunk
    causal = (lax.broadcasted_iota(jnp.int32, (chunk, chunk), 0)
              >= lax.broadcasted_iota(jnp.int32, (chunk, chunk), 1))
    for g in range(groups):
        w = jnp.where(causal, w_ref[g], 0.0).astype(BF16)
        v_cat = jnp.concatenate([v_ref[g, c * chunk:(c + 1) * chunk, :] for c in range(n_ch)], axis=1)
        mixed = jnp.dot(w, v_cat, preferred_element_type=F32) + b_ref[g]
        for c in range(n_ch):
            rows = slice(c * chunk, (c + 1) * chunk)
            o = u_ref[g, rows, :].astype(F32) * mixed[:, c * LANE:(c + 1) * LANE]
            o_ref[g, rows, :] = (_rms(o) * g_ref[g]).astype(BF16)


def _gmlp(proj, w_s, b_s, gain, *, tr=512):
    _, T, _ = proj.shape
    groups, chunk, _ = w_s.shape
    blk = pl.BlockSpec((groups, tr, LANE), lambda r: (0, r, 0))
    return pl.pallas_call(
        functools.partial(_gmlp_kernel, chunk=chunk),
        out_shape=jax.ShapeDtypeStruct((groups, T, LANE), BF16),
        grid=(T // tr,),
        in_specs=[blk,
                  pl.BlockSpec((groups, tr, LANE), lambda r: (1, r, 0)),
                  pl.BlockSpec((groups, chunk, chunk), lambda r: (0, 0, 0)),
                  pl.BlockSpec((groups, chunk, 1), lambda r: (0, 0, 0)),
                  pl.BlockSpec((groups, 1, LANE), lambda r: (0, 0, 0))],
        out_specs=blk,
        compiler_params=_params("parallel"),
        name="gmlp",
    )(proj, proj, w_s, b_s.reshape(groups, chunk, 1), gain.reshape(groups, 1, LANE))


def _attn_kernel(q_ref, k_ref, v_ref, g_ref, *rest, tq, n_cast):
    cast_in, o_ref, cast_out = rest[:n_cast], rest[n_cast], rest[n_cast + 1:2 * n_cast + 1]
    acc_ref, car_ref = rest[2 * n_cast + 1:]
    for src, dst in zip(cast_in, cast_out):
        dst[...] = src[...].astype(BF16)

    S = q_ref.shape[1]
    nq = S // tq
    ri = lax.broadcasted_iota(jnp.int32, (tq, tq), 0)
    ci = lax.broadcasted_iota(jnp.int32, (tq, tq), 1)
    below = ri > ci
    suffix = jnp.where(below, 1.0, 0.0).astype(BF16)

    def logits(q, k):
        z = lax.dot_general(q, k, (((1,), (1,)), ((), ())), preferred_element_type=F32)
        m = jnp.minimum(z, 0.0)
        d = m - z
        l = jnp.log(1.0 + jnp.exp2(m + d)) * LOG2E
        return m - l, d - l

    car_up = acc_up = None
    for j in reversed(range(nq)):
        two = j + 1 < nq
        kj = slice(j * tq, (j + 1) * tq)
        q = q_ref[0, j * tq:(j + 2) * tq, :] if two else q_ref[0, kj, :]
        lb, lm = logits(q, k_ref[0, kj, :])
        lm_diag = jnp.where(below, lm[:tq], 0.0)
        lm = jnp.concatenate([lm_diag, lm[tq:]], axis=0) if two else lm_diag
        tail = jnp.dot(lm.astype(BF16), suffix, preferred_element_type=F32)
        x = lb + tail
        a = jnp.where(below, jnp.exp2(x[:tq]), 0.0)
        if two:
            a = jnp.concatenate([a, jnp.exp2(x[tq:] + car_up)], axis=0)
        pv = jnp.dot(a.astype(BF16), v_ref[0, kj, :], preferred_element_type=F32)
        rowsum = tail[:, :1] + lm[:, :1]
        if two:
            up = slice((j + 1) * tq, (j + 2) * tq)
            acc_ref[up, :] = acc_up + pv[tq:]
            car_ref[up, :] = car_up + rowsum[tq:]
        car_up, acc_up = rowsum[:tq], pv[:tq]
    acc_ref[:tq, :] = acc_up
    car_ref[:tq, :] = car_up

    if nq > 2:
        @pl.when(jnp.max(car_ref[2 * tq:, :]) > DEAD_LOG2)
        def _():
            def qblock(i, _):
                rows = pl.ds(pl.multiple_of(i * tq, tq), tq)
                q = q_ref[0, rows, :]

                def alive(st):
                    return jnp.logical_and(st[0] >= 0, jnp.max(st[1]) > DEAD_LOG2)

                def visit(st):
                    j, car, acc = st
                    ks = pl.ds(pl.multiple_of(j * tq, tq), tq)
                    lb, lm = logits(q, k_ref[0, ks, :])
                    tail = jnp.dot(lm.astype(BF16), suffix, preferred_element_type=F32)
                    a = jnp.exp2(lb + tail + car)
                    acc = acc + jnp.dot(a.astype(BF16), v_ref[0, ks, :], preferred_element_type=F32)
                    return j - 1, car + (tail[:, :1] + lm[:, :1]), acc

                _, car, acc = lax.while_loop(alive, visit, (i - 2, car_ref[rows, :], acc_ref[rows, :]))
                acc_ref[rows, :] = acc
                car_ref[rows, :] = car
                return 0
            lax.fori_loop(2, nq, qblock, 0)

    gain = g_ref[0]
    for i in range(nq):
        rows = slice(i * tq, (i + 1) * tq)
        o_ref[0, rows, :] = (_rms(acc_ref[rows, :]) * gain).astype(BF16)


def _attention(proj, gain, to_cast, *, batch, seq, heads, q_off, tq=256):
    steps = batch * heads
    bf16_rows = 16
    assert all(w.shape[0] % (steps * bf16_rows) == 0 for w in to_cast)
    kern = functools.partial(_attn_kernel, tq=tq, n_cast=len(to_cast))

    def spec(off):
        return pl.BlockSpec((1, seq, LANE), lambda b, h: (off + h, b, 0))

    cast_specs = [pl.BlockSpec((w.shape[0] // steps, w.shape[1]), lambda b, h: (b * heads + h, 0))
                  for w in to_cast]
    out, *casted = pl.pallas_call(
        kern,
        out_shape=[jax.ShapeDtypeStruct((heads, batch * seq, LANE), BF16)]
        + [jax.ShapeDtypeStruct(w.shape, BF16) for w in to_cast],
        grid=(batch, heads),
        in_specs=[spec(q_off), spec(q_off + heads), spec(q_off + 2 * heads),
                  pl.BlockSpec((1, 1, LANE), lambda b, h: (h, 0, 0))] + cast_specs,
        out_specs=[pl.BlockSpec((1, seq, LANE), lambda b, h: (h, b, 0))] + cast_specs,
        scratch_shapes=[pltpu.VMEM((seq, LANE), F32), pltpu.VMEM((seq, 1), F32)],
        compiler_params=_params("parallel", "parallel"),
        name="sb_attention",
    )(proj, proj, proj, gain.reshape(heads, 1, LANE), *to_cast)
    return out, casted


def _out_kernel(x_ref, a_ref, b_ref, w_ref, gate_ref, o_ref):
    pieces = [a_ref[c] for c in range(a_ref.shape[0])] + [b_ref[c] for c in range(b_ref.shape[0])]
    o = jnp.concatenate(pieces, axis=-1)
    y = jnp.dot(o, w_ref[...].astype(BF16), preferred_element_type=F32)
    o_ref[...] = x_ref[...] + gate_ref[0] * y


def _out_proj(x2d, o_gm, o_sb, w_out, gate, *, seq, tm=512):
    T, D = x2d.shape
    per_b = seq // tm
    return pl.pallas_call(
        _out_kernel,
        out_shape=jax.ShapeDtypeStruct((T, D), F32),
        grid=(T // tm,),
        in_specs=[pl.BlockSpec((tm, D), lambda i: (i, 0)),
                  pl.BlockSpec((o_gm.shape[0], tm, LANE), lambda i: (0, i, 0)),
                  pl.BlockSpec((o_sb.shape[0], tm, LANE), lambda i: (0, i, 0)),
                  pl.BlockSpec(w_out.shape, lambda i: (0, 0), pipeline_mode=pl.Buffered(1)),
                  pl.BlockSpec((1, 1, D), lambda i: (i // per_b, 0, 0))],
        out_specs=pl.BlockSpec((tm, D), lambda i: (i, 0)),
        compiler_params=_params("parallel"),
        name="out_proj",
    )(x2d, o_gm, o_sb, w_out, gate)


def _ffn_kernel(x_hbm, sc_ref, sh_ref, gate_ref, g_ref, wg_ref, wu_ref, wd_ref, fg_ref,
                o_ref, xbuf, h_ref, sem, *, final_norm, rc):
    i, j = pl.program_id(0), pl.program_id(1)
    tm = xbuf.shape[0]

    def x_copy(tile):
        return pltpu.make_async_copy(x_hbm.at[pl.ds(tile * tm, tm), :], xbuf, sem)

    @pl.when(j == 0)
    def _():
        @pl.when(i == 0)
        def _():
            x_copy(0).start()
        x_copy(i).wait()
        gs = g_ref[...] * (1.0 + sc_ref[0])
        sh = sh_ref[0]

        def chunk(r, carry):
            rows = pl.ds(pl.multiple_of(r * rc, rc), rc)
            x = xbuf[rows, :]
            h_ref[rows, :] = (_rms(x) * gs + sh).astype(BF16)
            o_ref[rows, :] = x
            return carry
        lax.fori_loop(0, tm // rc, chunk, 0, unroll=8)

    @pl.when(jnp.logical_and(j == 1, i + 1 < pl.num_programs(0)))
    def _():
        x_copy(i + 1).start()

    h = h_ref[...]
    gt = jnp.dot(h, wg_ref[...], preferred_element_type=F32)
    up = jnp.dot(h, wu_ref[...], preferred_element_type=F32)
    a = (gt * jax.nn.sigmoid(gt) * up).astype(BF16)
    o_ref[...] += gate_ref[0] * jnp.dot(a, wd_ref[...].astype(BF16), preferred_element_type=F32)

    if final_norm:
        @pl.when(j == pl.num_programs(1) - 1)
        def _():
            o_ref[...] = _rms(o_ref[...]) * fg_ref[...]


def _ffn(x2d, scale, shift, gate, norm_g, w_gate, w_up, w_down, final_g, *, seq, final_norm,
         tm=1024, tf=512, rc=16):
    T, D = x2d.shape
    FF = w_gate.shape[1]
    assert FF // tf >= 2
    per_b = seq // tm
    mod_spec = pl.BlockSpec((1, 1, D), lambda i, j: (i // per_b, 0, 0))
    vec_spec = pl.BlockSpec((1, D), lambda i, j: (0, 0))
    return pl.pallas_call(
        functools.partial(_ffn_kernel, final_norm=final_norm, rc=rc),
        out_shape=jax.ShapeDtypeStruct((T, D), F32),
        grid=(T // tm, FF // tf),
        in_specs=[pl.BlockSpec(memory_space=pl.ANY),
                  mod_spec, mod_spec, mod_spec, vec_spec,
                  pl.BlockSpec((D, tf), lambda i, j: (0, j)),
                  pl.BlockSpec((D, tf), lambda i, j: (0, j)),
                  pl.BlockSpec((tf, D), lambda i, j: (j, 0)),
                  vec_spec],
        out_specs=pl.BlockSpec((tm, D), lambda i, j: (i, 0)),
        scratch_shapes=[pltpu.VMEM((tm, D), F32), pltpu.VMEM((tm, D), BF16),
                        pltpu.SemaphoreType.DMA(())],
        compiler_params=_params("arbitrary", "arbitrary"),
        name="ffn",
    )(x2d, scale, shift, gate, norm_g.reshape(1, D), w_gate, w_up, w_down, final_g.reshape(1, D))


def kernel(x, c, w_ada, b_ada, norm1_g, w_in, v_norm_g, w_spatial, b_spatial, out_norm_g, w_out,
           norm2_g, w_gate, w_up, w_down, final_g):
    B, S, D = x.shape
    depth = w_ada.shape[0]
    gm_width = v_norm_g.shape[-1]
    sb_width = (w_in.shape[-1] - 2 * gm_width) // 3
    head_dim = sb_width // SB_HEADS
    assert head_dim == LANE and gm_width // w_spatial.shape[1] == LANE

    xf = x.reshape(B * S, D)
    for l in range(depth):
        mod = _modulation(c, w_ada[l], b_ada[l]).reshape(B, N_MOD, 1, D)
        shift1, scale1, gate1, shift2, scale2, gate2 = [mod[:, m] for m in range(N_MOD)]

        proj = _in_proj(xf, scale1, shift1, norm1_g[l], w_in[l], v_norm_g[l],
                        seq=S, gm_width=gm_width, sb_width=sb_width, head_dim=head_dim)
        o_gm = _gmlp(proj, w_spatial[l], b_spatial[l], out_norm_g[l, :gm_width])
        o_sb, (wg, wu) = _attention(proj, out_norm_g[l, gm_width:], (w_gate[l], w_up[l]), batch=B,
                                    seq=S, heads=SB_HEADS, q_off=2 * gm_width // LANE)
        xf = _out_proj(xf, o_gm, o_sb, w_out[l], gate1, seq=S)
        xf = _ffn(xf, scale2, shift2, gate2, norm2_g[l], wg, wu, w_down[l], final_g,
                  seq=S, final_norm=(l == depth - 1))
    return xf.reshape(B, S, D)
```

```python
import functools

import jax
import jax.numpy as jnp
from jax import lax
from jax.experimental import pallas as pl
from jax.experimental.pallas import tpu as pltpu

EPS = 1e-6
N_MOD = 6
SB_HEADS = 8
LANE = 128
VMEM_LIMIT = 56 * 1024 * 1024
LOG2E = 1.4426950408889634
DEAD_LOG2 = -150.0

F32 = jnp.float32
BF16 = jnp.bfloat16


def _params(*sem):
    return pltpu.CompilerParams(dimension_semantics=sem, vmem_limit_bytes=VMEM_LIMIT)


def _rms(x):
    return x * lax.rsqrt(jnp.mean(x * x, axis=-1, keepdims=True) + EPS)


def _mod_kernel(c_ref, w_ref, b_ref, o_ref):
    c = c_ref[...]
    c_act = (c * jax.nn.sigmoid(c)).astype(BF16)
    o_ref[...] = jnp.dot(c_act, w_ref[...].astype(BF16), preferred_element_type=F32) + b_ref[...]


def _modulation(c, w_ada, b_ada, tn=2048):
    B, D = c.shape
    N = w_ada.shape[1]
    return pl.pallas_call(
        _mod_kernel,
        out_shape=jax.ShapeDtypeStruct((B, N), F32),
        grid=(N // tn,),
        in_specs=[pl.BlockSpec((B, D), lambda j: (0, 0)),
                  pl.BlockSpec((D, tn), lambda j: (0, j)),
                  pl.BlockSpec((1, tn), lambda j: (0, j))],
        out_specs=pl.BlockSpec((B, tn), lambda j: (0, j)),
        compiler_params=_params("parallel"),
        name="modulation",
    )(c, w_ada, b_ada.reshape(1, N))


def _gelu(x):
    return 0.5 * x * (1.0 + lax.erf(x * 0.7071067811865476))


def _in_kernel(x_ref, sc_ref, sh_ref, g_ref, w_ref, vg_ref, o_ref, h_ref,
               *, n_gm, n_seg, q_scale, pc):
    j = pl.program_id(1)
    tm = h_ref.shape[0]
    ncb = o_ref.shape[0]

    @pl.when(j == 0)
    def _():
        gs = g_ref[...] * (1.0 + sc_ref[0])
        sh = sh_ref[0]

        def chunk(r, carry):
            rows = pl.ds(pl.multiple_of(r * pc, pc), pc)
            h_ref[rows, :] = (_rms(x_ref[rows, :]) * gs + sh).astype(BF16)
            return carry
        lax.fori_loop(0, tm // pc, chunk, 0, unroll=8)

    def epilogue(fn):
        acc = jnp.dot(h_ref[...], w_ref[...].astype(BF16), preferred_element_type=F32)
        for cb in range(ncb):
            o_ref[cb] = fn(acc[:, cb * LANE:(cb + 1) * LANE], cb).astype(BF16)

    @pl.when(j < n_gm)
    def _():
        epilogue(lambda blk, cb: _gelu(blk))

    @pl.when(jnp.logical_and(j >= n_gm, j < 2 * n_gm))
    def _():
        def fn(blk, cb):
            z = _gelu(blk)
            zc = z - jnp.mean(z, axis=-1, keepdims=True)
            return _rms(zc) * vg_ref[0, :, cb * LANE:(cb + 1) * LANE]
        epilogue(fn)

    @pl.when(jnp.logical_and(j >= 2 * n_gm, j < 2 * n_gm + n_seg))
    def _():
        epilogue(lambda blk, cb: blk * q_scale)

    @pl.when(j >= 2 * n_gm + n_seg)
    def _():
        epilogue(lambda blk, cb: blk)


def _in_proj(x2d, scale, shift, norm_g, w_in, v_norm_g, *, seq, gm_width, sb_width, head_dim,
             tm=1024, tn=1024, pc=16):
    T, D = x2d.shape
    N = w_in.shape[1]
    assert gm_width % tn == 0 and sb_width % tn == 0 and seq % tm == 0
    n_gm, n_seg = gm_width // tn, sb_width // tn
    per_b = seq // tm
    kern = functools.partial(_in_kernel, n_gm=n_gm, n_seg=n_seg, q_scale=LOG2E * head_dim ** -0.5, pc=pc)
    return pl.pallas_call(
        kern,
        out_shape=jax.ShapeDtypeStruct((N // LANE, T, LANE), BF16),
        grid=(T // tm, N // tn),
        in_specs=[pl.BlockSpec((tm, D), lambda i, j: (i, 0)),
                  pl.BlockSpec((1, 1, D), lambda i, j: (i // per_b, 0, 0)),
                  pl.BlockSpec((1, 1, D), lambda i, j: (i // per_b, 0, 0)),
                  pl.BlockSpec((1, D), lambda i, j: (0, 0)),
                  pl.BlockSpec((D, tn), lambda i, j: (0, j)),
                  pl.BlockSpec((1, 1, tn), lambda i, j: (jnp.clip(j - n_gm, 0, n_gm - 1), 0, 0))],
        out_specs=pl.BlockSpec((tn // LANE, tm, LANE), lambda i, j: (j, i, 0)),
        scratch_shapes=[pltpu.VMEM((tm, D), BF16)],
        compiler_params=_params("parallel", "arbitrary"),
        name="in_proj",
    )(x2d, scale, shift, norm_g.reshape(1, D), w_in, v_norm_g.reshape(n_gm, 1, tn))


def _gmlp_kernel(u_ref, v_ref, w_ref, b_ref, g_ref, o_ref, *, chunk):
    groups, tr, _ = u_ref.shape
    n_ch = tr // chunk
    causal = (lax.broadcasted_iota(jnp.int32, (chunk, chunk), 0)
              >= lax.broadcasted_iota(jnp.int32, (chunk, chunk), 1))
    for g in range(groups):
        w = jnp.where(causal, w_ref[g], 0.0).astype(BF16)
        v_cat = jnp.concatenate([v_ref[g, c * chunk:(c + 1) * chunk, :] for c in range(n_ch)], axis=1)
        mixed = jnp.dot(w, v_cat, preferred_element_type=F32) + b_ref[g]
        for c in range(n_ch):
            rows = slice(c * chunk, (c + 1) * chunk)
            o = u_ref[g, rows, :].astype(F32) * mixed[:, c * LANE:(c + 1) * LANE]
            o_ref[g, rows, :] = (_rms(o) * g_ref[g]).astype(BF16)


def _gmlp(proj, w_s, b_s, gain, *, tr=2048):
    _, T, _ = proj.shape
    groups, chunk, _ = w_s.shape
    blk = pl.BlockSpec((groups, tr, LANE), lambda r: (0, r, 0))
    return pl.pallas_call(
        functools.partial(_gmlp_kernel, chunk=chunk),
        out_shape=jax.ShapeDtypeStruct((groups, T, LANE), BF16),
        grid=(T // tr,),
        in_specs=[blk,
                  pl.BlockSpec((groups, tr, LANE), lambda r: (1, r, 0)),
                  pl.BlockSpec((groups, chunk, chunk), lambda r: (0, 0, 0)),
                  pl.BlockSpec((groups, chunk, 1), lambda r: (0, 0, 0)),
                  pl.BlockSpec((groups, 1, LANE), lambda r: (0, 0, 0))],
        out_specs=blk,
        compiler_params=_params("parallel"),
        name="gmlp",
    )(proj, proj, w_s, b_s.reshape(groups, chunk, 1), gain.reshape(groups, 1, LANE))


def _attn_kernel(q_ref, k_ref, v_ref, g_ref, *rest, tq, n_cast):
    cast_in, o_ref, cast_out = rest[:n_cast], rest[n_cast], rest[n_cast + 1:2 * n_cast + 1]
    acc_ref, car_ref = rest[2 * n_cast + 1:]
    for src, dst in zip(cast_in, cast_out):
        dst[...] = src[...].astype(BF16)

    S = q_ref.shape[1]
    nq = S // tq
    ri = lax.broadcasted_iota(jnp.int32, (tq, tq), 0)
    ci = lax.broadcasted_iota(jnp.int32, (tq, tq), 1)
    below = ri > ci
    suffix = jnp.where(below, 1.0, 0.0).astype(BF16)

    def logits(q, k):
        z = lax.dot_general(q, k, (((1,), (1,)), ((), ())), preferred_element_type=F32)
        m = jnp.minimum(z, 0.0)
        d = m - z
        l = jnp.log(1.0 + jnp.exp2(m + d)) * LOG2E
        return m - l, d - l

    car_up = acc_up = None
    for j in reversed(range(nq)):
        two = j + 1 < nq
        kj = slice(j * tq, (j + 1) * tq)
        q = q_ref[0, j * tq:(j + 2) * tq, :] if two else q_ref[0, kj, :]
        lb, lm = logits(q, k_ref[0, kj, :])
        lm_diag = jnp.where(below, lm[:tq], 0.0)
        lm = jnp.concatenate([lm_diag, lm[tq:]], axis=0) if two else lm_diag
        tail = jnp.dot(lm.astype(BF16), suffix, preferred_element_type=F32)
        x = lb + tail
        a = jnp.where(below, jnp.exp2(x[:tq]), 0.0)
        if two:
            a = jnp.concatenate([a, jnp.exp2(x[tq:] + car_up)], axis=0)
        pv = jnp.dot(a.astype(BF16), v_ref[0, kj, :], preferred_element_type=F32)
        rowsum = tail[:, :1] + lm[:, :1]
        if two:
            up = slice((j + 1) * tq, (j + 2) * tq)
            acc_ref[up, :] = acc_up + pv[tq:]
            car_ref[up, :] = car_up + rowsum[tq:]
        car_up, acc_up = rowsum[:tq], pv[:tq]
    acc_ref[:tq, :] = acc_up
    car_ref[:tq, :] = car_up

    if nq > 2:
        @pl.when(jnp.max(car_ref[2 * tq:, :]) > DEAD_LOG2)
        def _():
            def qblock(i, _):
                rows = pl.ds(pl.multiple_of(i * tq, tq), tq)
                q = q_ref[0, rows, :]

                def alive(st):
                    return jnp.logical_and(st[0] >= 0, jnp.max(st[1]) > DEAD_LOG2)

                def visit(st):
                    j, car, acc = st
                    ks = pl.ds(pl.multiple_of(j * tq, tq), tq)
                    lb, lm = logits(q, k_ref[0, ks, :])
                    tail = jnp.dot(lm.astype(BF16), suffix, preferred_element_type=F32)
                    a = jnp.exp2(lb + tail + car)
                    acc = acc + jnp.dot(a.astype(BF16), v_ref[0, ks, :], preferred_element_type=F32)
                    return j - 1, car + (tail[:, :1] + lm[:, :1]), acc

                _, car, acc = lax.while_loop(alive, visit, (i - 2, car_ref[rows, :], acc_ref[rows, :]))
                acc_ref[rows, :] = acc
                car_ref[rows, :] = car
                return 0
            lax.fori_loop(2, nq, qblock, 0)

    gain = g_ref[0]
    for i in range(nq):
        rows = slice(i * tq, (i + 1) * tq)
        o_ref[0, rows, :] = (_rms(acc_ref[rows, :]) * gain).astype(BF16)


def _attention(proj, gain, to_cast, *, batch, seq, heads, q_off, tq=256):
    steps = batch * heads
    bf16_rows = 16
    assert all(w.shape[0] % (steps * bf16_rows) == 0 for w in to_cast)
    kern = functools.partial(_attn_kernel, tq=tq, n_cast=len(to_cast))

    def spec(off):
        return pl.BlockSpec((1, seq, LANE), lambda b, h: (off + h, b, 0))

    cast_specs = [pl.BlockSpec((w.shape[0] // steps, w.shape[1]), lambda b, h: (b * heads + h, 0))
                  for w in to_cast]
    out, *casted = pl.pallas_call(
        kern,
        out_shape=[jax.ShapeDtypeStruct((heads, batch * seq, LANE), BF16)]
        + [jax.ShapeDtypeStruct(w.shape, BF16) for w in to_cast],
        grid=(batch, heads),
        in_specs=[spec(q_off), spec(q_off + heads), spec(q_off + 2 * heads),
                  pl.BlockSpec((1, 1, LANE), lambda b, h: (h, 0, 0))] + cast_specs,
        out_specs=[pl.BlockSpec((1, seq, LANE), lambda b, h: (h, b, 0))] + cast_specs,
        scratch_shapes=[pltpu.VMEM((seq, LANE), F32), pltpu.VMEM((seq, 1), F32)],
        compiler_params=_params("parallel", "parallel"),
        name="sb_attention",
    )(proj, proj, proj, gain.reshape(heads, 1, LANE), *to_cast)
    return out, casted


def _out_kernel(x_ref, a_ref, b_ref, w_ref, gate_ref, o_ref):
    pieces = [a_ref[c] for c in range(a_ref.shape[0])] + [b_ref[c] for c in range(b_ref.shape[0])]
    o = jnp.concatenate(pieces, axis=-1)
    y = jnp.dot(o, w_ref[...], preferred_element_type=F32)
    o_ref[...] = x_ref[...] + gate_ref[0] * y


def _out_proj(x2d, o_gm, o_sb, w_out, gate, *, seq, tm=1024):
    T, D = x2d.shape
    per_b = seq // tm
    return pl.pallas_call(
        _out_kernel,
        out_shape=jax.ShapeDtypeStruct((T, D), F32),
        grid=(T // tm,),
        in_specs=[pl.BlockSpec((tm, D), lambda i: (i, 0)),
                  pl.BlockSpec((o_gm.shape[0], tm, LANE), lambda i: (0, i, 0)),
                  pl.BlockSpec((o_sb.shape[0], tm, LANE), lambda i: (0, i, 0)),
                  pl.BlockSpec(w_out.shape, lambda i: (0, 0), pipeline_mode=pl.Buffered(1)),
                  pl.BlockSpec((1, 1, D), lambda i: (i // per_b, 0, 0))],
        out_specs=pl.BlockSpec((tm, D), lambda i: (i, 0)),
        compiler_params=_params("parallel"),
        name="out_proj",
    )(x2d, o_gm, o_sb, w_out, gate)


def _ffn_kernel(x_hbm, sc_ref, sh_ref, gate_ref, g_ref, wg_ref, wu_ref, wd_ref, fg_ref,
                o_ref, xbuf, h_ref, sem, *, final_norm, rc):
    i, j = pl.program_id(0), pl.program_id(1)
    tm = xbuf.shape[0]

    def x_copy(tile):
        return pltpu.make_async_copy(x_hbm.at[pl.ds(tile * tm, tm), :], xbuf, sem)

    @pl.when(j == 0)
    def _():
        @pl.when(i == 0)
        def _():
            x_copy(0).start()
        x_copy(i).wait()
        gs = g_ref[...] * (1.0 + sc_ref[0])
        sh = sh_ref[0]

        def chunk(r, carry):
            rows = pl.ds(pl.multiple_of(r * rc, rc), rc)
            x = xbuf[rows, :]
            h_ref[rows, :] = (_rms(x) * gs + sh).astype(BF16)
            o_ref[rows, :] = x
            return carry
        lax.fori_loop(0, tm // rc, chunk, 0, unroll=8)

    @pl.when(jnp.logical_and(j == 1, i + 1 < pl.num_programs(0)))
    def _():
        x_copy(i + 1).start()

    h = h_ref[...]
    gt = jnp.dot(h, wg_ref[...], preferred_element_type=F32)
    up = jnp.dot(h, wu_ref[...], preferred_element_type=F32)
    half = 0.5 * gt
    a = ((half * jnp.tanh(half) + half) * up).astype(BF16)
    o_ref[...] += gate_ref[0] * jnp.dot(a, wd_ref[...].astype(BF16), preferred_element_type=F32)

    if final_norm:
        @pl.when(j == pl.num_programs(1) - 1)
        def _():
            o_ref[...] = _rms(o_ref[...]) * fg_ref[...]


def _ffn(x2d, scale, shift, gate, norm_g, w_gate, w_up, w_down, final_g, *, seq, final_norm,
         tm=1024, tf=512, rc=16):
    T, D = x2d.shape
    FF = w_gate.shape[1]
    assert FF // tf >= 2
    per_b = seq // tm
    mod_spec = pl.BlockSpec((1, 1, D), lambda i, j: (i // per_b, 0, 0))
    vec_spec = pl.BlockSpec((1, D), lambda i, j: (0, 0))
    return pl.pallas_call(
        functools.partial(_ffn_kernel, final_norm=final_norm, rc=rc),
        out_shape=jax.ShapeDtypeStruct((T, D), F32),
        grid=(T // tm, FF // tf),
        in_specs=[pl.BlockSpec(memory_space=pl.ANY),
                  mod_spec, mod_spec, mod_spec, vec_spec,
                  pl.BlockSpec((D, tf), lambda i, j: (0, j)),
                  pl.BlockSpec((D, tf), lambda i, j: (0, j)),
                  pl.BlockSpec((tf, D), lambda i, j: (j, 0)),
                  vec_spec],
        out_specs=pl.BlockSpec((tm, D), lambda i, j: (i, 0)),
        scratch_shapes=[pltpu.VMEM((tm, D), F32), pltpu.VMEM((tm, D), BF16),
                        pltpu.SemaphoreType.DMA(())],
        compiler_params=_params("arbitrary", "arbitrary"),
        name="ffn",
    )(x2d, scale, shift, gate, norm_g.reshape(1, D), w_gate, w_up, w_down, final_g.reshape(1, D))


def kernel(x, c, w_ada, b_ada, norm1_g, w_in, v_norm_g, w_spatial, b_spatial, out_norm_g, w_out,
           norm2_g, w_gate, w_up, w_down, final_g):
    B, S, D = x.shape
    depth = w_ada.shape[0]
    gm_width = v_norm_g.shape[-1]
    sb_width = (w_in.shape[-1] - 2 * gm_width) // 3
    head_dim = sb_width // SB_HEADS
    assert head_dim == LANE and gm_width // w_spatial.shape[1] == LANE

    xf = x.reshape(B * S, D)
    for l in range(depth):
        mod = _modulation(c, w_ada[l], b_ada[l]).reshape(B, N_MOD, 1, D)
        shift1, scale1, gate1, shift2, scale2, gate2 = [mod[:, m] for m in range(N_MOD)]

        proj = _in_proj(xf, scale1, shift1, norm1_g[l], w_in[l], v_norm_g[l],
                        seq=S, gm_width=gm_width, sb_width=sb_width, head_dim=head_dim)
        o_gm = _gmlp(proj, w_spatial[l], b_spatial[l], out_norm_g[l, :gm_width])
        o_sb, (wo, wg, wu) = _attention(proj, out_norm_g[l, gm_width:], (w_out[l], w_gate[l], w_up[l]),
                                        batch=B, seq=S, heads=SB_HEADS, q_off=2 * gm_width // LANE)
        xf = _out_proj(xf, o_gm, o_sb, wo, gate1, seq=S)
        xf = _ffn(xf, scale2, shift2, gate2, norm2_g[l], wg, wu, w_down[l], final_g,
                  seq=S, final_norm=(l == depth - 1))
    return xf.reshape(B, S, D)
```

```python
import functools

import jax
import jax.numpy as jnp
from jax import lax
from jax.experimental import pallas as pl
from jax.experimental.pallas import tpu as pltpu

EPS = 1e-6
N_MOD = 6
SB_HEADS = 8
LANE = 128
VMEM_LIMIT = 56 * 1024 * 1024
LOG2E = 1.4426950408889634
DEAD_LOG2 = -150.0

F32 = jnp.float32
BF16 = jnp.bfloat16


def _params(*sem):
    return pltpu.CompilerParams(dimension_semantics=sem, vmem_limit_bytes=VMEM_LIMIT)


def _rms(x):
    return x * lax.rsqrt(jnp.mean(x * x, axis=-1, keepdims=True) + EPS)


def _mod_kernel(c_ref, w_ref, b_ref, o_ref):
    c = c_ref[...]
    c_act = (c * jax.nn.sigmoid(c)).astype(BF16)
    o_ref[...] = jnp.dot(c_act, w_ref[...].astype(BF16), preferred_element_type=F32) + b_ref[...]


def _modulation(c, w_ada, b_ada, tn=1024):
    B, D = c.shape
    N = w_ada.shape[1]
    return pl.pallas_call(
        _mod_kernel,
        out_shape=jax.ShapeDtypeStruct((B, N), F32),
        grid=(N // tn,),
        in_specs=[pl.BlockSpec((B, D), lambda j: (0, 0)),
                  pl.BlockSpec((D, tn), lambda j: (0, j)),
                  pl.BlockSpec((1, tn), lambda j: (0, j))],
        out_specs=pl.BlockSpec((B, tn), lambda j: (0, j)),
        compiler_params=_params("parallel"),
        name="modulation",
    )(c, w_ada, b_ada.reshape(1, N))


def _gelu(x):
    return 0.5 * x * (1.0 + lax.erf(x * 0.7071067811865476))


def _in_kernel(x_ref, sc_ref, sh_ref, g_ref, w_ref, vg_ref, o_ref, h_ref,
               *, n_gm, n_seg, q_scale, pc):
    j = pl.program_id(1)
    tm = h_ref.shape[0]
    ncb = o_ref.shape[0]

    @pl.when(j == 0)
    def _():
        gs = g_ref[...] * (1.0 + sc_ref[0])
        sh = sh_ref[0]

        def chunk(r, carry):
            rows = pl.ds(pl.multiple_of(r * pc, pc), pc)
            h_ref[rows, :] = (_rms(x_ref[rows, :]) * gs + sh).astype(BF16)
            return carry
        lax.fori_loop(0, tm // pc, chunk, 0, unroll=8)

    def epilogue(fn):
        acc = jnp.dot(h_ref[...], w_ref[...].astype(BF16), preferred_element_type=F32)
        for cb in range(ncb):
            o_ref[cb] = fn(acc[:, cb * LANE:(cb + 1) * LANE], cb).astype(BF16)

    @pl.when(j < n_gm)
    def _():
        epilogue(lambda blk, cb: _gelu(blk))

    @pl.when(jnp.logical_and(j >= n_gm, j < 2 * n_gm))
    def _():
        def fn(blk, cb):
            z = _gelu(blk)
            zc = z - jnp.mean(z, axis=-1, keepdims=True)
            return _rms(zc) * vg_ref[0, :, cb * LANE:(cb + 1) * LANE]
        epilogue(fn)

    @pl.when(jnp.logical_and(j >= 2 * n_gm, j < 2 * n_gm + n_seg))
    def _():
        epilogue(lambda blk, cb: blk * q_scale)

    @pl.when(j >= 2 * n_gm + n_seg)
    def _():
        epilogue(lambda blk, cb: blk)


def _in_proj(x2d, scale, shift, norm_g, w_in, v_norm_g, *, seq, gm_width, sb_width, head_dim,
             tm=1024, tn=1024, pc=32):
    T, D = x2d.shape
    N = w_in.shape[1]
    assert gm_width % tn == 0 and sb_width % tn == 0 and seq % tm == 0
    n_gm, n_seg = gm_width // tn, sb_width // tn
    per_b = seq // tm
    kern = functools.partial(_in_kernel, n_gm=n_gm, n_seg=n_seg, q_scale=LOG2E * head_dim ** -0.5, pc=pc)
    return pl.pallas_call(
        kern,
        out_shape=jax.ShapeDtypeStruct((N // LANE, T, LANE), BF16),
        grid=(T // tm, N // tn),
        in_specs=[pl.BlockSpec((tm, D), lambda i, j: (i, 0)),
                  pl.BlockSpec((1, 1, D), lambda i, j: (i // per_b, 0, 0)),
                  pl.BlockSpec((1, 1, D), lambda i, j: (i // per_b, 0, 0)),
                  pl.BlockSpec((1, D), lambda i, j: (0, 0)),
                  pl.BlockSpec((D, tn), lambda i, j: (0, j)),
                  pl.BlockSpec((1, 1, tn), lambda i, j: (jnp.clip(j - n_gm, 0, n_gm - 1), 0, 0))],
        out_specs=pl.BlockSpec((tn // LANE, tm, LANE), lambda i, j: (j, i, 0)),
        scratch_shapes=[pltpu.VMEM((tm, D), BF16)],
        compiler_params=_params("parallel", "arbitrary"),
        name="in_proj",
    )(x2d, scale, shift, norm_g.reshape(1, D), w_in, v_norm_g.reshape(n_gm, 1, tn))


def _gmlp_kernel(u_ref, v_ref, w_ref, b_ref, g_ref, o_ref, *, chunk):
    groups, tr, _ = u_ref.shape
    n_ch = tr // chunk
    causal = (lax.broadcasted_iota(jnp.int32, (chunk, chunk), 0)
              >= lax.broadcasted_iota(jnp.int32, (chunk, chunk), 1))
    for g in range(groups):
        w = jnp.where(causal, w_ref[g], 0.0).astype(BF16)
        v_cat = jnp.concatenate([v_ref[g, c * chunk:(c + 1) * chunk, :] for c in range(n_ch)], axis=1)
        mixed = jnp.dot(w, v_cat, preferred_element_type=F32) + b_ref[g]
        for c in range(n_ch):
            rows = slice(c * chunk, (c + 1) * chunk)
            o = u_ref[g, rows, :].astype(F32) * mixed[:, c * LANE:(c + 1) * LANE]
            o_ref[g, rows, :] = (_rms(o) * g_ref[g]).astype(BF16)


def _gmlp(proj, w_s, b_s, gain, *, tr=2048):
    _, T, _ = proj.shape
    groups, chunk, _ = w_s.shape
    blk = pl.BlockSpec((groups, tr, LANE), lambda r: (0, r, 0))
    return pl.pallas_call(
        functools.partial(_gmlp_kernel, chunk=chunk),
        out_shape=jax.ShapeDtypeStruct((groups, T, LANE), BF16),
        grid=(T // tr,),
        in_specs=[blk,
                  pl.BlockSpec((groups, tr, LANE), lambda r: (1, r, 0)),
                  pl.BlockSpec((groups, chunk, chunk), lambda r: (0, 0, 0)),
                  pl.BlockSpec((groups, chunk, 1), lambda r: (0, 0, 0)),
                  pl.BlockSpec((groups, 1, LANE), lambda r: (0, 0, 0))],
        out_specs=blk,
        compiler_params=_params("parallel"),
        name="gmlp",
    )(proj, proj, w_s, b_s.reshape(groups, chunk, 1), gain.reshape(groups, 1, LANE))


def _attn_kernel(q_ref, k_ref, v_ref, g_ref, *rest, tq, n_cast):
    cast_in, o_ref, cast_out = rest[:n_cast], rest[n_cast], rest[n_cast + 1:2 * n_cast + 1]
    acc_ref, car_ref = rest[2 * n_cast + 1:]
    for src, dst in zip(cast_in, cast_out):
        dst[...] = src[...].astype(BF16)

    hp, S, _ = q_ref.shape
    nq = S // tq
    ri = lax.broadcasted_iota(jnp.int32, (tq, tq), 0)
    ci = lax.broadcasted_iota(jnp.int32, (tq, tq), 1)
    below = ri > ci
    suffix = jnp.where(below, 1.0, 0.0).astype(BF16)

    def logits(q, k):
        z = lax.dot_general(q, k, (((1,), (1,)), ((), ())), preferred_element_type=F32)
        m = jnp.minimum(z, 0.0)
        d = m - z
        l = jnp.log(1.0 + jnp.exp2(m + d)) * LOG2E
        return m - l, d - l

    car_up, acc_up = [None] * hp, [None] * hp
    for j in reversed(range(nq)):
        two = j + 1 < nq
        kj = slice(j * tq, (j + 1) * tq)
        for h in range(hp):
            q = q_ref[h, j * tq:(j + 2) * tq, :] if two else q_ref[h, kj, :]
            lb, lm = logits(q, k_ref[h, kj, :])
            lm_diag = jnp.where(below, lm[:tq], 0.0)
            lm = jnp.concatenate([lm_diag, lm[tq:]], axis=0) if two else lm_diag
            tail = jnp.dot(lm.astype(BF16), suffix, preferred_element_type=F32)
            x = lb + tail
            a = jnp.where(below, jnp.exp2(x[:tq]), 0.0)
            if two:
                a = jnp.concatenate([a, jnp.exp2(x[tq:] + car_up[h])], axis=0)
            pv = jnp.dot(a.astype(BF16), v_ref[h, kj, :], preferred_element_type=F32)
            rowsum = tail[:, :1] + lm[:, :1]
            if two:
                up = slice((j + 1) * tq, (j + 2) * tq)
                acc_ref[h, up, :] = acc_up[h] + pv[tq:]
                car_ref[h, up, :] = car_up[h] + rowsum[tq:]
            car_up[h], acc_up[h] = rowsum[:tq], pv[:tq]
    for h in range(hp):
        acc_ref[h, :tq, :] = acc_up[h]
        car_ref[h, :tq, :] = car_up[h]

    if nq > 2:
        @pl.when(jnp.max(car_ref[:, 2 * tq:, :]) > DEAD_LOG2)
        def _():
            def qblock(t, _):
                h, i = t // (nq - 2), 2 + t % (nq - 2)
                rows = pl.ds(pl.multiple_of(i * tq, tq), tq)
                q = q_ref[h, rows, :]

                def alive(st):
                    return jnp.logical_and(st[0] >= 0, jnp.max(st[1]) > DEAD_LOG2)

                def visit(st):
                    j, car, acc = st
                    ks = pl.ds(pl.multiple_of(j * tq, tq), tq)
                    lb, lm = logits(q, k_ref[h, ks, :])
                    tail = jnp.dot(lm.astype(BF16), suffix, preferred_element_type=F32)
                    a = jnp.exp2(lb + tail + car)
                    acc = acc + jnp.dot(a.astype(BF16), v_ref[h, ks, :], preferred_element_type=F32)
                    return j - 1, car + (tail[:, :1] + lm[:, :1]), acc

                _, car, acc = lax.while_loop(alive, visit,
                                             (i - 2, car_ref[h, rows, :], acc_ref[h, rows, :]))
                acc_ref[h, rows, :] = acc
                car_ref[h, rows, :] = car
                return 0
            lax.fori_loop(0, hp * (nq - 2), qblock, 0)

    for h in range(hp):
        gain = g_ref[h]
        for i in range(nq):
            rows = slice(i * tq, (i + 1) * tq)
            o_ref[h, rows, :] = (_rms(acc_ref[h, rows, :]) * gain).astype(BF16)


def _attention(proj, gain, to_cast, *, batch, seq, heads, q_off, tq=256, hp=2):
    assert heads % hp == 0 and q_off % hp == 0
    pairs = heads // hp
    steps = batch * pairs
    bf16_rows = 16
    assert all(w.shape[0] % (steps * bf16_rows) == 0 for w in to_cast)
    kern = functools.partial(_attn_kernel, tq=tq, n_cast=len(to_cast))

    def spec(off):
        return pl.BlockSpec((hp, seq, LANE), lambda b, p: (off // hp + p, b, 0))

    cast_specs = [pl.BlockSpec((w.shape[0] // steps, w.shape[1]), lambda b, p: (b * pairs + p, 0))
                  for w in to_cast]
    out, *casted = pl.pallas_call(
        kern,
        out_shape=[jax.ShapeDtypeStruct((heads, batch * seq, LANE), BF16)]
        + [jax.ShapeDtypeStruct(w.shape, BF16) for w in to_cast],
        grid=(batch, pairs),
        in_specs=[spec(q_off), spec(q_off + heads), spec(q_off + 2 * heads),
                  pl.BlockSpec((hp, 1, LANE), lambda b, p: (p, 0, 0))] + cast_specs,
        out_specs=[pl.BlockSpec((hp, seq, LANE), lambda b, p: (p, b, 0))] + cast_specs,
        scratch_shapes=[pltpu.VMEM((hp, seq, LANE), F32), pltpu.VMEM((hp, seq, 1), F32)],
        compiler_params=_params("parallel", "parallel"),
        name="sb_attention",
    )(proj, proj, proj, gain.reshape(heads, 1, LANE), *to_cast)
    return out, casted


def _out_kernel(x_ref, a_ref, b_ref, w_ref, gate_ref, o_ref):
    pieces = [a_ref[c] for c in range(a_ref.shape[0])] + [b_ref[c] for c in range(b_ref.shape[0])]
    o = jnp.concatenate(pieces, axis=-1)
    y = jnp.dot(o, w_ref[...], preferred_element_type=F32)
    o_ref[...] = x_ref[...] + gate_ref[0] * y


def _out_proj(x2d, o_gm, o_sb, w_out, gate, *, seq, tm=1024):
    T, D = x2d.shape
    per_b = seq // tm
    return pl.pallas_call(
        _out_kernel,
        out_shape=jax.ShapeDtypeStruct((T, D), F32),
        grid=(T // tm,),
        in_specs=[pl.BlockSpec((tm, D), lambda i: (i, 0)),
                  pl.BlockSpec((o_gm.shape[0], tm, LANE), lambda i: (0, i, 0)),
                  pl.BlockSpec((o_sb.shape[0], tm, LANE), lambda i: (0, i, 0)),
                  pl.BlockSpec(w_out.shape, lambda i: (0, 0), pipeline_mode=pl.Buffered(1)),
                  pl.BlockSpec((1, 1, D), lambda i: (i // per_b, 0, 0))],
        out_specs=pl.BlockSpec((tm, D), lambda i: (i, 0)),
        compiler_params=_params("parallel"),
        name="out_proj",
    )(x2d, o_gm, o_sb, w_out, gate)


def _ffn_kernel(x_hbm, sc_ref, sh_ref, gate_ref, g_ref, wg_ref, wu_ref, wd_ref, fg_ref,
                o_ref, xbuf, h_ref, sem, *, final_norm, rc):
    i, j = pl.program_id(0), pl.program_id(1)
    tm = xbuf.shape[0]

    def x_copy(tile):
        return pltpu.make_async_copy(x_hbm.at[pl.ds(tile * tm, tm), :], xbuf, sem)

    @pl.when(j == 0)
    def _():
        @pl.when(i == 0)
        def _():
            x_copy(0).start()
        x_copy(i).wait()
        gs = g_ref[...] * (1.0 + sc_ref[0])
        sh = sh_ref[0]

        def chunk(r, carry):
            rows = pl.ds(pl.multiple_of(r * rc, rc), rc)
            x = xbuf[rows, :]
            h_ref[rows, :] = (_rms(x) * gs + sh).astype(BF16)
            o_ref[rows, :] = x
            return carry
        lax.fori_loop(0, tm // rc, chunk, 0, unroll=8)

    @pl.when(jnp.logical_and(j == 1, i + 1 < pl.num_programs(0)))
    def _():
        x_copy(i + 1).start()

    h = h_ref[...]
    gt = jnp.dot(h, wg_ref[...], preferred_element_type=F32)
    up = jnp.dot(h, wu_ref[...], preferred_element_type=F32)
    half = 0.5 * gt
    a = ((half * jnp.tanh(half) + half) * up).astype(BF16)
    o_ref[...] += gate_ref[0] * jnp.dot(a, wd_ref[...].astype(BF16), preferred_element_type=F32)

    if final_norm:
        @pl.when(j == pl.num_programs(1) - 1)
        def _():
            o_ref[...] = _rms(o_ref[...]) * fg_ref[...]


def _ffn(x2d, scale, shift, gate, norm_g, w_gate, w_up, w_down, final_g, *, seq, final_norm,
         tm=1024, tf=512, rc=16):
    T, D = x2d.shape
    FF = w_gate.shape[1]
    assert FF // tf >= 2
    per_b = seq // tm
    mod_spec = pl.BlockSpec((1, 1, D), lambda i, j: (i // per_b, 0, 0))
    vec_spec = pl.BlockSpec((1, D), lambda i, j: (0, 0))
    return pl.pallas_call(
        functools.partial(_ffn_kernel, final_norm=final_norm, rc=rc),
        out_shape=jax.ShapeDtypeStruct((T, D), F32),
        grid=(T // tm, FF // tf),
        in_specs=[pl.BlockSpec(memory_space=pl.ANY),
                  mod_spec, mod_spec, mod_spec, vec_spec,
                  pl.BlockSpec((D, tf), lambda i, j: (0, j)),
                  pl.BlockSpec((D, tf), lambda i, j: (0, j)),
                  pl.BlockSpec((tf, D), lambda i, j: (j, 0)),
                  vec_spec],
        out_specs=pl.BlockSpec((tm, D), lambda i, j: (i, 0)),
        scratch_shapes=[pltpu.VMEM((tm, D), F32), pltpu.VMEM((tm, D), BF16),
                        pltpu.SemaphoreType.DMA(())],
        compiler_params=_params("arbitrary", "arbitrary"),
        name="ffn",
    )(x2d, scale, shift, gate, norm_g.reshape(1, D), w_gate, w_up, w_down, final_g.reshape(1, D))


def kernel(x, c, w_ada, b_ada, norm1_g, w_in, v_norm_g, w_spatial, b_spatial, out_norm_g, w_out,
           norm2_g, w_gate, w_up, w_down, final_g):
    B, S, D = x.shape
    depth = w_ada.shape[0]
    gm_width = v_norm_g.shape[-1]
    sb_width = (w_in.shape[-1] - 2 * gm_width) // 3
    head_dim = sb_width // SB_HEADS
    assert head_dim == LANE and gm_width // w_spatial.shape[1] == LANE

    xf = x.reshape(B * S, D)
    for l in range(depth):
        mod = _modulation(c, w_ada[l], b_ada[l]).reshape(B, N_MOD, 1, D)
        shift1, scale1, gate1, shift2, scale2, gate2 = [mod[:, m] for m in range(N_MOD)]

        proj = _in_proj(xf, scale1, shift1, norm1_g[l], w_in[l], v_norm_g[l],
                        seq=S, gm_width=gm_width, sb_width=sb_width, head_dim=head_dim)
        o_gm = _gmlp(proj, w_spatial[l], b_spatial[l], out_norm_g[l, :gm_width])
        o_sb, (wo, wg, wu) = _attention(proj, out_norm_g[l, gm_width:], (w_out[l], w_gate[l], w_up[l]),
                                        batch=B, seq=S, heads=SB_HEADS, q_off=2 * gm_width // LANE)
        xf = _out_proj(xf, o_gm, o_sb, wo, gate1, seq=S)
        xf = _ffn(xf, scale2, shift2, gate2, norm2_g[l], wg, wu, w_down[l], final_g,
                  seq=S, final_norm=(l == depth - 1))
    return xf.reshape(B, S, D)
```

```python
import functools

import jax
import jax.numpy as jnp
from jax import lax
from jax.experimental import pallas as pl
from jax.experimental.pallas import tpu as pltpu

EPS = 1e-6
N_MOD = 6
SB_HEADS = 8
LANE = 128
VMEM_LIMIT = 56 * 1024 * 1024
LOG2E = 1.4426950408889634
DEAD_LOG2 = -150.0

F32 = jnp.float32
BF16 = jnp.bfloat16


def _params(*sem):
    return pltpu.CompilerParams(dimension_semantics=sem, vmem_limit_bytes=VMEM_LIMIT)


def _rms(x):
    return x * lax.rsqrt(jnp.mean(x * x, axis=-1, keepdims=True) + EPS)


def _mod_kernel(c_ref, w_ref, b_ref, o_ref):
    c = c_ref[...]
    c_act = (c * jax.nn.sigmoid(c)).astype(BF16)
    o_ref[...] = jnp.dot(c_act, w_ref[...].astype(BF16), preferred_element_type=F32) + b_ref[...]


def _modulation(c, w_ada, b_ada, tn=1024):
    B, D = c.shape
    N = w_ada.shape[1]
    return pl.pallas_call(
        _mod_kernel,
        out_shape=jax.ShapeDtypeStruct((B, N), F32),
        grid=(N // tn,),
        in_specs=[pl.BlockSpec((B, D), lambda j: (0, 0)),
                  pl.BlockSpec((D, tn), lambda j: (0, j)),
                  pl.BlockSpec((1, tn), lambda j: (0, j))],
        out_specs=pl.BlockSpec((B, tn), lambda j: (0, j)),
        compiler_params=_params("parallel"),
        name="modulation",
    )(c, w_ada, b_ada.reshape(1, N))


def _gelu(x):
    return 0.5 * x * (1.0 + lax.erf(x * 0.7071067811865476))


def _in_kernel(x_hbm, sc_ref, sh_ref, g_ref, w_ref, vg_ref, o_ref, xbuf, h_ref, sem,
               *, n_gm, n_seg, q_scale, pc):
    i, j = pl.program_id(0), pl.program_id(1)
    tm = h_ref.shape[0]
    ncb = o_ref.shape[0]

    def x_copy(tile):
        return pltpu.make_async_copy(x_hbm.at[pl.ds(tile * tm, tm), :], xbuf, sem)

    @pl.when(j == 0)
    def _():
        @pl.when(i == 0)
        def _():
            x_copy(0).start()
        x_copy(i).wait()
        gs = g_ref[...] * (1.0 + sc_ref[0])
        sh = sh_ref[0]

        def chunk(r, carry):
            rows = pl.ds(pl.multiple_of(r * pc, pc), pc)
            h_ref[rows, :] = (_rms(xbuf[rows, :]) * gs + sh).astype(BF16)
            return carry
        lax.fori_loop(0, tm // pc, chunk, 0, unroll=8)

    @pl.when(jnp.logical_and(j == 1, i + 1 < pl.num_programs(0)))
    def _():
        x_copy(i + 1).start()

    def epilogue(fn):
        acc = jnp.dot(h_ref[...], w_ref[...].astype(BF16), preferred_element_type=F32)
        for cb in range(ncb):
            o_ref[cb] = fn(acc[:, cb * LANE:(cb + 1) * LANE], cb).astype(BF16)

    @pl.when(j < n_gm)
    def _():
        epilogue(lambda blk, cb: _gelu(blk))

    @pl.when(jnp.logical_and(j >= n_gm, j < 2 * n_gm))
    def _():
        def fn(blk, cb):
            z = _gelu(blk)
            zc = z - jnp.mean(z, axis=-1, keepdims=True)
            return _rms(zc) * vg_ref[0, :, cb * LANE:(cb + 1) * LANE]
        epilogue(fn)

    @pl.when(jnp.logical_and(j >= 2 * n_gm, j < 2 * n_gm + n_seg))
    def _():
        epilogue(lambda blk, cb: blk * q_scale)

    @pl.when(j >= 2 * n_gm + n_seg)
    def _():
        epilogue(lambda blk, cb: blk)


def _in_proj(x2d, scale, shift, norm_g, w_in, v_norm_g, *, seq, gm_width, sb_width, head_dim,
             tm=2048, tn=512, pc=32):
    T, D = x2d.shape
    N = w_in.shape[1]
    assert gm_width % tn == 0 and sb_width % tn == 0 and seq % tm == 0
    assert N // tn >= 2
    n_gm, n_seg = gm_width // tn, sb_width // tn
    per_b = seq // tm
    kern = functools.partial(_in_kernel, n_gm=n_gm, n_seg=n_seg, q_scale=LOG2E * head_dim ** -0.5, pc=pc)
    return pl.pallas_call(
        kern,
        out_shape=jax.ShapeDtypeStruct((N // LANE, T, LANE), BF16),
        grid=(T // tm, N // tn),
        in_specs=[pl.BlockSpec(memory_space=pl.ANY),
                  pl.BlockSpec((1, 1, D), lambda i, j: (i // per_b, 0, 0)),
                  pl.BlockSpec((1, 1, D), lambda i, j: (i // per_b, 0, 0)),
                  pl.BlockSpec((1, D), lambda i, j: (0, 0)),
                  pl.BlockSpec((D, tn), lambda i, j: (0, j)),
                  pl.BlockSpec((1, 1, tn), lambda i, j: (jnp.clip(j - n_gm, 0, n_gm - 1), 0, 0))],
        out_specs=pl.BlockSpec((tn // LANE, tm, LANE), lambda i, j: (j, i, 0)),
        scratch_shapes=[pltpu.VMEM((tm, D), F32), pltpu.VMEM((tm, D), BF16),
                        pltpu.SemaphoreType.DMA(())],
        compiler_params=_params("arbitrary", "arbitrary"),
        name="in_proj",
    )(x2d, scale, shift, norm_g.reshape(1, D), w_in, v_norm_g.reshape(n_gm, 1, tn))


def _gmlp_kernel(u_ref, v_ref, w_ref, b_ref, g_ref, o_ref, *, chunk):
    groups, tr, _ = u_ref.shape
    n_ch = tr // chunk
    causal = (lax.broadcasted_iota(jnp.int32, (chunk, chunk), 0)
              >= lax.broadcasted_iota(jnp.int32, (chunk, chunk), 1))
    for g in range(groups):
        w = jnp.where(causal, w_ref[g], 0.0).astype(BF16)
        v_cat = jnp.concatenate([v_ref[g, c * chunk:(c + 1) * chunk, :] for c in range(n_ch)], axis=1)
        mixed = jnp.dot(w, v_cat, preferred_element_type=F32) + b_ref[g]
        for c in range(n_ch):
            rows = slice(c * chunk, (c + 1) * chunk)
            o = u_ref[g, rows, :].astype(F32) * mixed[:, c * LANE:(c + 1) * LANE]
            o_ref[g, rows, :] = (_rms(o) * g_ref[g]).astype(BF16)


def _gmlp(proj, w_s, b_s, gain, *, tr=2048):
    _, T, _ = proj.shape
    groups, chunk, _ = w_s.shape
    blk = pl.BlockSpec((groups, tr, LANE), lambda r: (0, r, 0))
    return pl.pallas_call(
        functools.partial(_gmlp_kernel, chunk=chunk),
        out_shape=jax.ShapeDtypeStruct((groups, T, LANE), BF16),
        grid=(T // tr,),
        in_specs=[blk,
                  pl.BlockSpec((groups, tr, LANE), lambda r: (1, r, 0)),
                  pl.BlockSpec((groups, chunk, chunk), lambda r: (0, 0, 0)),
                  pl.BlockSpec((groups, chunk, 1), lambda r: (0, 0, 0)),
                  pl.BlockSpec((groups, 1, LANE), lambda r: (0, 0, 0))],
        out_specs=blk,
        compiler_params=_params("parallel"),
        name="gmlp",
    )(proj, proj, w_s, b_s.reshape(groups, chunk, 1), gain.reshape(groups, 1, LANE))


def _attn_kernel(q_ref, k_ref, v_ref, g_ref, *rest, tq, n_cast):
    cast_in, o_ref, cast_out = rest[:n_cast], rest[n_cast], rest[n_cast + 1:2 * n_cast + 1]
    acc_ref, car_ref = rest[2 * n_cast + 1:]
    for src, dst in zip(cast_in, cast_out):
        dst[...] = src[...].astype(BF16)

    hp, S, _ = q_ref.shape
    nq = S // tq
    ri = lax.broadcasted_iota(jnp.int32, (tq, tq), 0)
    ci = lax.broadcasted_iota(jnp.int32, (tq, tq), 1)
    below = ri > ci
    suffix = jnp.where(below, 1.0, 0.0).astype(BF16)

    def logits(q, k):
        z = lax.dot_general(q, k, (((1,), (1,)), ((), ())), preferred_element_type=F32)
        m = jnp.minimum(z, 0.0)
        d = m - z
        l = jnp.log(1.0 + jnp.exp2(m + d)) * LOG2E
        return m - l, d - l

    car_up, acc_up = [None] * hp, [None] * hp
    for j in reversed(range(nq)):
        two = j + 1 < nq
        kj = slice(j * tq, (j + 1) * tq)
        for h in range(hp):
            q = q_ref[h, j * tq:(j + 2) * tq, :] if two else q_ref[h, kj, :]
            lb, lm = logits(q, k_ref[h, kj, :])
            lm_diag = jnp.where(below, lm[:tq], 0.0)
            lm = jnp.concatenate([lm_diag, lm[tq:]], axis=0) if two else lm_diag
            tail = jnp.dot(lm.astype(BF16), suffix, preferred_element_type=F32)
            x = lb + tail
            a = jnp.where(below, jnp.exp2(x[:tq]), 0.0)
            if two:
                a = jnp.concatenate([a, jnp.exp2(x[tq:] + car_up[h])], axis=0)
            pv = jnp.dot(a.astype(BF16), v_ref[h, kj, :], preferred_element_type=F32)
            rowsum = tail[:, :1] + lm[:, :1]
            if two:
                up = slice((j + 1) * tq, (j + 2) * tq)
                acc_ref[h, up, :] = acc_up[h] + pv[tq:]
                car_ref[h, up, :] = car_up[h] + rowsum[tq:]
            car_up[h], acc_up[h] = rowsum[:tq], pv[:tq]
    for h in range(hp):
        acc_ref[h, :tq, :] = acc_up[h]
        car_ref[h, :tq, :] = car_up[h]

    if nq > 2:
        @pl.when(jnp.max(car_ref[:, 2 * tq:, :]) > DEAD_LOG2)
        def _():
            def qblock(t, _):
                h, i = t // (nq - 2), 2 + t % (nq - 2)
                rows = pl.ds(pl.multiple_of(i * tq, tq), tq)
                q = q_ref[h, rows, :]

                def alive(st):
                    return jnp.logical_and(st[0] >= 0, jnp.max(st[1]) > DEAD_LOG2)

                def visit(st):
                    j, car, acc = st
                    ks = pl.ds(pl.multiple_of(j * tq, tq), tq)
                    lb, lm = logits(q, k_ref[h, ks, :])
                    tail = jnp.dot(lm.astype(BF16), suffix, preferred_element_type=F32)
                    a = jnp.exp2(lb + tail + car)
                    acc = acc + jnp.dot(a.astype(BF16), v_ref[h, ks, :], preferred_element_type=F32)
                    return j - 1, car + (tail[:, :1] + lm[:, :1]), acc

                _, car, acc = lax.while_loop(alive, visit,
                                             (i - 2, car_ref[h, rows, :], acc_ref[h, rows, :]))
                acc_ref[h, rows, :] = acc
                car_ref[h, rows, :] = car
                return 0
            lax.fori_loop(0, hp * (nq - 2), qblock, 0)

    for h in range(hp):
        gain = g_ref[h]
        for i in range(nq):
            rows = slice(i * tq, (i + 1) * tq)
            o_ref[h, rows, :] = (_rms(acc_ref[h, rows, :]) * gain).astype(BF16)


def _attention(proj, gain, to_cast, *, batch, seq, heads, q_off, tq=256, hp=1):
    assert heads % hp == 0 and q_off % hp == 0
    pairs = heads // hp
    steps = batch * pairs
    bf16_rows = 16
    assert all(w.shape[0] % (steps * bf16_rows) == 0 for w in to_cast)
    kern = functools.partial(_attn_kernel, tq=tq, n_cast=len(to_cast))

    def spec(off):
        return pl.BlockSpec((hp, seq, LANE), lambda b, p: (off // hp + p, b, 0))

    cast_specs = [pl.BlockSpec((w.shape[0] // steps, w.shape[1]), lambda b, p: (b * pairs + p, 0))
                  for w in to_cast]
    out, *casted = pl.pallas_call(
        kern,
        out_shape=[jax.ShapeDtypeStruct((heads, batch * seq, LANE), BF16)]
        + [jax.ShapeDtypeStruct(w.shape, BF16) for w in to_cast],
        grid=(batch, pairs),
        in_specs=[spec(q_off), spec(q_off + heads), spec(q_off + 2 * heads),
                  pl.BlockSpec((hp, 1, LANE), lambda b, p: (p, 0, 0))] + cast_specs,
        out_specs=[pl.BlockSpec((hp, seq, LANE), lambda b, p: (p, b, 0))] + cast_specs,
        scratch_shapes=[pltpu.VMEM((hp, seq, LANE), F32), pltpu.VMEM((hp, seq, 1), F32)],
        compiler_params=_params("parallel", "parallel"),
        name="sb_attention",
    )(proj, proj, proj, gain.reshape(heads, 1, LANE), *to_cast)
    return out, casted


def _out_kernel(x_ref, a_ref, b_ref, w_ref, gate_ref, o_ref):
    pieces = [a_ref[c] for c in range(a_ref.shape[0])] + [b_ref[c] for c in range(b_ref.shape[0])]
    o = jnp.concatenate(pieces, axis=-1)
    y = jnp.dot(o, w_ref[...], preferred_element_type=F32)
    o_ref[...] = x_ref[...] + gate_ref[0] * y


def _out_proj(x2d, o_gm, o_sb, w_out, gate, *, seq, tm=1024):
    T, D = x2d.shape
    per_b = seq // tm
    return pl.pallas_call(
        _out_kernel,
        out_shape=jax.ShapeDtypeStruct((T, D), F32),
        grid=(T // tm,),
        in_specs=[pl.BlockSpec((tm, D), lambda i: (i, 0)),
                  pl.BlockSpec((o_gm.shape[0], tm, LANE), lambda i: (0, i, 0)),
                  pl.BlockSpec((o_sb.shape[0], tm, LANE), lambda i: (0, i, 0)),
                  pl.BlockSpec(w_out.shape, lambda i: (0, 0), pipeline_mode=pl.Buffered(1)),
                  pl.BlockSpec((1, 1, D), lambda i: (i // per_b, 0, 0))],
        out_specs=pl.BlockSpec((tm, D), lambda i: (i, 0)),
        compiler_params=_params("parallel"),
        name="out_proj",
    )(x2d, o_gm, o_sb, w_out, gate)


def _ffn_kernel(x_hbm, sc_ref, sh_ref, gate_ref, g_ref, wg_ref, wu_ref, wd_ref, fg_ref,
                o_ref, xbuf, h_ref, sem, *, final_norm, rc):
    i, j = pl.program_id(0), pl.program_id(1)
    tm = xbuf.shape[0]

    def x_copy(tile):
        return pltpu.make_async_copy(x_hbm.at[pl.ds(tile * tm, tm), :], xbuf, sem)

    @pl.when(j == 0)
    def _():
        @pl.when(i == 0)
        def _():
            x_copy(0).start()
        x_copy(i).wait()
        gs = g_ref[...] * (1.0 + sc_ref[0])
        sh = sh_ref[0]

        def chunk(r, carry):
            rows = pl.ds(pl.multiple_of(r * rc, rc), rc)
            x = xbuf[rows, :]
            h_ref[rows, :] = (_rms(x) * gs + sh).astype(BF16)
            o_ref[rows, :] = x
            return carry
        lax.fori_loop(0, tm // rc, chunk, 0, unroll=8)

    @pl.when(jnp.logical_and(j == 1, i + 1 < pl.num_programs(0)))
    def _():
        x_copy(i + 1).start()

    h = h_ref[...]
    gt = jnp.dot(h, wg_ref[...], preferred_element_type=F32)
    up = jnp.dot(h, wu_ref[...], preferred_element_type=F32)
    half = 0.5 * gt
    a = ((half * jnp.tanh(half) + half) * up).astype(BF16)
    o_ref[...] += gate_ref[0] * jnp.dot(a, wd_ref[...].astype(BF16), preferred_element_type=F32)

    if final_norm:
        @pl.when(j == pl.num_programs(1) - 1)
        def _():
            o_ref[...] = _rms(o_ref[...]) * fg_ref[...]


def _ffn(x2d, scale, shift, gate, norm_g, w_gate, w_up, w_down, final_g, *, seq, final_norm,
         tm=1024, tf=512, rc=16):
    T, D = x2d.shape
    FF = w_gate.shape[1]
    assert FF // tf >= 2
    per_b = seq // tm
    mod_spec = pl.BlockSpec((1, 1, D), lambda i, j: (i // per_b, 0, 0))
    vec_spec = pl.BlockSpec((1, D), lambda i, j: (0, 0))
    return pl.pallas_call(
        functools.partial(_ffn_kernel, final_norm=final_norm, rc=rc),
        out_shape=jax.ShapeDtypeStruct((T, D), F32),
        grid=(T // tm, FF // tf),
        in_specs=[pl.BlockSpec(memory_space=pl.ANY),
                  mod_spec, mod_spec, mod_spec, vec_spec,
                  pl.BlockSpec((D, tf), lambda i, j: (0, j)),
                  pl.BlockSpec((D, tf), lambda i, j: (0, j)),
                  pl.BlockSpec((tf, D), lambda i, j: (j, 0)),
                  vec_spec],
        out_specs=pl.BlockSpec((tm, D), lambda i, j: (i, 0)),
        scratch_shapes=[pltpu.VMEM((tm, D), F32), pltpu.VMEM((tm, D), BF16),
                        pltpu.SemaphoreType.DMA(())],
        compiler_params=_params("arbitrary", "arbitrary"),
        name="ffn",
    )(x2d, scale, shift, gate, norm_g.reshape(1, D), w_gate, w_up, w_down, final_g.reshape(1, D))


def kernel(x, c, w_ada, b_ada, norm1_g, w_in, v_norm_g, w_spatial, b_spatial, out_norm_g, w_out,
           norm2_g, w_gate, w_up, w_down, final_g):
    B, S, D = x.shape
    depth = w_ada.shape[0]
    gm_width = v_norm_g.shape[-1]
    sb_width = (w_in.shape[-1] - 2 * gm_width) // 3
    head_dim = sb_width // SB_HEADS
    assert head_dim == LANE and gm_width // w_spatial.shape[1] == LANE

    xf = x.reshape(B * S, D)
    for l in range(depth):
        mod = _modulation(c, w_ada[l], b_ada[l]).reshape(B, N_MOD, 1, D)
        shift1, scale1, gate1, shift2, scale2, gate2 = [mod[:, m] for m in range(N_MOD)]

        proj = _in_proj(xf, scale1, shift1, norm1_g[l], w_in[l], v_norm_g[l],
                        seq=S, gm_width=gm_width, sb_width=sb_width, head_dim=head_dim)
        o_gm = _gmlp(proj, w_spatial[l], b_spatial[l], out_norm_g[l, :gm_width])
        o_sb, (wo, wg, wu) = _attention(proj, out_norm_g[l, gm_width:], (w_out[l], w_gate[l], w_up[l]),
                                        batch=B, seq=S, heads=SB_HEADS, q_off=2 * gm_width // LANE)
        xf = _out_proj(xf, o_gm, o_sb, wo, gate1, seq=S)
        xf = _ffn(xf, scale2, shift2, gate2, norm2_g[l], wg, wu, w_down[l], final_g,
                  seq=S, final_norm=(l == depth - 1))
    return xf.reshape(B, S, D)
```

```python
import functools

import jax
import jax.numpy as jnp
from jax import lax
from jax.experimental import pallas as pl
from jax.experimental.pallas import tpu as pltpu

EPS = 1e-6
N_MOD = 6
SB_HEADS = 8
LANE = 128
VMEM_LIMIT = 56 * 1024 * 1024
LOG2E = 1.4426950408889634
DEAD_LOG2 = -150.0

F32 = jnp.float32
BF16 = jnp.bfloat16


def _params(*sem):
    return pltpu.CompilerParams(dimension_semantics=sem, vmem_limit_bytes=VMEM_LIMIT)


def _rms(x):
    return x * lax.rsqrt(jnp.mean(x * x, axis=-1, keepdims=True) + EPS)


def _mod_kernel(c_ref, w_ref, b_ref, o_ref):
    c = c_ref[...]
    c_act = (c * jax.nn.sigmoid(c)).astype(BF16)
    o_ref[...] = jnp.dot(c_act, w_ref[...].astype(BF16), preferred_element_type=F32) + b_ref[...]


def _modulation(c, w_ada, b_ada, tn=1024):
    B, D = c.shape
    N = w_ada.shape[1]
    return pl.pallas_call(
        _mod_kernel,
        out_shape=jax.ShapeDtypeStruct((B, N), F32),
        grid=(N // tn,),
        in_specs=[pl.BlockSpec((B, D), lambda j: (0, 0)),
                  pl.BlockSpec((D, tn), lambda j: (0, j)),
                  pl.BlockSpec((1, tn), lambda j: (0, j))],
        out_specs=pl.BlockSpec((B, tn), lambda j: (0, j)),
        compiler_params=_params("parallel"),
        name="modulation",
    )(c, w_ada, b_ada.reshape(1, N))


def _gelu(x):
    return 0.5 * x * (1.0 + lax.erf(x * 0.7071067811865476))


def _in_kernel(x_hbm, sc_ref, sh_ref, g_ref, w_ref, vg_ref, o_ref, xbuf, h_ref, sem,
               *, n_gm, n_seg, q_scale, pc):
    i, j = pl.program_id(0), pl.program_id(1)
    tm = h_ref.shape[0]
    ncb = o_ref.shape[0]

    def x_copy(tile):
        return pltpu.make_async_copy(x_hbm.at[pl.ds(tile * tm, tm), :], xbuf, sem)

    @pl.when(j == 0)
    def _():
        @pl.when(i == 0)
        def _():
            x_copy(0).start()
        x_copy(i).wait()
        gs = g_ref[...] * (1.0 + sc_ref[0])
        sh = sh_ref[0]

        def chunk(r, carry):
            rows = pl.ds(pl.multiple_of(r * pc, pc), pc)
            h_ref[rows, :] = (_rms(xbuf[rows, :]) * gs + sh).astype(BF16)
            return carry
        lax.fori_loop(0, tm // pc, chunk, 0, unroll=8)

    @pl.when(jnp.logical_and(j == 1, i + 1 < pl.num_programs(0)))
    def _():
        x_copy(i + 1).start()

    def epilogue(fn):
        acc = jnp.dot(h_ref[...], w_ref[...].astype(BF16), preferred_element_type=F32)
        for cb in range(ncb):
            o_ref[cb] = fn(acc[:, cb * LANE:(cb + 1) * LANE], cb).astype(BF16)

    @pl.when(j < n_gm)
    def _():
        epilogue(lambda blk, cb: _gelu(blk))

    @pl.when(jnp.logical_and(j >= n_gm, j < 2 * n_gm))
    def _():
        def fn(blk, cb):
            z = _gelu(blk)
            zc = z - jnp.mean(z, axis=-1, keepdims=True)
            return _rms(zc) * vg_ref[0, :, cb * LANE:(cb + 1) * LANE]
        epilogue(fn)

    @pl.when(jnp.logical_and(j >= 2 * n_gm, j < 2 * n_gm + n_seg))
    def _():
        epilogue(lambda blk, cb: blk * q_scale)

    @pl.when(j >= 2 * n_gm + n_seg)
    def _():
        epilogue(lambda blk, cb: blk)


def _in_proj(x2d, scale, shift, norm_g, w_in, v_norm_g, *, seq, gm_width, sb_width, head_dim,
             tm=2048, tn=512, pc=32):
    T, D = x2d.shape
    N = w_in.shape[1]
    assert gm_width % tn == 0 and sb_width % tn == 0 and seq % tm == 0
    assert N // tn >= 2
    n_gm, n_seg = gm_width // tn, sb_width // tn
    per_b = seq // tm
    kern = functools.partial(_in_kernel, n_gm=n_gm, n_seg=n_seg, q_scale=LOG2E * head_dim ** -0.5, pc=pc)
    return pl.pallas_call(
        kern,
        out_shape=jax.ShapeDtypeStruct((N // LANE, T, LANE), BF16),
        grid=(T // tm, N // tn),
        in_specs=[pl.BlockSpec(memory_space=pl.ANY),
                  pl.BlockSpec((1, 1, D), lambda i, j: (i // per_b, 0, 0)),
                  pl.BlockSpec((1, 1, D), lambda i, j: (i // per_b, 0, 0)),
                  pl.BlockSpec((1, D), lambda i, j: (0, 0)),
                  pl.BlockSpec((D, tn), lambda i, j: (0, j)),
                  pl.BlockSpec((1, 1, tn), lambda i, j: (jnp.clip(j - n_gm, 0, n_gm - 1), 0, 0))],
        out_specs=pl.BlockSpec((tn // LANE, tm, LANE), lambda i, j: (j, i, 0)),
        scratch_shapes=[pltpu.VMEM((tm, D), F32), pltpu.VMEM((tm, D), BF16),
                        pltpu.SemaphoreType.DMA(())],
        compiler_params=_params("arbitrary", "arbitrary"),
        name="in_proj",
    )(x2d, scale, shift, norm_g.reshape(1, D), w_in, v_norm_g.reshape(n_gm, 1, tn))


def _gmlp_kernel(u_ref, v_ref, w_ref, b_ref, g_ref, o_ref, *, chunk):
    groups, tr, _ = u_ref.shape
    n_ch = tr // chunk
    causal = (lax.broadcasted_iota(jnp.int32, (chunk, chunk), 0)
              >= lax.broadcasted_iota(jnp.int32, (chunk, chunk), 1))
    for g in range(groups):
        w = jnp.where(causal, w_ref[g], 0.0).astype(BF16)
        v_cat = jnp.concatenate([v_ref[g, c * chunk:(c + 1) * chunk, :] for c in range(n_ch)], axis=1)
        mixed = jnp.dot(w, v_cat, preferred_element_type=F32) + b_ref[g]
        for c in range(n_ch):
            rows = slice(c * chunk, (c + 1) * chunk)
            o = u_ref[g, rows, :].astype(F32) * mixed[:, c * LANE:(c + 1) * LANE]
            o_ref[g, rows, :] = (_rms(o) * g_ref[g]).astype(BF16)


def _gmlp(proj, w_s, b_s, gain, *, tr=2048):
    _, T, _ = proj.shape
    groups, chunk, _ = w_s.shape
    blk = pl.BlockSpec((groups, tr, LANE), lambda r: (0, r, 0))
    return pl.pallas_call(
        functools.partial(_gmlp_kernel, chunk=chunk),
        out_shape=jax.ShapeDtypeStruct((groups, T, LANE), BF16),
        grid=(T // tr,),
        in_specs=[blk,
                  pl.BlockSpec((groups, tr, LANE), lambda r: (1, r, 0)),
                  pl.BlockSpec((groups, chunk, chunk), lambda r: (0, 0, 0)),
                  pl.BlockSpec((groups, chunk, 1), lambda r: (0, 0, 0)),
                  pl.BlockSpec((groups, 1, LANE), lambda r: (0, 0, 0))],
        out_specs=blk,
        compiler_params=_params("parallel"),
        name="gmlp",
    )(proj, proj, w_s, b_s.reshape(groups, chunk, 1), gain.reshape(groups, 1, LANE))


def _attn_kernel(q_ref, k_ref, v_ref, g_ref, *rest, tq, n_cast):
    cast_in, o_ref, cast_out = rest[:n_cast], rest[n_cast], rest[n_cast + 1:2 * n_cast + 1]
    acc_ref, car_ref = rest[2 * n_cast + 1:]
    for src, dst in zip(cast_in, cast_out):
        dst[...] = src[...].astype(BF16)

    hp, S, _ = q_ref.shape
    nq = S // tq
    ri = lax.broadcasted_iota(jnp.int32, (tq, tq), 0)
    ci = lax.broadcasted_iota(jnp.int32, (tq, tq), 1)
    below = ri > ci
    suffix = jnp.where(below, 1.0, 0.0).astype(BF16)

    def logits(q, k):
        z = lax.dot_general(q, k, (((1,), (1,)), ((), ())), preferred_element_type=F32)
        m = jnp.minimum(z, 0.0)
        d = m - z
        l = jnp.log(1.0 + jnp.exp2(m + d)) * LOG2E
        return m - l, d - l

    car_up, acc_up = [None] * hp, [None] * hp
    for j in reversed(range(nq)):
        two = j + 1 < nq
        kj = slice(j * tq, (j + 1) * tq)
        for h in range(hp):
            q = q_ref[h, j * tq:(j + 2) * tq, :] if two else q_ref[h, kj, :]
            lb, lm = logits(q, k_ref[h, kj, :])
            lm_diag = jnp.where(below, lm[:tq], 0.0)
            lm = jnp.concatenate([lm_diag, lm[tq:]], axis=0) if two else lm_diag
            tail = jnp.dot(lm.astype(BF16), suffix, preferred_element_type=F32)
            x = lb + tail
            a = jnp.where(below, jnp.exp2(x[:tq]), 0.0)
            if two:
                a = jnp.concatenate([a, jnp.exp2(x[tq:] + car_up[h])], axis=0)
            pv = jnp.dot(a.astype(BF16), v_ref[h, kj, :], preferred_element_type=F32)
            rowsum = tail[:, :1] + lm[:, :1]
            if two:
                up = slice((j + 1) * tq, (j + 2) * tq)
                acc_ref[h, up, :] = acc_up[h] + pv[tq:]
                car_ref[h, up, :] = car_up[h] + rowsum[tq:]
            car_up[h], acc_up[h] = rowsum[:tq], pv[:tq]
    for h in range(hp):
        acc_ref[h, :tq, :] = acc_up[h]
        car_ref[h, :tq, :] = car_up[h]

    if nq > 2:
        @pl.when(jnp.max(car_ref[:, 2 * tq:, :]) > DEAD_LOG2)
        def _():
            def qblock(t, _):
                h, i = t // (nq - 2), 2 + t % (nq - 2)
                rows = pl.ds(pl.multiple_of(i * tq, tq), tq)
                q = q_ref[h, rows, :]

                def alive(st):
                    return jnp.logical_and(st[0] >= 0, jnp.max(st[1]) > DEAD_LOG2)

                def visit(st):
                    j, car, acc = st
                    ks = pl.ds(pl.multiple_of(j * tq, tq), tq)
                    lb, lm = logits(q, k_ref[h, ks, :])
                    tail = jnp.dot(lm.astype(BF16), suffix, preferred_element_type=F32)
                    a = jnp.exp2(lb + tail + car)
                    acc = acc + jnp.dot(a.astype(BF16), v_ref[h, ks, :], preferred_element_type=F32)
                    return j - 1, car + (tail[:, :1] + lm[:, :1]), acc

                _, car, acc = lax.while_loop(alive, visit,
                                             (i - 2, car_ref[h, rows, :], acc_ref[h, rows, :]))
                acc_ref[h, rows, :] = acc
                car_ref[h, rows, :] = car
                return 0
            lax.fori_loop(0, hp * (nq - 2), qblock, 0)

    for h in range(hp):
        gain = g_ref[h]
        for i in range(nq):
            rows = slice(i * tq, (i + 1) * tq)
            o_ref[h, rows, :] = (_rms(acc_ref[h, rows, :]) * gain).astype(BF16)


def _attention(proj, gain, to_cast, *, batch, seq, heads, q_off, tq=256, hp=1):
    assert heads % hp == 0 and q_off % hp == 0
    pairs = heads // hp
    steps = batch * pairs
    bf16_rows = 16
    shapes = [w.shape for w in to_cast]

    def slab_view(w):
        rows, cols = w.shape
        k = 1
        while (rows * k) % (steps * bf16_rows) or cols % (k * LANE):
            k *= 2
            assert k <= cols // LANE, w.shape
        return w.reshape(rows * k, cols // k)

    to_cast = [slab_view(w) for w in to_cast]
    kern = functools.partial(_attn_kernel, tq=tq, n_cast=len(to_cast))

    def spec(off):
        return pl.BlockSpec((hp, seq, LANE), lambda b, p: (off // hp + p, b, 0))

    cast_specs = [pl.BlockSpec((w.shape[0] // steps, w.shape[1]), lambda b, p: (b * pairs + p, 0))
                  for w in to_cast]
    out, *casted = pl.pallas_call(
        kern,
        out_shape=[jax.ShapeDtypeStruct((heads, batch * seq, LANE), BF16)]
        + [jax.ShapeDtypeStruct(w.shape, BF16) for w in to_cast],
        grid=(batch, pairs),
        in_specs=[spec(q_off), spec(q_off + heads), spec(q_off + 2 * heads),
                  pl.BlockSpec((hp, 1, LANE), lambda b, p: (p, 0, 0))] + cast_specs,
        out_specs=[pl.BlockSpec((hp, seq, LANE), lambda b, p: (p, b, 0))] + cast_specs,
        scratch_shapes=[pltpu.VMEM((hp, seq, LANE), F32), pltpu.VMEM((hp, seq, 1), F32)],
        compiler_params=_params("parallel", "parallel"),
        name="sb_attention",
    )(proj, proj, proj, gain.reshape(heads, 1, LANE), *to_cast)
    return out, [w.reshape(shape) for w, shape in zip(casted, shapes)]


def _out_kernel(x_ref, a_ref, b_ref, w_ref, gate_ref, o_ref):
    pieces = [a_ref[c] for c in range(a_ref.shape[0])] + [b_ref[c] for c in range(b_ref.shape[0])]
    o = jnp.concatenate(pieces, axis=-1)
    y = jnp.dot(o, w_ref[...], preferred_element_type=F32)
    o_ref[...] = x_ref[...] + gate_ref[0] * y


def _out_proj(x2d, o_gm, o_sb, w_out, gate, *, seq, tm=1024):
    T, D = x2d.shape
    per_b = seq // tm
    return pl.pallas_call(
        _out_kernel,
        out_shape=jax.ShapeDtypeStruct((T, D), F32),
        grid=(T // tm,),
        in_specs=[pl.BlockSpec((tm, D), lambda i: (i, 0)),
                  pl.BlockSpec((o_gm.shape[0], tm, LANE), lambda i: (0, i, 0)),
                  pl.BlockSpec((o_sb.shape[0], tm, LANE), lambda i: (0, i, 0)),
                  pl.BlockSpec(w_out.shape, lambda i: (0, 0), pipeline_mode=pl.Buffered(1)),
                  pl.BlockSpec((1, 1, D), lambda i: (i // per_b, 0, 0))],
        out_specs=pl.BlockSpec((tm, D), lambda i: (i, 0)),
        compiler_params=_params("parallel"),
        name="out_proj",
    )(x2d, o_gm, o_sb, w_out, gate)


def _ffn_kernel(x_hbm, sc_ref, sh_ref, gate_ref, g_ref, wg_ref, wu_ref, wd_ref, fg_ref,
                o_ref, xbuf, h_ref, sem, *, final_norm, rc):
    i, j = pl.program_id(0), pl.program_id(1)
    tm = xbuf.shape[0]

    def x_copy(tile):
        return pltpu.make_async_copy(x_hbm.at[pl.ds(tile * tm, tm), :], xbuf, sem)

    @pl.when(j == 0)
    def _():
        @pl.when(i == 0)
        def _():
            x_copy(0).start()
        x_copy(i).wait()
        gs = g_ref[...] * (1.0 + sc_ref[0])
        sh = sh_ref[0]

        def chunk(r, carry):
            rows = pl.ds(pl.multiple_of(r * rc, rc), rc)
            x = xbuf[rows, :]
            h_ref[rows, :] = (_rms(x) * gs + sh).astype(BF16)
            o_ref[rows, :] = x
            return carry
        lax.fori_loop(0, tm // rc, chunk, 0, unroll=8)

    @pl.when(jnp.logical_and(j == 1, i + 1 < pl.num_programs(0)))
    def _():
        x_copy(i + 1).start()

    h = h_ref[...]
    gt = jnp.dot(h, wg_ref[...], preferred_element_type=F32)
    up = jnp.dot(h, wu_ref[...], preferred_element_type=F32)
    half = 0.5 * gt
    a = ((half * jnp.tanh(half) + half) * up).astype(BF16)
    o_ref[...] += gate_ref[0] * jnp.dot(a, wd_ref[...], preferred_element_type=F32)

    if final_norm:
        @pl.when(j == pl.num_programs(1) - 1)
        def _():
            o_ref[...] = _rms(o_ref[...]) * fg_ref[...]


def _ffn(x2d, scale, shift, gate, norm_g, w_gate, w_up, w_down, final_g, *, seq, final_norm,
         tm=1024, tf=512, rc=16):
    T, D = x2d.shape
    FF = w_gate.shape[1]
    assert FF // tf >= 2
    per_b = seq // tm
    mod_spec = pl.BlockSpec((1, 1, D), lambda i, j: (i // per_b, 0, 0))
    vec_spec = pl.BlockSpec((1, D), lambda i, j: (0, 0))
    return pl.pallas_call(
        functools.partial(_ffn_kernel, final_norm=final_norm, rc=rc),
        out_shape=jax.ShapeDtypeStruct((T, D), F32),
        grid=(T // tm, FF // tf),
        in_specs=[pl.BlockSpec(memory_space=pl.ANY),
                  mod_spec, mod_spec, mod_spec, vec_spec,
                  pl.BlockSpec((D, tf), lambda i, j: (0, j)),
                  pl.BlockSpec((D, tf), lambda i, j: (0, j)),
                  pl.BlockSpec((tf, D), lambda i, j: (j, 0)),
                  vec_spec],
        out_specs=pl.BlockSpec((tm, D), lambda i, j: (i, 0)),
        scratch_shapes=[pltpu.VMEM((tm, D), F32), pltpu.VMEM((tm, D), BF16),
                        pltpu.SemaphoreType.DMA(())],
        compiler_params=_params("arbitrary", "arbitrary"),
        name="ffn",
    )(x2d, scale, shift, gate, norm_g.reshape(1, D), w_gate, w_up, w_down, final_g.reshape(1, D))


def kernel(x, c, w_ada, b_ada, norm1_g, w_in, v_norm_g, w_spatial, b_spatial, out_norm_g, w_out,
           norm2_g, w_gate, w_up, w_down, final_g):
    B, S, D = x.shape
    depth = w_ada.shape[0]
    gm_width = v_norm_g.shape[-1]
    sb_width = (w_in.shape[-1] - 2 * gm_width) // 3
    head_dim = sb_width // SB_HEADS
    assert head_dim == LANE and gm_width // w_spatial.shape[1] == LANE

    xf = x.reshape(B * S, D)
    for l in range(depth):
        mod = _modulation(c, w_ada[l], b_ada[l]).reshape(B, N_MOD, 1, D)
        shift1, scale1, gate1, shift2, scale2, gate2 = [mod[:, m] for m in range(N_MOD)]

        proj = _in_proj(xf, scale1, shift1, norm1_g[l], w_in[l], v_norm_g[l],
                        seq=S, gm_width=gm_width, sb_width=sb_width, head_dim=head_dim)
        o_gm = _gmlp(proj, w_spatial[l], b_spatial[l], out_norm_g[l, :gm_width])
        o_sb, (wo, wg, wu, wd) = _attention(
            proj, out_norm_g[l, gm_width:], (w_out[l], w_gate[l], w_up[l], w_down[l]),
            batch=B, seq=S, heads=SB_HEADS, q_off=2 * gm_width // LANE)
        xf = _out_proj(xf, o_gm, o_sb, wo, gate1, seq=S)
        xf = _ffn(xf, scale2, shift2, gate2, norm2_g[l], wg, wu, wd, final_g,
                  seq=S, final_norm=(l == depth - 1))
    return xf.reshape(B, S, D)
```

```python
import functools

import jax
import jax.numpy as jnp
from jax import lax
from jax.experimental import pallas as pl
from jax.experimental.pallas import tpu as pltpu

EPS = 1e-6
N_MOD = 6
SB_HEADS = 8
LANE = 128
VMEM_LIMIT = 56 * 1024 * 1024
LOG2E = 1.4426950408889634
DEAD_LOG2 = -150.0

F32 = jnp.float32
BF16 = jnp.bfloat16


def _params(*sem):
    return pltpu.CompilerParams(dimension_semantics=sem, vmem_limit_bytes=VMEM_LIMIT)


def _rms(x):
    return x * lax.rsqrt(jnp.mean(x * x, axis=-1, keepdims=True) + EPS)


def _mod_kernel(c_ref, w_ref, b_ref, o_ref):
    c = c_ref[...]
    c_act = (c * jax.nn.sigmoid(c)).astype(BF16)
    o_ref[...] = jnp.dot(c_act, w_ref[...].astype(BF16), preferred_element_type=F32) + b_ref[...]


def _modulation(c, w_ada, b_ada, tn=1024):
    B, D = c.shape
    N = w_ada.shape[1]
    return pl.pallas_call(
        _mod_kernel,
        out_shape=jax.ShapeDtypeStruct((B, N), F32),
        grid=(N // tn,),
        in_specs=[pl.BlockSpec((B, D), lambda j: (0, 0)),
                  pl.BlockSpec((D, tn), lambda j: (0, j)),
                  pl.BlockSpec((1, tn), lambda j: (0, j))],
        out_specs=pl.BlockSpec((B, tn), lambda j: (0, j)),
        compiler_params=_params("parallel"),
        name="modulation",
    )(c, w_ada, b_ada.reshape(1, N))


def _gelu(x):
    return 0.5 * x * (1.0 + lax.erf(x * 0.7071067811865476))


def _in_kernel(x_hbm, sc_ref, sh_ref, g_ref, w_ref, vg_ref, o_ref, xbuf, h_ref, sem,
               *, n_gm, n_seg, q_scale, pc):
    i, j = pl.program_id(0), pl.program_id(1)
    tm = h_ref.shape[0]
    ncb = o_ref.shape[0]

    def x_copy(tile):
        return pltpu.make_async_copy(x_hbm.at[pl.ds(tile * tm, tm), :], xbuf, sem)

    @pl.when(j == 0)
    def _():
        @pl.when(i == 0)
        def _():
            x_copy(0).start()
        x_copy(i).wait()
        gs = g_ref[...] * (1.0 + sc_ref[0])
        sh = sh_ref[0]

        def chunk(r, carry):
            rows = pl.ds(pl.multiple_of(r * pc, pc), pc)
            h_ref[rows, :] = (_rms(xbuf[rows, :]) * gs + sh).astype(BF16)
            return carry
        lax.fori_loop(0, tm // pc, chunk, 0, unroll=8)

    @pl.when(jnp.logical_and(j == 1, i + 1 < pl.num_programs(0)))
    def _():
        x_copy(i + 1).start()

    def epilogue(fn):
        acc = jnp.dot(h_ref[...], w_ref[...].astype(BF16), preferred_element_type=F32)
        for cb in range(ncb):
            o_ref[cb] = fn(acc[:, cb * LANE:(cb + 1) * LANE], cb).astype(BF16)

    @pl.when(j < n_gm)
    def _():
        epilogue(lambda blk, cb: _gelu(blk))

    @pl.when(jnp.logical_and(j >= n_gm, j < 2 * n_gm))
    def _():
        def fn(blk, cb):
            z = _gelu(blk)
            zc = z - jnp.mean(z, axis=-1, keepdims=True)
            return _rms(zc) * vg_ref[0, :, cb * LANE:(cb + 1) * LANE]
        epilogue(fn)

    @pl.when(jnp.logical_and(j >= 2 * n_gm, j < 2 * n_gm + n_seg))
    def _():
        epilogue(lambda blk, cb: blk * q_scale)

    @pl.when(j >= 2 * n_gm + n_seg)
    def _():
        epilogue(lambda blk, cb: blk)


def _in_proj(x2d, scale, shift, norm_g, w_in, v_norm_g, *, seq, gm_width, sb_width, head_dim,
             tm=2048, tn=512, pc=32):
    T, D = x2d.shape
    N = w_in.shape[1]
    assert gm_width % tn == 0 and sb_width % tn == 0 and seq % tm == 0
    assert N // tn >= 2
    n_gm, n_seg = gm_width // tn, sb_width // tn
    per_b = seq // tm
    kern = functools.partial(_in_kernel, n_gm=n_gm, n_seg=n_seg, q_scale=LOG2E * head_dim ** -0.5, pc=pc)
    return pl.pallas_call(
        kern,
        out_shape=jax.ShapeDtypeStruct((N // LANE, T, LANE), BF16),
        grid=(T // tm, N // tn),
        in_specs=[pl.BlockSpec(memory_space=pl.ANY),
                  pl.BlockSpec((1, 1, D), lambda i, j: (i // per_b, 0, 0)),
                  pl.BlockSpec((1, 1, D), lambda i, j: (i // per_b, 0, 0)),
                  pl.BlockSpec((1, D), lambda i, j: (0, 0)),
                  pl.BlockSpec((D, tn), lambda i, j: (0, j)),
                  pl.BlockSpec((1, 1, tn), lambda i, j: (jnp.clip(j - n_gm, 0, n_gm - 1), 0, 0))],
        out_specs=pl.BlockSpec((tn // LANE, tm, LANE), lambda i, j: (j, i, 0)),
        scratch_shapes=[pltpu.VMEM((tm, D), F32), pltpu.VMEM((tm, D), BF16),
                        pltpu.SemaphoreType.DMA(())],
        compiler_params=_params("arbitrary", "arbitrary"),
        name="in_proj",
    )(x2d, scale, shift, norm_g.reshape(1, D), w_in, v_norm_g.reshape(n_gm, 1, tn))


def _gmlp_kernel(u_ref, v_ref, w_ref, b_ref, g_ref, o_ref, *, chunk):
    groups, tr, _ = u_ref.shape
    n_ch = tr // chunk
    causal = (lax.broadcasted_iota(jnp.int32, (chunk, chunk), 0)
              >= lax.broadcasted_iota(jnp.int32, (chunk, chunk), 1))
    for g in range(groups):
        w = jnp.where(causal, w_ref[g], 0.0).astype(BF16)
        v_cat = jnp.concatenate([v_ref[g, c * chunk:(c + 1) * chunk, :] for c in range(n_ch)], axis=1)
        mixed = jnp.dot(w, v_cat, preferred_element_type=F32) + b_ref[g]
        for c in range(n_ch):
            rows = slice(c * chunk, (c + 1) * chunk)
            o = u_ref[g, rows, :].astype(F32) * mixed[:, c * LANE:(c + 1) * LANE]
            o_ref[g, rows, :] = (_rms(o) * g_ref[g]).astype(BF16)


def _gmlp(proj, w_s, b_s, gain, *, tr=2048):
    _, T, _ = proj.shape
    groups, chunk, _ = w_s.shape
    blk = pl.BlockSpec((groups, tr, LANE), lambda r: (0, r, 0))
    return pl.pallas_call(
        functools.partial(_gmlp_kernel, chunk=chunk),
        out_shape=jax.ShapeDtypeStruct((groups, T, LANE), BF16),
        grid=(T // tr,),
        in_specs=[blk,
                  pl.BlockSpec((groups, tr, LANE), lambda r: (1, r, 0)),
                  pl.BlockSpec((groups, chunk, chunk), lambda r: (0, 0, 0)),
                  pl.BlockSpec((groups, chunk, 1), lambda r: (0, 0, 0)),
                  pl.BlockSpec((groups, 1, LANE), lambda r: (0, 0, 0))],
        out_specs=blk,
        compiler_params=_params("parallel"),
        name="gmlp",
    )(proj, proj, w_s, b_s.reshape(groups, chunk, 1), gain.reshape(groups, 1, LANE))


def _attn_kernel(q_ref, k_ref, v_ref, g_ref, *rest, tq, n_cast):
    cast_in, o_ref, cast_out = rest[:n_cast], rest[n_cast], rest[n_cast + 1:2 * n_cast + 1]
    acc_ref, car_ref = rest[2 * n_cast + 1:]
    for src, dst in zip(cast_in, cast_out):
        dst[...] = src[...].astype(BF16)

    hp, S, _ = q_ref.shape
    nq = S // tq
    ri = lax.broadcasted_iota(jnp.int32, (tq, tq), 0)
    ci = lax.broadcasted_iota(jnp.int32, (tq, tq), 1)
    below = ri > ci
    suffix = jnp.where(below, 1.0, 0.0).astype(BF16)

    def logits(q, k):
        z = lax.dot_general(q, k, (((1,), (1,)), ((), ())), preferred_element_type=F32)
        m = jnp.minimum(z, 0.0)
        d = m - z
        l = jnp.log(1.0 + jnp.exp2(m + d)) * LOG2E
        return m - l, d - l

    car_up, acc_up = [None] * hp, [None] * hp
    for j in reversed(range(nq)):
        two = j + 1 < nq
        kj = slice(j * tq, (j + 1) * tq)
        for h in range(hp):
            q = q_ref[h, j * tq:(j + 2) * tq, :] if two else q_ref[h, kj, :]
            lb, lm = logits(q, k_ref[h, kj, :])
            lm_diag = jnp.where(below, lm[:tq], 0.0)
            lm = jnp.concatenate([lm_diag, lm[tq:]], axis=0) if two else lm_diag
            tail = jnp.dot(lm.astype(BF16), suffix, preferred_element_type=F32)
            x = lb + tail
            a = jnp.where(below, jnp.exp2(x[:tq]), 0.0)
            if two:
                a = jnp.concatenate([a, jnp.exp2(x[tq:] + car_up[h])], axis=0)
            pv = jnp.dot(a.astype(BF16), v_ref[h, kj, :], preferred_element_type=F32)
            rowsum = tail[:, :1] + lm[:, :1]
            if two:
                up = slice((j + 1) * tq, (j + 2) * tq)
                acc_ref[h, up, :] = acc_up[h] + pv[tq:]
                car_ref[h, up, :] = car_up[h] + rowsum[tq:]
            car_up[h], acc_up[h] = rowsum[:tq], pv[:tq]
    for h in range(hp):
        acc_ref[h, :tq, :] = acc_up[h]
        car_ref[h, :tq, :] = car_up[h]

    if nq > 2:
        @pl.when(jnp.max(car_ref[:, 2 * tq:, :]) > DEAD_LOG2)
        def _():
            def qblock(t, _):
                h, i = t // (nq - 2), 2 + t % (nq - 2)
                rows = pl.ds(pl.multiple_of(i * tq, tq), tq)
                q = q_ref[h, rows, :]

                def alive(st):
                    return jnp.logical_and(st[0] >= 0, jnp.max(st[1]) > DEAD_LOG2)

                def visit(st):
                    j, car, acc = st
                    ks = pl.ds(pl.multiple_of(j * tq, tq), tq)
                    lb, lm = logits(q, k_ref[h, ks, :])
                    tail = jnp.dot(lm.astype(BF16), suffix, preferred_element_type=F32)
                    a = jnp.exp2(lb + tail + car)
                    acc = acc + jnp.dot(a.astype(BF16), v_ref[h, ks, :], preferred_element_type=F32)
                    return j - 1, car + (tail[:, :1] + lm[:, :1]), acc

                _, car, acc = lax.while_loop(alive, visit,
                                             (i - 2, car_ref[h, rows, :], acc_ref[h, rows, :]))
                acc_ref[h, rows, :] = acc
                car_ref[h, rows, :] = car
                return 0
            lax.fori_loop(0, hp * (nq - 2), qblock, 0)

    for h in range(hp):
        gain = g_ref[h]
        for i in range(nq):
            rows = slice(i * tq, (i + 1) * tq)
            o_ref[h, rows, :] = (_rms(acc_ref[h, rows, :]) * gain).astype(BF16)


def _attention(proj, gain, to_cast, *, batch, seq, heads, q_off, tq=256, hp=1):
    assert heads % hp == 0 and q_off % hp == 0
    pairs = heads // hp
    steps = batch * pairs
    bf16_rows = 16
    assert all(w.shape[0] % (steps * bf16_rows) == 0 for w in to_cast)
    kern = functools.partial(_attn_kernel, tq=tq, n_cast=len(to_cast))

    def spec(off):
        return pl.BlockSpec((hp, seq, LANE), lambda b, p: (off // hp + p, b, 0))

    cast_specs = [pl.BlockSpec((w.shape[0] // steps, w.shape[1]), lambda b, p: (b * pairs + p, 0))
                  for w in to_cast]
    out, *casted = pl.pallas_call(
        kern,
        out_shape=[jax.ShapeDtypeStruct((heads, batch * seq, LANE), BF16)]
        + [jax.ShapeDtypeStruct(w.shape, BF16) for w in to_cast],
        grid=(batch, pairs),
        in_specs=[spec(q_off), spec(q_off + heads), spec(q_off + 2 * heads),
                  pl.BlockSpec((hp, 1, LANE), lambda b, p: (p, 0, 0))] + cast_specs,
        out_specs=[pl.BlockSpec((hp, seq, LANE), lambda b, p: (p, b, 0))] + cast_specs,
        scratch_shapes=[pltpu.VMEM((hp, seq, LANE), F32), pltpu.VMEM((hp, seq, 1), F32)],
        compiler_params=_params("parallel", "parallel"),
        name="sb_attention",
    )(proj, proj, proj, gain.reshape(heads, 1, LANE), *to_cast)
    return out, casted


def _out_kernel(x_ref, a_ref, b_ref, w_ref, gate_ref, o_ref):
    pieces = [a_ref[c] for c in range(a_ref.shape[0])] + [b_ref[c] for c in range(b_ref.shape[0])]
    o = jnp.concatenate(pieces, axis=-1)
    y = jnp.dot(o, w_ref[...], preferred_element_type=F32)
    o_ref[...] = x_ref[...] + gate_ref[0] * y


def _out_proj(x2d, o_gm, o_sb, w_out, gate, *, seq, tm=1024):
    T, D = x2d.shape
    per_b = seq // tm
    return pl.pallas_call(
        _out_kernel,
        out_shape=jax.ShapeDtypeStruct((T, D), F32),
        grid=(T // tm,),
        in_specs=[pl.BlockSpec((tm, D), lambda i: (i, 0)),
                  pl.BlockSpec((o_gm.shape[0], tm, LANE), lambda i: (0, i, 0)),
                  pl.BlockSpec((o_sb.shape[0], tm, LANE), lambda i: (0, i, 0)),
                  pl.BlockSpec(w_out.shape, lambda i: (0, 0), pipeline_mode=pl.Buffered(1)),
                  pl.BlockSpec((1, 1, D), lambda i: (i // per_b, 0, 0))],
        out_specs=pl.BlockSpec((tm, D), lambda i: (i, 0)),
        compiler_params=_params("parallel"),
        name="out_proj",
    )(x2d, o_gm, o_sb, w_out, gate)


def _ffn_kernel(x_hbm, sc_ref, sh_ref, gate_ref, g_ref, wg_hbm, wu_hbm, wd_hbm, fg_ref,
                o_ref, xbuf, h_ref, wg_buf, wu_buf, wd_buf, xsem, wsem, *, final_norm, rc):
    i, ni = pl.program_id(0), pl.num_programs(0)
    tm = xbuf.shape[0]
    tf = wg_buf.shape[2]
    nj = wg_hbm.shape[1] // tf

    def x_copy(tile):
        return pltpu.make_async_copy(x_hbm.at[pl.ds(tile * tm, tm), :], xbuf, xsem)

    def w_copies(j, slot):
        cols = pl.ds(pl.multiple_of(j * tf, tf), tf)
        return (pltpu.make_async_copy(wg_hbm.at[:, cols], wg_buf.at[slot], wsem.at[slot, 0]),
                pltpu.make_async_copy(wu_hbm.at[:, cols], wu_buf.at[slot], wsem.at[slot, 1]),
                pltpu.make_async_copy(wd_hbm.at[cols, :], wd_buf.at[slot], wsem.at[slot, 2]))

    @pl.when(i == 0)
    def _():
        x_copy(0).start()
        for cp in w_copies(0, 0):
            cp.start()

    x_copy(i).wait()
    gs = g_ref[...] * (1.0 + sc_ref[0])
    sh = sh_ref[0]

    def chunk(r, carry):
        rows = pl.ds(pl.multiple_of(r * rc, rc), rc)
        x = xbuf[rows, :]
        h_ref[rows, :] = (_rms(x) * gs + sh).astype(BF16)
        o_ref[rows, :] = x
        return carry
    lax.fori_loop(0, tm // rc, chunk, 0, unroll=8)

    @pl.when(i + 1 < ni)
    def _():
        x_copy(i + 1).start()

    def ff_tile(j, carry):
        t = i * nj + j
        slot = lax.rem(t, 2)
        for cp in w_copies(j, slot):
            cp.wait()

        @pl.when(t + 1 < ni * nj)
        def _():
            for cp in w_copies(lax.rem(j + 1, nj), 1 - slot):
                cp.start()

        h = h_ref[...]
        gt = jnp.dot(h, wg_buf[slot], preferred_element_type=F32)
        up = jnp.dot(h, wu_buf[slot], preferred_element_type=F32)
        half = 0.5 * gt
        a = ((half * jnp.tanh(half) + half) * up).astype(BF16)
        o_ref[...] += gate_ref[0] * jnp.dot(a, wd_buf[slot].astype(BF16), preferred_element_type=F32)
        return carry
    lax.fori_loop(0, nj, ff_tile, 0)

    if final_norm:
        o_ref[...] = _rms(o_ref[...]) * fg_ref[...]


def _ffn(x2d, scale, shift, gate, norm_g, w_gate, w_up, w_down, final_g, *, seq, final_norm,
         tm=1024, tf=512, rc=16):
    T, D = x2d.shape
    FF = w_gate.shape[1]
    assert FF % tf == 0 and seq % tm == 0
    per_b = seq // tm
    mod_spec = pl.BlockSpec((1, 1, D), lambda i: (i // per_b, 0, 0))
    vec_spec = pl.BlockSpec((1, D), lambda i: (0, 0))
    hbm_spec = pl.BlockSpec(memory_space=pl.ANY)
    return pl.pallas_call(
        functools.partial(_ffn_kernel, final_norm=final_norm, rc=rc),
        out_shape=jax.ShapeDtypeStruct((T, D), F32),
        grid=(T // tm,),
        in_specs=[hbm_spec, mod_spec, mod_spec, mod_spec, vec_spec,
                  hbm_spec, hbm_spec, hbm_spec, vec_spec],
        out_specs=pl.BlockSpec((tm, D), lambda i: (i, 0)),
        scratch_shapes=[pltpu.VMEM((tm, D), F32), pltpu.VMEM((tm, D), BF16),
                        pltpu.VMEM((2, D, tf), w_gate.dtype), pltpu.VMEM((2, D, tf), w_up.dtype),
                        pltpu.VMEM((2, tf, D), w_down.dtype),
                        pltpu.SemaphoreType.DMA(()), pltpu.SemaphoreType.DMA((2, 3))],
        compiler_params=_params("arbitrary"),
        name="ffn",
    )(x2d, scale, shift, gate, norm_g.reshape(1, D), w_gate, w_up, w_down, final_g.reshape(1, D))


def kernel(x, c, w_ada, b_ada, norm1_g, w_in, v_norm_g, w_spatial, b_spatial, out_norm_g, w_out,
           norm2_g, w_gate, w_up, w_down, final_g):
    B, S, D = x.shape
    depth = w_ada.shape[0]
    gm_width = v_norm_g.shape[-1]
    sb_width = (w_in.shape[-1] - 2 * gm_width) // 3
    head_dim = sb_width // SB_HEADS
    assert head_dim == LANE and gm_width // w_spatial.shape[1] == LANE

    xf = x.reshape(B * S, D)
    for l in range(depth):
        mod = _modulation(c, w_ada[l], b_ada[l]).reshape(B, N_MOD, 1, D)
        shift1, scale1, gate1, shift2, scale2, gate2 = [mod[:, m] for m in range(N_MOD)]

        proj = _in_proj(xf, scale1, shift1, norm1_g[l], w_in[l], v_norm_g[l],
                        seq=S, gm_width=gm_width, sb_width=sb_width, head_dim=head_dim)
        o_gm = _gmlp(proj, w_spatial[l], b_spatial[l], out_norm_g[l, :gm_width])
        o_sb, (wo, wg, wu) = _attention(proj, out_norm_g[l, gm_width:], (w_out[l], w_gate[l], w_up[l]),
                                        batch=B, seq=S, heads=SB_HEADS, q_off=2 * gm_width // LANE)
        xf = _out_proj(xf, o_gm, o_sb, wo, gate1, seq=S)
        xf = _ffn(xf, scale2, shift2, gate2, norm2_g[l], wg, wu, w_down[l], final_g,
                  seq=S, final_norm=(l == depth - 1))
    return xf.reshape(B, S, D)
```

```python
import functools

import jax
import jax.numpy as jnp
from jax import lax
from jax.experimental import pallas as pl
from jax.experimental.pallas import tpu as pltpu

EPS = 1e-6
N_MOD = 6
SB_HEADS = 8
LANE = 128
VMEM_LIMIT = 56 * 1024 * 1024
LOG2E = 1.4426950408889634
DEAD_LOG2 = -150.0

F32 = jnp.float32
BF16 = jnp.bfloat16


def _params(*sem):
    return pltpu.CompilerParams(dimension_semantics=sem, vmem_limit_bytes=VMEM_LIMIT)


def _rms(x):
    return x * lax.rsqrt(jnp.mean(x * x, axis=-1, keepdims=True) + EPS)


def _mod_kernel(c_ref, w_ref, b_ref, o_ref):
    c = c_ref[...]
    c_act = (c * jax.nn.sigmoid(c)).astype(BF16)
    o_ref[...] = jnp.dot(c_act, w_ref[...].astype(BF16), preferred_element_type=F32) + b_ref[...]


def _modulation(c, w_ada, b_ada, tn=1024):
    B, D = c.shape
    N = w_ada.shape[1]
    return pl.pallas_call(
        _mod_kernel,
        out_shape=jax.ShapeDtypeStruct((B, N), F32),
        grid=(N // tn,),
        in_specs=[pl.BlockSpec((B, D), lambda j: (0, 0)),
                  pl.BlockSpec((D, tn), lambda j: (0, j)),
                  pl.BlockSpec((1, tn), lambda j: (0, j))],
        out_specs=pl.BlockSpec((B, tn), lambda j: (0, j)),
        compiler_params=_params("parallel"),
        name="modulation",
    )(c, w_ada, b_ada.reshape(1, N))


def _gelu(x):
    return 0.5 * x * (1.0 + lax.erf(x * 0.7071067811865476))


def _in_kernel(x_hbm, sc_ref, sh_ref, g_ref, w_ref, vg_ref, o_ref, xbuf, h_ref, sem,
               *, n_gm, n_seg, q_scale, pc):
    i, j = pl.program_id(0), pl.program_id(1)
    tm = h_ref.shape[0]
    ncb = o_ref.shape[0]

    def x_copy(tile):
        return pltpu.make_async_copy(x_hbm.at[pl.ds(tile * tm, tm), :], xbuf, sem)

    @pl.when(j == 0)
    def _():
        @pl.when(i == 0)
        def _():
            x_copy(0).start()
        x_copy(i).wait()
        gs = g_ref[...] * (1.0 + sc_ref[0])
        sh = sh_ref[0]

        def chunk(r, carry):
            rows = pl.ds(pl.multiple_of(r * pc, pc), pc)
            h_ref[rows, :] = (_rms(xbuf[rows, :]) * gs + sh).astype(BF16)
            return carry
        lax.fori_loop(0, tm // pc, chunk, 0, unroll=8)

    @pl.when(jnp.logical_and(j == 1, i + 1 < pl.num_programs(0)))
    def _():
        x_copy(i + 1).start()

    def epilogue(fn):
        acc = jnp.dot(h_ref[...], w_ref[...].astype(BF16), preferred_element_type=F32)
        for cb in range(ncb):
            o_ref[cb] = fn(acc[:, cb * LANE:(cb + 1) * LANE], cb).astype(BF16)

    @pl.when(j < n_gm)
    def _():
        epilogue(lambda blk, cb: _gelu(blk))

    @pl.when(jnp.logical_and(j >= n_gm, j < 2 * n_gm))
    def _():
        def fn(blk, cb):
            z = _gelu(blk)
            zc = z - jnp.mean(z, axis=-1, keepdims=True)
            return _rms(zc) * vg_ref[0, :, cb * LANE:(cb + 1) * LANE]
        epilogue(fn)

    @pl.when(jnp.logical_and(j >= 2 * n_gm, j < 2 * n_gm + n_seg))
    def _():
        epilogue(lambda blk, cb: blk * q_scale)

    @pl.when(j >= 2 * n_gm + n_seg)
    def _():
        epilogue(lambda blk, cb: blk)


def _in_proj(x2d, scale, shift, norm_g, w_in, v_norm_g, *, seq, gm_width, sb_width, head_dim,
             tm=2048, tn=512, pc=32):
    T, D = x2d.shape
    N = w_in.shape[1]
    assert gm_width % tn == 0 and sb_width % tn == 0 and seq % tm == 0
    assert N // tn >= 2
    n_gm, n_seg = gm_width // tn, sb_width // tn
    per_b = seq // tm
    kern = functools.partial(_in_kernel, n_gm=n_gm, n_seg=n_seg, q_scale=LOG2E * head_dim ** -0.5, pc=pc)
    return pl.pallas_call(
        kern,
        out_shape=jax.ShapeDtypeStruct((N // LANE, T, LANE), BF16),
        grid=(T // tm, N // tn),
        in_specs=[pl.BlockSpec(memory_space=pl.ANY),
                  pl.BlockSpec((1, 1, D), lambda i, j: (i // per_b, 0, 0)),
                  pl.BlockSpec((1, 1, D), lambda i, j: (i // per_b, 0, 0)),
                  pl.BlockSpec((1, D), lambda i, j: (0, 0)),
                  pl.BlockSpec((D, tn), lambda i, j: (0, j)),
                  pl.BlockSpec((1, 1, tn), lambda i, j: (jnp.clip(j - n_gm, 0, n_gm - 1), 0, 0))],
        out_specs=pl.BlockSpec((tn // LANE, tm, LANE), lambda i, j: (j, i, 0)),
        scratch_shapes=[pltpu.VMEM((tm, D), F32), pltpu.VMEM((tm, D), BF16),
                        pltpu.SemaphoreType.DMA(())],
        compiler_params=_params("arbitrary", "arbitrary"),
        name="in_proj",
    )(x2d, scale, shift, norm_g.reshape(1, D), w_in, v_norm_g.reshape(n_gm, 1, tn))


def _gmlp_kernel(u_ref, v_ref, w_ref, b_ref, g_ref, o_ref, *, chunk):
    groups, tr, _ = u_ref.shape
    n_ch = tr // chunk
    causal = (lax.broadcasted_iota(jnp.int32, (chunk, chunk), 0)
              >= lax.broadcasted_iota(jnp.int32, (chunk, chunk), 1))
    for g in range(groups):
        w = jnp.where(causal, w_ref[g], 0.0).astype(BF16)
        v_cat = jnp.concatenate([v_ref[g, c * chunk:(c + 1) * chunk, :] for c in range(n_ch)], axis=1)
        mixed = jnp.dot(w, v_cat, preferred_element_type=F32) + b_ref[g]
        for c in range(n_ch):
            rows = slice(c * chunk, (c + 1) * chunk)
            o = u_ref[g, rows, :].astype(F32) * mixed[:, c * LANE:(c + 1) * LANE]
            o_ref[g, rows, :] = (_rms(o) * g_ref[g]).astype(BF16)


def _gmlp(proj, w_s, b_s, gain, *, tr=2048):
    _, T, _ = proj.shape
    groups, chunk, _ = w_s.shape
    blk = pl.BlockSpec((groups, tr, LANE), lambda r: (0, r, 0))
    return pl.pallas_call(
        functools.partial(_gmlp_kernel, chunk=chunk),
        out_shape=jax.ShapeDtypeStruct((groups, T, LANE), BF16),
        grid=(T // tr,),
        in_specs=[blk,
                  pl.BlockSpec((groups, tr, LANE), lambda r: (1, r, 0)),
                  pl.BlockSpec((groups, chunk, chunk), lambda r: (0, 0, 0)),
                  pl.BlockSpec((groups, chunk, 1), lambda r: (0, 0, 0)),
                  pl.BlockSpec((groups, 1, LANE), lambda r: (0, 0, 0))],
        out_specs=blk,
        compiler_params=_params("parallel"),
        name="gmlp",
    )(proj, proj, w_s, b_s.reshape(groups, chunk, 1), gain.reshape(groups, 1, LANE))


def _attn_kernel(q_ref, k_ref, v_ref, g_ref, *rest, tq, n_cast):
    cast_in, o_ref, cast_out = rest[:n_cast], rest[n_cast], rest[n_cast + 1:2 * n_cast + 1]
    acc_ref, car_ref = rest[2 * n_cast + 1:]
    for src, dst in zip(cast_in, cast_out):
        dst[...] = src[...].astype(BF16)

    hp, S, _ = q_ref.shape
    nq = S // tq
    ri = lax.broadcasted_iota(jnp.int32, (tq, tq), 0)
    ci = lax.broadcasted_iota(jnp.int32, (tq, tq), 1)
    below = ri > ci
    suffix = jnp.where(below, 1.0, 0.0).astype(BF16)

    def logits(q, k):
        z = lax.dot_general(q, k, (((1,), (1,)), ((), ())), preferred_element_type=F32)
        m = jnp.minimum(z, 0.0)
        d = m - z
        l = jnp.log(1.0 + jnp.exp2(m + d)) * LOG2E
        return m - l, d - l

    car_up, acc_up = [None] * hp, [None] * hp
    for j in reversed(range(nq)):
        two = j + 1 < nq
        kj = slice(j * tq, (j + 1) * tq)
        for h in range(hp):
            q = q_ref[h, j * tq:(j + 2) * tq, :] if two else q_ref[h, kj, :]
            lb, lm = logits(q, k_ref[h, kj, :])
            lm_diag = jnp.where(below, lm[:tq], 0.0)
            lm = jnp.concatenate([lm_diag, lm[tq:]], axis=0) if two else lm_diag
            tail = jnp.dot(lm.astype(BF16), suffix, preferred_element_type=F32)
            x = lb + tail
            a = jnp.where(below, jnp.exp2(x[:tq]), 0.0)
            if two:
                a = jnp.concatenate([a, jnp.exp2(x[tq:] + car_up[h])], axis=0)
            pv = jnp.dot(a.astype(BF16), v_ref[h, kj, :], preferred_element_type=F32)
            rowsum = tail[:, :1] + lm[:, :1]
            if two:
                up = slice((j + 1) * tq, (j + 2) * tq)
                acc_ref[h, up, :] = acc_up[h] + pv[tq:]
                car_ref[h, up, :] = car_up[h] + rowsum[tq:]
            car_up[h], acc_up[h] = rowsum[:tq], pv[:tq]
    for h in range(hp):
        acc_ref[h, :tq, :] = acc_up[h]
        car_ref[h, :tq, :] = car_up[h]

    if nq > 2:
        @pl.when(jnp.max(car_ref[:, 2 * tq:, :]) > DEAD_LOG2)
        def _():
            def qblock(t, _):
                h, i = t // (nq - 2), 2 + t % (nq - 2)
                rows = pl.ds(pl.multiple_of(i * tq, tq), tq)
                q = q_ref[h, rows, :]

                def alive(st):
                    return jnp.logical_and(st[0] >= 0, jnp.max(st[1]) > DEAD_LOG2)

                def visit(st):
                    j, car, acc = st
                    ks = pl.ds(pl.multiple_of(j * tq, tq), tq)
                    lb, lm = logits(q, k_ref[h, ks, :])
                    tail = jnp.dot(lm.astype(BF16), suffix, preferred_element_type=F32)
                    a = jnp.exp2(lb + tail + car)
                    acc = acc + jnp.dot(a.astype(BF16), v_ref[h, ks, :], preferred_element_type=F32)
                    return j - 1, car + (tail[:, :1] + lm[:, :1]), acc

                _, car, acc = lax.while_loop(alive, visit,
                                             (i - 2, car_ref[h, rows, :], acc_ref[h, rows, :]))
                acc_ref[h, rows, :] = acc
                car_ref[h, rows, :] = car
                return 0
            lax.fori_loop(0, hp * (nq - 2), qblock, 0)

    for h in range(hp):
        gain = g_ref[h]
        for i in range(nq):
            rows = slice(i * tq, (i + 1) * tq)
            o_ref[h, rows, :] = (_rms(acc_ref[h, rows, :]) * gain).astype(BF16)


def _attention(proj, gain, to_cast, *, batch, seq, heads, q_off, tq=256, hp=1):
    assert heads % hp == 0 and q_off % hp == 0
    pairs = heads // hp
    steps = batch * pairs
    bf16_rows = 16
    assert all(w.shape[0] % (steps * bf16_rows) == 0 for w in to_cast)
    kern = functools.partial(_attn_kernel, tq=tq, n_cast=len(to_cast))

    def spec(off):
        return pl.BlockSpec((hp, seq, LANE), lambda b, p: (off // hp + p, b, 0))

    cast_specs = [pl.BlockSpec((w.shape[0] // steps, w.shape[1]), lambda b, p: (b * pairs + p, 0))
                  for w in to_cast]
    out, *casted = pl.pallas_call(
        kern,
        out_shape=[jax.ShapeDtypeStruct((heads, batch * seq, LANE), BF16)]
        + [jax.ShapeDtypeStruct(w.shape, BF16) for w in to_cast],
        grid=(batch, pairs),
        in_specs=[spec(q_off), spec(q_off + heads), spec(q_off + 2 * heads),
                  pl.BlockSpec((hp, 1, LANE), lambda b, p: (p, 0, 0))] + cast_specs,
        out_specs=[pl.BlockSpec((hp, seq, LANE), lambda b, p: (p, b, 0))] + cast_specs,
        scratch_shapes=[pltpu.VMEM((hp, seq, LANE), F32), pltpu.VMEM((hp, seq, 1), F32)],
        compiler_params=_params("parallel", "parallel"),
        name="sb_attention",
    )(proj, proj, proj, gain.reshape(heads, 1, LANE), *to_cast)
    return out, casted


def _out_kernel(x_ref, a_ref, b_ref, w_ref, gate_ref, o_ref):
    pieces = [a_ref[c] for c in range(a_ref.shape[0])] + [b_ref[c] for c in range(b_ref.shape[0])]
    o = jnp.concatenate(pieces, axis=-1)
    y = jnp.dot(o, w_ref[...], preferred_element_type=F32)
    o_ref[...] = x_ref[...] + gate_ref[0] * y


def _out_proj(x2d, o_gm, o_sb, w_out, gate, *, seq, tm=1024):
    T, D = x2d.shape
    per_b = seq // tm
    return pl.pallas_call(
        _out_kernel,
        out_shape=jax.ShapeDtypeStruct((T, D), F32),
        grid=(T // tm,),
        in_specs=[pl.BlockSpec((tm, D), lambda i: (i, 0)),
                  pl.BlockSpec((o_gm.shape[0], tm, LANE), lambda i: (0, i, 0)),
                  pl.BlockSpec((o_sb.shape[0], tm, LANE), lambda i: (0, i, 0)),
                  pl.BlockSpec(w_out.shape, lambda i: (0, 0), pipeline_mode=pl.Buffered(1)),
                  pl.BlockSpec((1, 1, D), lambda i: (i // per_b, 0, 0))],
        out_specs=pl.BlockSpec((tm, D), lambda i: (i, 0)),
        compiler_params=_params("parallel"),
        name="out_proj",
    )(x2d, o_gm, o_sb, w_out, gate)


def _ffn_kernel(x_hbm, sc_ref, sh_ref, gate_ref, g_ref, wg_hbm, wu_hbm, wd_hbm, fg_ref,
                o_ref, xbuf, h_ref, wg_buf, wu_buf, wd_buf, xsem, wsem, *, final_norm, rc):
    i, ni = pl.program_id(0), pl.num_programs(0)
    tm = xbuf.shape[0]
    tf = wg_buf.shape[2]
    nj = wg_hbm.shape[1] // tf

    def x_copy(tile):
        return pltpu.make_async_copy(x_hbm.at[pl.ds(tile * tm, tm), :], xbuf, xsem)

    def w_copies(j, slot):
        cols = pl.ds(pl.multiple_of(j * tf, tf), tf)
        return (pltpu.make_async_copy(wg_hbm.at[:, cols], wg_buf.at[slot], wsem.at[slot, 0]),
                pltpu.make_async_copy(wu_hbm.at[:, cols], wu_buf.at[slot], wsem.at[slot, 1]),
                pltpu.make_async_copy(wd_hbm.at[cols, :], wd_buf.at[slot], wsem.at[slot, 2]))

    @pl.when(i == 0)
    def _():
        x_copy(0).start()
        for cp in w_copies(0, 0):
            cp.start()

    x_copy(i).wait()
    gs = g_ref[...] * (1.0 + sc_ref[0])
    sh = sh_ref[0]

    def chunk(r, carry):
        rows = pl.ds(pl.multiple_of(r * rc, rc), rc)
        h_ref[rows, :] = (_rms(xbuf[rows, :]) * gs + sh).astype(BF16)
        return carry
    lax.fori_loop(0, tm // rc, chunk, 0, unroll=8)

    def ff_tile(j, acc_ref):
        t = i * nj + j
        slot = lax.rem(t, 2)
        for cp in w_copies(j, slot):
            cp.wait()

        @pl.when(t + 1 < ni * nj)
        def _():
            for cp in w_copies(lax.rem(j + 1, nj), 1 - slot):
                cp.start()

        h = h_ref[...]
        gt = jnp.dot(h, wg_buf[slot], preferred_element_type=F32)
        up = jnp.dot(h, wu_buf[slot], preferred_element_type=F32)
        half = 0.5 * gt
        a = ((half * jnp.tanh(half) + half) * up).astype(BF16)
        part = jnp.dot(a, wd_buf[slot].astype(BF16), preferred_element_type=F32)
        o_ref[...] = acc_ref[...] + gate_ref[0] * part

    ff_tile(0, xbuf)

    @pl.when(i + 1 < ni)
    def _():
        x_copy(i + 1).start()

    def rest(j, carry):
        ff_tile(j, o_ref)
        return carry
    lax.fori_loop(1, nj, rest, 0)

    if final_norm:
        o_ref[...] = _rms(o_ref[...]) * fg_ref[...]


def _ffn(x2d, scale, shift, gate, norm_g, w_gate, w_up, w_down, final_g, *, seq, final_norm,
         tm=1024, tf=512, rc=32):
    T, D = x2d.shape
    FF = w_gate.shape[1]
    assert FF % tf == 0 and seq % tm == 0
    per_b = seq // tm
    mod_spec = pl.BlockSpec((1, 1, D), lambda i: (i // per_b, 0, 0))
    vec_spec = pl.BlockSpec((1, D), lambda i: (0, 0))
    hbm_spec = pl.BlockSpec(memory_space=pl.ANY)
    return pl.pallas_call(
        functools.partial(_ffn_kernel, final_norm=final_norm, rc=rc),
        out_shape=jax.ShapeDtypeStruct((T, D), F32),
        grid=(T // tm,),
        in_specs=[hbm_spec, mod_spec, mod_spec, mod_spec, vec_spec,
                  hbm_spec, hbm_spec, hbm_spec, vec_spec],
        out_specs=pl.BlockSpec((tm, D), lambda i: (i, 0)),
        scratch_shapes=[pltpu.VMEM((tm, D), F32), pltpu.VMEM((tm, D), BF16),
                        pltpu.VMEM((2, D, tf), w_gate.dtype), pltpu.VMEM((2, D, tf), w_up.dtype),
                        pltpu.VMEM((2, tf, D), w_down.dtype),
                        pltpu.SemaphoreType.DMA(()), pltpu.SemaphoreType.DMA((2, 3))],
        compiler_params=_params("arbitrary"),
        name="ffn",
    )(x2d, scale, shift, gate, norm_g.reshape(1, D), w_gate, w_up, w_down, final_g.reshape(1, D))


def kernel(x, c, w_ada, b_ada, norm1_g, w_in, v_norm_g, w_spatial, b_spatial, out_norm_g, w_out,
           norm2_g, w_gate, w_up, w_down, final_g):
    B, S, D = x.shape
    depth = w_ada.shape[0]
    gm_width = v_norm_g.shape[-1]
    sb_width = (w_in.shape[-1] - 2 * gm_width) // 3
    head_dim = sb_width // SB_HEADS
    assert head_dim == LANE and gm_width // w_spatial.shape[1] == LANE

    xf = x.reshape(B * S, D)
    for l in range(depth):
        mod = _modulation(c, w_ada[l], b_ada[l]).reshape(B, N_MOD, 1, D)
        shift1, scale1, gate1, shift2, scale2, gate2 = [mod[:, m] for m in range(N_MOD)]

        proj = _in_proj(xf, scale1, shift1, norm1_g[l], w_in[l], v_norm_g[l],
                        seq=S, gm_width=gm_width, sb_width=sb_width, head_dim=head_dim)
        o_gm = _gmlp(proj, w_spatial[l], b_spatial[l], out_norm_g[l, :gm_width])
        o_sb, (wo, wg, wu) = _attention(proj, out_norm_g[l, gm_width:], (w_out[l], w_gate[l], w_up[l]),
                                        batch=B, seq=S, heads=SB_HEADS, q_off=2 * gm_width // LANE)
        xf = _out_proj(xf, o_gm, o_sb, wo, gate1, seq=S)
        xf = _ffn(xf, scale2, shift2, gate2, norm2_g[l], wg, wu, w_down[l], final_g,
                  seq=S, final_norm=(l == depth - 1))
    return xf.reshape(B, S, D)
```

```python
import functools

import jax
import jax.numpy as jnp
from jax import lax
from jax.experimental import pallas as pl
from jax.experimental.pallas import tpu as pltpu

EPS = 1e-6
N_MOD = 6
SB_HEADS = 8
LANE = 128
VMEM_LIMIT = 56 * 1024 * 1024
LOG2E = 1.4426950408889634
DEAD_LOG2 = -150.0

F32 = jnp.float32
BF16 = jnp.bfloat16


def _params(*sem):
    return pltpu.CompilerParams(dimension_semantics=sem, vmem_limit_bytes=VMEM_LIMIT)


def _rms(x):
    return x * lax.rsqrt(jnp.mean(x * x, axis=-1, keepdims=True) + EPS)


def _mod_kernel(c_ref, w_ref, b_ref, o_ref):
    c = c_ref[...]
    c_act = (c * jax.nn.sigmoid(c)).astype(BF16)
    o_ref[...] = jnp.dot(c_act, w_ref[...].astype(BF16), preferred_element_type=F32) + b_ref[...]


def _modulation(c, w_ada, b_ada, n_cols, tn=1024):
    B, D = c.shape
    N = n_cols
    return pl.pallas_call(
        _mod_kernel,
        out_shape=jax.ShapeDtypeStruct((B, N), F32),
        grid=(N // tn,),
        in_specs=[pl.BlockSpec((B, D), lambda j: (0, 0)),
                  pl.BlockSpec((D, tn), lambda j: (0, j)),
                  pl.BlockSpec((1, tn), lambda j: (0, j))],
        out_specs=pl.BlockSpec((B, tn), lambda j: (0, j)),
        compiler_params=_params("parallel"),
        name="modulation",
    )(c, w_ada, b_ada.reshape(1, -1))


def _gelu(x):
    return 0.5 * x * (1.0 + lax.erf(x * 0.7071067811865476))


def _in_kernel(x_hbm, sc_ref, sh_ref, g_ref, w_ref, vg_ref, o_ref, xbuf, h_ref, sem,
               *, n_gm, n_seg, q_scale, pc):
    i, j = pl.program_id(0), pl.program_id(1)
    tm = h_ref.shape[0]
    ncb = o_ref.shape[0]

    def x_copy(tile):
        return pltpu.make_async_copy(x_hbm.at[pl.ds(tile * tm, tm), :], xbuf, sem)

    @pl.when(j == 0)
    def _():
        @pl.when(i == 0)
        def _():
            x_copy(0).start()
        x_copy(i).wait()
        gs = g_ref[...] * (1.0 + sc_ref[0])
        sh = sh_ref[0]

        def chunk(r, carry):
            rows = pl.ds(pl.multiple_of(r * pc, pc), pc)
            h_ref[rows, :] = (_rms(xbuf[rows, :]) * gs + sh).astype(BF16)
            return carry
        lax.fori_loop(0, tm // pc, chunk, 0, unroll=8)

    @pl.when(jnp.logical_and(j == 1, i + 1 < pl.num_programs(0)))
    def _():
        x_copy(i + 1).start()

    def epilogue(fn):
        acc = jnp.dot(h_ref[...], w_ref[...].astype(BF16), preferred_element_type=F32)
        for cb in range(ncb):
            o_ref[cb] = fn(acc[:, cb * LANE:(cb + 1) * LANE], cb).astype(BF16)

    @pl.when(j < n_gm)
    def _():
        epilogue(lambda blk, cb: _gelu(blk))

    @pl.when(jnp.logical_and(j >= n_gm, j < 2 * n_gm))
    def _():
        def fn(blk, cb):
            z = _gelu(blk)
            zc = z - jnp.mean(z, axis=-1, keepdims=True)
            return _rms(zc) * vg_ref[0, :, cb * LANE:(cb + 1) * LANE]
        epilogue(fn)

    @pl.when(jnp.logical_and(j >= 2 * n_gm, j < 2 * n_gm + n_seg))
    def _():
        epilogue(lambda blk, cb: blk * q_scale)

    @pl.when(j >= 2 * n_gm + n_seg)
    def _():
        epilogue(lambda blk, cb: blk)


def _in_proj(x2d, scale, shift, norm_g, w_in, v_norm_g, *, seq, gm_width, sb_width, head_dim,
             tm=2048, tn=512, pc=32):
    T, D = x2d.shape
    N = w_in.shape[1]
    assert gm_width % tn == 0 and sb_width % tn == 0 and seq % tm == 0
    assert N // tn >= 2
    n_gm, n_seg = gm_width // tn, sb_width // tn
    per_b = seq // tm
    kern = functools.partial(_in_kernel, n_gm=n_gm, n_seg=n_seg, q_scale=LOG2E * head_dim ** -0.5, pc=pc)
    return pl.pallas_call(
        kern,
        out_shape=jax.ShapeDtypeStruct((N // LANE, T, LANE), BF16),
        grid=(T // tm, N // tn),
        in_specs=[pl.BlockSpec(memory_space=pl.ANY),
                  pl.BlockSpec((1, 1, D), lambda i, j: (i // per_b, 0, 0)),
                  pl.BlockSpec((1, 1, D), lambda i, j: (i // per_b, 0, 0)),
                  pl.BlockSpec((1, D), lambda i, j: (0, 0)),
                  pl.BlockSpec((D, tn), lambda i, j: (0, j)),
                  pl.BlockSpec((1, 1, tn), lambda i, j: (jnp.clip(j - n_gm, 0, n_gm - 1), 0, 0))],
        out_specs=pl.BlockSpec((tn // LANE, tm, LANE), lambda i, j: (j, i, 0)),
        scratch_shapes=[pltpu.VMEM((tm, D), F32), pltpu.VMEM((tm, D), BF16),
                        pltpu.SemaphoreType.DMA(())],
        compiler_params=_params("arbitrary", "arbitrary"),
        name="in_proj",
    )(x2d, scale, shift, norm_g.reshape(1, D), w_in, v_norm_g.reshape(n_gm, 1, tn))


def _gmlp_kernel(u_ref, v_ref, w_ref, b_ref, g_ref, o_ref, *, chunk):
    groups, tr, _ = u_ref.shape
    n_ch = tr // chunk
    causal = (lax.broadcasted_iota(jnp.int32, (chunk, chunk), 0)
              >= lax.broadcasted_iota(jnp.int32, (chunk, chunk), 1))
    for g in range(groups):
        w = jnp.where(causal, w_ref[g], 0.0).astype(BF16)
        v_cat = jnp.concatenate([v_ref[g, c * chunk:(c + 1) * chunk, :] for c in range(n_ch)], axis=1)
        mixed = jnp.dot(w, v_cat, preferred_element_type=F32) + b_ref[g]
        for c in range(n_ch):
            rows = slice(c * chunk, (c + 1) * chunk)
            o = u_ref[g, rows, :].astype(F32) * mixed[:, c * LANE:(c + 1) * LANE]
            o_ref[g, rows, :] = (_rms(o) * g_ref[g]).astype(BF16)


def _gmlp(proj, w_s, b_s, gain, *, tr=2048):
    _, T, _ = proj.shape
    groups, chunk, _ = w_s.shape
    blk = pl.BlockSpec((groups, tr, LANE), lambda r: (0, r, 0))
    return pl.pallas_call(
        functools.partial(_gmlp_kernel, chunk=chunk),
        out_shape=jax.ShapeDtypeStruct((groups, T, LANE), BF16),
        grid=(T // tr,),
        in_specs=[blk,
                  pl.BlockSpec((groups, tr, LANE), lambda r: (1, r, 0)),
                  pl.BlockSpec((groups, chunk, chunk), lambda r: (0, 0, 0)),
                  pl.BlockSpec((groups, chunk, 1), lambda r: (0, 0, 0)),
                  pl.BlockSpec((groups, 1, LANE), lambda r: (0, 0, 0))],
        out_specs=blk,
        compiler_params=_params("parallel"),
        name="gmlp",
    )(proj, proj, w_s, b_s.reshape(groups, chunk, 1), gain.reshape(groups, 1, LANE))


def _attn_kernel(q_ref, k_ref, v_ref, g_ref, *rest, tq, n_cast):
    cast_in, (c_ref, wm_ref, bm_ref, o_ref) = rest[:n_cast], rest[n_cast:n_cast + 4]
    cast_out, mod_ref = rest[n_cast + 4:2 * n_cast + 4], rest[2 * n_cast + 4]
    acc_ref, car_ref = rest[2 * n_cast + 5:]
    for src, dst in zip(cast_in, cast_out):
        dst[...] = src[...].astype(BF16)
    _mod_kernel(c_ref, wm_ref, bm_ref, mod_ref)

    hp, S, _ = q_ref.shape
    nq = S // tq
    ri = lax.broadcasted_iota(jnp.int32, (tq, tq), 0)
    ci = lax.broadcasted_iota(jnp.int32, (tq, tq), 1)
    below = ri > ci
    suffix = jnp.where(below, 1.0, 0.0).astype(BF16)

    def logits(q, k):
        z = lax.dot_general(q, k, (((1,), (1,)), ((), ())), preferred_element_type=F32)
        m = jnp.minimum(z, 0.0)
        d = m - z
        l = jnp.log(1.0 + jnp.exp2(m + d)) * LOG2E
        return m - l, d - l

    car_up, acc_up = [None] * hp, [None] * hp
    for j in reversed(range(nq)):
        two = j + 1 < nq
        kj = slice(j * tq, (j + 1) * tq)
        for h in range(hp):
            q = q_ref[h, j * tq:(j + 2) * tq, :] if two else q_ref[h, kj, :]
            lb, lm = logits(q, k_ref[h, kj, :])
            lm_diag = jnp.where(below, lm[:tq], 0.0)
            lm = jnp.concatenate([lm_diag, lm[tq:]], axis=0) if two else lm_diag
            tail = jnp.dot(lm.astype(BF16), suffix, preferred_element_type=F32)
            x = lb + tail
            a = jnp.where(below, jnp.exp2(x[:tq]), 0.0)
            if two:
                a = jnp.concatenate([a, jnp.exp2(x[tq:] + car_up[h])], axis=0)
            pv = jnp.dot(a.astype(BF16), v_ref[h, kj, :], preferred_element_type=F32)
            rowsum = tail[:, :1] + lm[:, :1]
            if two:
                up = slice((j + 1) * tq, (j + 2) * tq)
                acc_ref[h, up, :] = acc_up[h] + pv[tq:]
                car_ref[h, up, :] = car_up[h] + rowsum[tq:]
            car_up[h], acc_up[h] = rowsum[:tq], pv[:tq]
    for h in range(hp):
        acc_ref[h, :tq, :] = acc_up[h]
        car_ref[h, :tq, :] = car_up[h]

    if nq > 2:
        @pl.when(jnp.max(car_ref[:, 2 * tq:, :]) > DEAD_LOG2)
        def _():
            def qblock(t, _):
                h, i = t // (nq - 2), 2 + t % (nq - 2)
                rows = pl.ds(pl.multiple_of(i * tq, tq), tq)
                q = q_ref[h, rows, :]

                def alive(st):
                    return jnp.logical_and(st[0] >= 0, jnp.max(st[1]) > DEAD_LOG2)

                def visit(st):
                    j, car, acc = st
                    ks = pl.ds(pl.multiple_of(j * tq, tq), tq)
                    lb, lm = logits(q, k_ref[h, ks, :])
                    tail = jnp.dot(lm.astype(BF16), suffix, preferred_element_type=F32)
                    a = jnp.exp2(lb + tail + car)
                    acc = acc + jnp.dot(a.astype(BF16), v_ref[h, ks, :], preferred_element_type=F32)
                    return j - 1, car + (tail[:, :1] + lm[:, :1]), acc

                _, car, acc = lax.while_loop(alive, visit,
                                             (i - 2, car_ref[h, rows, :], acc_ref[h, rows, :]))
                acc_ref[h, rows, :] = acc
                car_ref[h, rows, :] = car
                return 0
            lax.fori_loop(0, hp * (nq - 2), qblock, 0)

    for h in range(hp):
        gain = g_ref[h]
        for i in range(nq):
            rows = slice(i * tq, (i + 1) * tq)
            o_ref[h, rows, :] = (_rms(acc_ref[h, rows, :]) * gain).astype(BF16)


def _attention(proj, gain, to_cast, mod_job, *, batch, seq, heads, q_off, tq=256, hp=1):
    assert heads % hp == 0 and q_off % hp == 0
    pairs = heads // hp
    steps = batch * pairs
    bf16_rows = 16
    assert all(w.shape[0] % (steps * bf16_rows) == 0 for w in to_cast)
    c, w_ada, b_ada, col0 = mod_job
    n_mod = w_ada.shape[1] - col0
    tc = n_mod // steps
    assert n_mod % steps == 0 and tc % LANE == 0 and col0 % tc == 0
    kern = functools.partial(_attn_kernel, tq=tq, n_cast=len(to_cast))

    def spec(off):
        return pl.BlockSpec((hp, seq, LANE), lambda b, p: (off // hp + p, b, 0))

    cast_specs = [pl.BlockSpec((w.shape[0] // steps, w.shape[1]), lambda b, p: (b * pairs + p, 0))
                  for w in to_cast]
    mod_in = [pl.BlockSpec(c.shape, lambda b, p: (0, 0)),
              pl.BlockSpec((w_ada.shape[0], tc), lambda b, p: (0, col0 // tc + b * pairs + p)),
              pl.BlockSpec((1, tc), lambda b, p: (0, col0 // tc + b * pairs + p))]
    out, *rest = pl.pallas_call(
        kern,
        out_shape=[jax.ShapeDtypeStruct((heads, batch * seq, LANE), BF16)]
        + [jax.ShapeDtypeStruct(w.shape, BF16) for w in to_cast]
        + [jax.ShapeDtypeStruct((c.shape[0], n_mod), F32)],
        grid=(batch, pairs),
        in_specs=[spec(q_off), spec(q_off + heads), spec(q_off + 2 * heads),
                  pl.BlockSpec((hp, 1, LANE), lambda b, p: (p, 0, 0))] + cast_specs + mod_in,
        out_specs=[pl.BlockSpec((hp, seq, LANE), lambda b, p: (p, b, 0))] + cast_specs
        + [pl.BlockSpec((c.shape[0], tc), lambda b, p: (0, b * pairs + p))],
        scratch_shapes=[pltpu.VMEM((hp, seq, LANE), F32), pltpu.VMEM((hp, seq, 1), F32)],
        compiler_params=_params("parallel", "parallel"),
        name="sb_attention",
    )(proj, proj, proj, gain.reshape(heads, 1, LANE), *to_cast, c, w_ada, b_ada.reshape(1, -1))
    return out, rest[:-1], rest[-1]


def _out_kernel(x_ref, a_ref, b_ref, w_ref, gate_ref, o_ref):
    pieces = [a_ref[c] for c in range(a_ref.shape[0])] + [b_ref[c] for c in range(b_ref.shape[0])]
    o = jnp.concatenate(pieces, axis=-1)
    y = jnp.dot(o, w_ref[...], preferred_element_type=F32)
    o_ref[...] = x_ref[...] + gate_ref[0] * y


def _out_proj(x2d, o_gm, o_sb, w_out, gate, *, seq, tm=1024):
    T, D = x2d.shape
    per_b = seq // tm
    return pl.pallas_call(
        _out_kernel,
        out_shape=jax.ShapeDtypeStruct((T, D), F32),
        grid=(T // tm,),
        in_specs=[pl.BlockSpec((tm, D), lambda i: (i, 0)),
                  pl.BlockSpec((o_gm.shape[0], tm, LANE), lambda i: (0, i, 0)),
                  pl.BlockSpec((o_sb.shape[0], tm, LANE), lambda i: (0, i, 0)),
                  pl.BlockSpec(w_out.shape, lambda i: (0, 0), pipeline_mode=pl.Buffered(1)),
                  pl.BlockSpec((1, 1, D), lambda i: (i // per_b, 0, 0))],
        out_specs=pl.BlockSpec((tm, D), lambda i: (i, 0)),
        compiler_params=_params("parallel"),
        name="out_proj",
    )(x2d, o_gm, o_sb, w_out, gate)


def _ffn_kernel(x_hbm, sc_ref, sh_ref, gate_ref, g_ref, wg_hbm, wu_hbm, wd_hbm, fg_ref,
                o_ref, xbuf, h_ref, wg_buf, wu_buf, wd_buf, xsem, wsem, *, final_norm, rc):
    i, ni = pl.program_id(0), pl.num_programs(0)
    tm = xbuf.shape[0]
    tf = wg_buf.shape[2]
    nj = wg_hbm.shape[1] // tf

    def x_copy(tile):
        return pltpu.make_async_copy(x_hbm.at[pl.ds(tile * tm, tm), :], xbuf, xsem)

    def w_copies(j, slot):
        cols = pl.ds(pl.multiple_of(j * tf, tf), tf)
        return (pltpu.make_async_copy(wg_hbm.at[:, cols], wg_buf.at[slot], wsem.at[slot, 0]),
                pltpu.make_async_copy(wu_hbm.at[:, cols], wu_buf.at[slot], wsem.at[slot, 1]),
                pltpu.make_async_copy(wd_hbm.at[cols, :], wd_buf.at[slot], wsem.at[slot, 2]))

    @pl.when(i == 0)
    def _():
        x_copy(0).start()
        for cp in w_copies(0, 0):
            cp.start()

    x_copy(i).wait()
    gs = g_ref[...] * (1.0 + sc_ref[0])
    sh = sh_ref[0]

    def chunk(r, carry):
        rows = pl.ds(pl.multiple_of(r * rc, rc), rc)
        h_ref[rows, :] = (_rms(xbuf[rows, :]) * gs + sh).astype(BF16)
        return carry
    lax.fori_loop(0, tm // rc, chunk, 0, unroll=8)

    def ff_tile(j, acc_ref):
        t = i * nj + j
        slot = lax.rem(t, 2)
        for cp in w_copies(j, slot):
            cp.wait()

        @pl.when(t + 1 < ni * nj)
        def _():
            for cp in w_copies(lax.rem(j + 1, nj), 1 - slot):
                cp.start()

        h = h_ref[...]
        gt = jnp.dot(h, wg_buf[slot], preferred_element_type=F32)
        up = jnp.dot(h, wu_buf[slot], preferred_element_type=F32)
        half = 0.5 * gt
        a = ((half * jnp.tanh(half) + half) * up).astype(BF16)
        part = jnp.dot(a, wd_buf[slot].astype(BF16), preferred_element_type=F32)
        o_ref[...] = acc_ref[...] + gate_ref[0] * part

    ff_tile(0, xbuf)

    @pl.when(i + 1 < ni)
    def _():
        x_copy(i + 1).start()

    def rest(j, carry):
        ff_tile(j, o_ref)
        return carry
    lax.fori_loop(1, nj, rest, 0)

    if final_norm:
        o_ref[...] = _rms(o_ref[...]) * fg_ref[...]


def _ffn(x2d, scale, shift, gate, norm_g, w_gate, w_up, w_down, final_g, *, seq, final_norm,
         tm=1024, tf=512, rc=32):
    T, D = x2d.shape
    FF = w_gate.shape[1]
    assert FF % tf == 0 and seq % tm == 0
    per_b = seq // tm
    mod_spec = pl.BlockSpec((1, 1, D), lambda i: (i // per_b, 0, 0))
    vec_spec = pl.BlockSpec((1, D), lambda i: (0, 0))
    hbm_spec = pl.BlockSpec(memory_space=pl.ANY)
    return pl.pallas_call(
        functools.partial(_ffn_kernel, final_norm=final_norm, rc=rc),
        out_shape=jax.ShapeDtypeStruct((T, D), F32),
        grid=(T // tm,),
        in_specs=[hbm_spec, mod_spec, mod_spec, mod_spec, vec_spec,
                  hbm_spec, hbm_spec, hbm_spec, vec_spec],
        out_specs=pl.BlockSpec((tm, D), lambda i: (i, 0)),
        scratch_shapes=[pltpu.VMEM((tm, D), F32), pltpu.VMEM((tm, D), BF16),
                        pltpu.VMEM((2, D, tf), w_gate.dtype), pltpu.VMEM((2, D, tf), w_up.dtype),
                        pltpu.VMEM((2, tf, D), w_down.dtype),
                        pltpu.SemaphoreType.DMA(()), pltpu.SemaphoreType.DMA((2, 3))],
        compiler_params=_params("arbitrary"),
        name="ffn",
    )(x2d, scale, shift, gate, norm_g.reshape(1, D), w_gate, w_up, w_down, final_g.reshape(1, D))


def kernel(x, c, w_ada, b_ada, norm1_g, w_in, v_norm_g, w_spatial, b_spatial, out_norm_g, w_out,
           norm2_g, w_gate, w_up, w_down, final_g):
    B, S, D = x.shape
    depth = w_ada.shape[0]
    gm_width = v_norm_g.shape[-1]
    sb_width = (w_in.shape[-1] - 2 * gm_width) // 3
    head_dim = sb_width // SB_HEADS
    assert head_dim == LANE and gm_width // w_spatial.shape[1] == LANE

    xf = x.reshape(B * S, D)
    for l in range(depth):
        early = 2
        mod1 = _modulation(c, w_ada[l], b_ada[l], early * D).reshape(B, early, 1, D)
        shift1, scale1 = mod1[:, 0], mod1[:, 1]

        proj = _in_proj(xf, scale1, shift1, norm1_g[l], w_in[l], v_norm_g[l],
                        seq=S, gm_width=gm_width, sb_width=sb_width, head_dim=head_dim)
        o_gm = _gmlp(proj, w_spatial[l], b_spatial[l], out_norm_g[l, :gm_width])
        o_sb, (wo, wg, wu), mod2 = _attention(
            proj, out_norm_g[l, gm_width:], (w_out[l], w_gate[l], w_up[l]),
            (c, w_ada[l], b_ada[l], early * D),
            batch=B, seq=S, heads=SB_HEADS, q_off=2 * gm_width // LANE)
        gate1, shift2, scale2, gate2 = [mod2.reshape(B, N_MOD - early, 1, D)[:, m]
                                        for m in range(N_MOD - early)]
        xf = _out_proj(xf, o_gm, o_sb, wo, gate1, seq=S)
        xf = _ffn(xf, scale2, shift2, gate2, norm2_g[l], wg, wu, w_down[l], final_g,
                  seq=S, final_norm=(l == depth - 1))
    return xf.reshape(B, S, D)
```

```python
import functools

import jax
import jax.numpy as jnp
from jax import lax
from jax.experimental import pallas as pl
from jax.experimental.pallas import tpu as pltpu

EPS = 1e-6
N_MOD = 6
SB_HEADS = 8
LANE = 128
VMEM_LIMIT = 56 * 1024 * 1024
LOG2E = 1.4426950408889634
DEAD_LOG2 = -150.0

F32 = jnp.float32
BF16 = jnp.bfloat16


def _params(*sem):
    return pltpu.CompilerParams(dimension_semantics=sem, vmem_limit_bytes=VMEM_LIMIT)


def _rms(x):
    return x * lax.rsqrt(jnp.mean(x * x, axis=-1, keepdims=True) + EPS)


def _mod_kernel(c_ref, w_ref, b_ref, o_ref):
    c = c_ref[...]
    c_act = (c * jax.nn.sigmoid(c)).astype(BF16)
    o_ref[...] = jnp.dot(c_act, w_ref[...].astype(BF16), preferred_element_type=F32) + b_ref[...]


def _modulation(c, w_ada, b_ada, n_cols, tn=1024):
    B, D = c.shape
    N = n_cols
    return pl.pallas_call(
        _mod_kernel,
        out_shape=jax.ShapeDtypeStruct((B, N), F32),
        grid=(N // tn,),
        in_specs=[pl.BlockSpec((B, D), lambda j: (0, 0)),
                  pl.BlockSpec((D, tn), lambda j: (0, j)),
                  pl.BlockSpec((1, tn), lambda j: (0, j))],
        out_specs=pl.BlockSpec((B, tn), lambda j: (0, j)),
        compiler_params=_params("parallel"),
        name="modulation",
    )(c, w_ada, b_ada.reshape(1, -1))


def _gelu(x):
    return 0.5 * x * (1.0 + lax.erf(x * 0.7071067811865476))


def _in_kernel(x_hbm, sc_ref, sh_ref, g_ref, w_ref, vg_ref, o_ref, xbuf, h_ref, sem,
               *, n_gm, n_seg, q_scale, pc):
    i, j = pl.program_id(0), pl.program_id(1)
    tm = h_ref.shape[0]
    ncb = o_ref.shape[0]

    def x_copy(tile):
        return pltpu.make_async_copy(x_hbm.at[pl.ds(tile * tm, tm), :], xbuf, sem)

    @pl.when(j == 0)
    def _():
        @pl.when(i == 0)
        def _():
            x_copy(0).start()
        x_copy(i).wait()
        gs = g_ref[...] * (1.0 + sc_ref[0])
        sh = sh_ref[0]

        def chunk(r, carry):
            rows = pl.ds(pl.multiple_of(r * pc, pc), pc)
            h_ref[rows, :] = (_rms(xbuf[rows, :]) * gs + sh).astype(BF16)
            return carry
        lax.fori_loop(0, tm // pc, chunk, 0, unroll=8)

    @pl.when(jnp.logical_and(j == 1, i + 1 < pl.num_programs(0)))
    def _():
        x_copy(i + 1).start()

    def epilogue(fn):
        acc = jnp.dot(h_ref[...], w_ref[...].astype(BF16), preferred_element_type=F32)
        for cb in range(ncb):
            o_ref[cb] = fn(acc[:, cb * LANE:(cb + 1) * LANE], cb).astype(BF16)

    @pl.when(j < n_gm)
    def _():
        epilogue(lambda blk, cb: _gelu(blk))

    @pl.when(jnp.logical_and(j >= n_gm, j < 2 * n_gm))
    def _():
        def fn(blk, cb):
            z = _gelu(blk)
            zc = z - jnp.mean(z, axis=-1, keepdims=True)
            return _rms(zc) * vg_ref[0, :, cb * LANE:(cb + 1) * LANE]
        epilogue(fn)

    @pl.when(jnp.logical_and(j >= 2 * n_gm, j < 2 * n_gm + n_seg))
    def _():
        epilogue(lambda blk, cb: blk * q_scale)

    @pl.when(j >= 2 * n_gm + n_seg)
    def _():
        epilogue(lambda blk, cb: blk)


def _in_proj(x2d, scale, shift, norm_g, w_in, v_norm_g, *, seq, gm_width, sb_width, head_dim,
             tm=2048, tn=512, pc=32):
    T, D = x2d.shape
    N = w_in.shape[1]
    assert gm_width % tn == 0 and sb_width % tn == 0 and seq % tm == 0
    assert N // tn >= 2
    n_gm, n_seg = gm_width // tn, sb_width // tn
    per_b = seq // tm
    kern = functools.partial(_in_kernel, n_gm=n_gm, n_seg=n_seg, q_scale=LOG2E * head_dim ** -0.5, pc=pc)
    return pl.pallas_call(
        kern,
        out_shape=jax.ShapeDtypeStruct((N // LANE, T, LANE), BF16),
        grid=(T // tm, N // tn),
        in_specs=[pl.BlockSpec(memory_space=pl.ANY),
                  pl.BlockSpec((1, 1, D), lambda i, j: (i // per_b, 0, 0)),
                  pl.BlockSpec((1, 1, D), lambda i, j: (i // per_b, 0, 0)),
                  pl.BlockSpec((1, D), lambda i, j: (0, 0)),
                  pl.BlockSpec((D, tn), lambda i, j: (0, j)),
                  pl.BlockSpec((1, 1, tn), lambda i, j: (jnp.clip(j - n_gm, 0, n_gm - 1), 0, 0))],
        out_specs=pl.BlockSpec((tn // LANE, tm, LANE), lambda i, j: (j, i, 0)),
        scratch_shapes=[pltpu.VMEM((tm, D), F32), pltpu.VMEM((tm, D), BF16),
                        pltpu.SemaphoreType.DMA(())],
        compiler_params=_params("arbitrary", "arbitrary"),
        name="in_proj",
    )(x2d, scale, shift, norm_g.reshape(1, D), w_in, v_norm_g.reshape(n_gm, 1, tn))


def _gmlp_kernel(u_ref, v_ref, w_ref, b_ref, g_ref, o_ref, *, chunk):
    groups, tr, _ = u_ref.shape
    n_ch = tr // chunk
    causal = (lax.broadcasted_iota(jnp.int32, (chunk, chunk), 0)
              >= lax.broadcasted_iota(jnp.int32, (chunk, chunk), 1))
    for g in range(groups):
        w = jnp.where(causal, w_ref[g], 0.0).astype(BF16)
        v_cat = jnp.concatenate([v_ref[g, c * chunk:(c + 1) * chunk, :] for c in range(n_ch)], axis=1)
        mixed = jnp.dot(w, v_cat, preferred_element_type=F32) + b_ref[g]
        for c in range(n_ch):
            rows = slice(c * chunk, (c + 1) * chunk)
            o = u_ref[g, rows, :].astype(F32) * mixed[:, c * LANE:(c + 1) * LANE]
            o_ref[g, rows, :] = (_rms(o) * g_ref[g]).astype(BF16)


def _gmlp(proj, w_s, b_s, gain, *, tr=2048):
    _, T, _ = proj.shape
    groups, chunk, _ = w_s.shape
    blk = pl.BlockSpec((groups, tr, LANE), lambda r: (0, r, 0))
    return pl.pallas_call(
        functools.partial(_gmlp_kernel, chunk=chunk),
        out_shape=jax.ShapeDtypeStruct((groups, T, LANE), BF16),
        grid=(T // tr,),
        in_specs=[blk,
                  pl.BlockSpec((groups, tr, LANE), lambda r: (1, r, 0)),
                  pl.BlockSpec((groups, chunk, chunk), lambda r: (0, 0, 0)),
                  pl.BlockSpec((groups, chunk, 1), lambda r: (0, 0, 0)),
                  pl.BlockSpec((groups, 1, LANE), lambda r: (0, 0, 0))],
        out_specs=blk,
        compiler_params=_params("parallel"),
        name="gmlp",
    )(proj, proj, w_s, b_s.reshape(groups, chunk, 1), gain.reshape(groups, 1, LANE))


def _attn_kernel(q_ref, k_ref, v_ref, g_ref, *rest, tq, n_cast):
    cast_in, (c_ref, wm_ref, bm_ref, o_ref) = rest[:n_cast], rest[n_cast:n_cast + 4]
    cast_out, mod_ref = rest[n_cast + 4:2 * n_cast + 4], rest[2 * n_cast + 4]
    acc_ref, car_ref = rest[2 * n_cast + 5:]
    for src, dst in zip(cast_in, cast_out):
        dst[...] = src[...].astype(BF16)
    _mod_kernel(c_ref, wm_ref, bm_ref, mod_ref)

    hp, S, _ = q_ref.shape
    nq = S // tq
    ri = lax.broadcasted_iota(jnp.int32, (tq, tq), 0)
    ci = lax.broadcasted_iota(jnp.int32, (tq, tq), 1)
    below = ri > ci
    suffix = jnp.where(below, 1.0, 0.0).astype(BF16)

    def logits(q, k):
        z = lax.dot_general(q, k, (((1,), (1,)), ((), ())), preferred_element_type=F32)
        m = jnp.minimum(z, 0.0)
        d = m - z
        l = jnp.log(1.0 + jnp.exp2(m + d)) * LOG2E
        return m - l, d - l

    car_up, acc_up = [None] * hp, [None] * hp
    for j in reversed(range(nq)):
        two = j + 1 < nq
        kj = slice(j * tq, (j + 1) * tq)
        for h in range(hp):
            q = q_ref[h, j * tq:(j + 2) * tq, :] if two else q_ref[h, kj, :]
            lb, lm = logits(q, k_ref[h, kj, :])
            lm_diag = jnp.where(below, lm[:tq], 0.0)
            lm = jnp.concatenate([lm_diag, lm[tq:]], axis=0) if two else lm_diag
            tail = jnp.dot(lm.astype(BF16), suffix, preferred_element_type=F32)
            x = lb + tail
            a = jnp.where(below, jnp.exp2(x[:tq]), 0.0)
            if two:
                a = jnp.concatenate([a, jnp.exp2(x[tq:] + car_up[h])], axis=0)
            pv = jnp.dot(a.astype(BF16), v_ref[h, kj, :], preferred_element_type=F32)
            rowsum = tail[:, :1] + lm[:, :1]
            if two:
                up = slice((j + 1) * tq, (j + 2) * tq)
                done = acc_up[h] + pv[tq:]
                acc_ref[h, up, :] = done
                car_ref[h, up, :] = car_up[h] + rowsum[tq:]
                o_ref[h, up, :] = (_rms(done) * g_ref[h]).astype(BF16)
            car_up[h], acc_up[h] = rowsum[:tq], pv[:tq]
    for h in range(hp):
        o_ref[h, :tq, :] = (_rms(acc_up[h]) * g_ref[h]).astype(BF16)

    if nq > 2:
        @pl.when(jnp.max(car_ref[:, 2 * tq:, :]) > DEAD_LOG2)
        def _():
            def qblock(t, _):
                h, i = t // (nq - 2), 2 + t % (nq - 2)
                rows = pl.ds(pl.multiple_of(i * tq, tq), tq)
                q = q_ref[h, rows, :]

                def alive(st):
                    return jnp.logical_and(st[0] >= 0, jnp.max(st[1]) > DEAD_LOG2)

                def visit(st):
                    j, car, acc = st
                    ks = pl.ds(pl.multiple_of(j * tq, tq), tq)
                    lb, lm = logits(q, k_ref[h, ks, :])
                    tail = jnp.dot(lm.astype(BF16), suffix, preferred_element_type=F32)
                    a = jnp.exp2(lb + tail + car)
                    acc = acc + jnp.dot(a.astype(BF16), v_ref[h, ks, :], preferred_element_type=F32)
                    return j - 1, car + (tail[:, :1] + lm[:, :1]), acc

                _, car, acc = lax.while_loop(alive, visit,
                                             (i - 2, car_ref[h, rows, :], acc_ref[h, rows, :]))
                o_ref[h, rows, :] = (_rms(acc) * g_ref[h]).astype(BF16)
                return 0
            lax.fori_loop(0, hp * (nq - 2), qblock, 0)


def _attention(proj, gain, to_cast, mod_job, *, batch, seq, heads, q_off, tq=256, hp=1):
    assert heads % hp == 0 and q_off % hp == 0
    pairs = heads // hp
    steps = batch * pairs
    bf16_rows = 16
    assert all(w.shape[0] % (steps * bf16_rows) == 0 for w in to_cast)
    c, w_ada, b_ada, col0 = mod_job
    n_mod = w_ada.shape[1] - col0
    tc = n_mod // steps
    assert n_mod % steps == 0 and tc % LANE == 0 and col0 % tc == 0
    kern = functools.partial(_attn_kernel, tq=tq, n_cast=len(to_cast))

    def spec(off):
        return pl.BlockSpec((hp, seq, LANE), lambda b, p: (off // hp + p, b, 0))

    cast_specs = [pl.BlockSpec((w.shape[0] // steps, w.shape[1]), lambda b, p: (b * pairs + p, 0))
                  for w in to_cast]
    mod_in = [pl.BlockSpec(c.shape, lambda b, p: (0, 0)),
              pl.BlockSpec((w_ada.shape[0], tc), lambda b, p: (0, col0 // tc + b * pairs + p)),
              pl.BlockSpec((1, tc), lambda b, p: (0, col0 // tc + b * pairs + p))]
    out, *rest = pl.pallas_call(
        kern,
        out_shape=[jax.ShapeDtypeStruct((heads, batch * seq, LANE), BF16)]
        + [jax.ShapeDtypeStruct(w.shape, BF16) for w in to_cast]
        + [jax.ShapeDtypeStruct((c.shape[0], n_mod), F32)],
        grid=(batch, pairs),
        in_specs=[spec(q_off), spec(q_off + heads), spec(q_off + 2 * heads),
                  pl.BlockSpec((hp, 1, LANE), lambda b, p: (p, 0, 0))] + cast_specs + mod_in,
        out_specs=[pl.BlockSpec((hp, seq, LANE), lambda b, p: (p, b, 0))] + cast_specs
        + [pl.BlockSpec((c.shape[0], tc), lambda b, p: (0, b * pairs + p))],
        scratch_shapes=[pltpu.VMEM((hp, seq, LANE), F32), pltpu.VMEM((hp, seq, 1), F32)],
        compiler_params=_params("parallel", "parallel"),
        name="sb_attention",
    )(proj, proj, proj, gain.reshape(heads, 1, LANE), *to_cast, c, w_ada, b_ada.reshape(1, -1))
    return out, rest[:-1], rest[-1]


def _out_kernel(x_ref, a_ref, b_ref, w_ref, gate_ref, o_ref):
    pieces = [a_ref[c] for c in range(a_ref.shape[0])] + [b_ref[c] for c in range(b_ref.shape[0])]
    o = jnp.concatenate(pieces, axis=-1)
    y = jnp.dot(o, w_ref[...], preferred_element_type=F32)
    o_ref[...] = x_ref[...] + gate_ref[0] * y


def _out_proj(x2d, o_gm, o_sb, w_out, gate, *, seq, tm=1024):
    T, D = x2d.shape
    per_b = seq // tm
    return pl.pallas_call(
        _out_kernel,
        out_shape=jax.ShapeDtypeStruct((T, D), F32),
        grid=(T // tm,),
        in_specs=[pl.BlockSpec((tm, D), lambda i: (i, 0)),
                  pl.BlockSpec((o_gm.shape[0], tm, LANE), lambda i: (0, i, 0)),
                  pl.BlockSpec((o_sb.shape[0], tm, LANE), lambda i: (0, i, 0)),
                  pl.BlockSpec(w_out.shape, lambda i: (0, 0), pipeline_mode=pl.Buffered(1)),
                  pl.BlockSpec((1, 1, D), lambda i: (i // per_b, 0, 0))],
        out_specs=pl.BlockSpec((tm, D), lambda i: (i, 0)),
        compiler_params=_params("parallel"),
        name="out_proj",
    )(x2d, o_gm, o_sb, w_out, gate)


def _ffn_kernel(x_hbm, sc_ref, sh_ref, gate_ref, g_ref, wg_hbm, wu_hbm, wd_hbm, fg_ref,
                o_ref, xbuf, h_ref, wg_buf, wu_buf, wd_buf, xsem, wsem, *, final_norm, rc):
    i, ni = pl.program_id(0), pl.num_programs(0)
    tm = xbuf.shape[0]
    tf = wg_buf.shape[2]
    nj = wg_hbm.shape[1] // tf

    def x_copy(tile):
        return pltpu.make_async_copy(x_hbm.at[pl.ds(tile * tm, tm), :], xbuf, xsem)

    def w_copies(j, slot):
        cols = pl.ds(pl.multiple_of(j * tf, tf), tf)
        return (pltpu.make_async_copy(wg_hbm.at[:, cols], wg_buf.at[slot], wsem.at[slot, 0]),
                pltpu.make_async_copy(wu_hbm.at[:, cols], wu_buf.at[slot], wsem.at[slot, 1]),
                pltpu.make_async_copy(wd_hbm.at[cols, :], wd_buf.at[slot], wsem.at[slot, 2]))

    @pl.when(i == 0)
    def _():
        x_copy(0).start()
        for cp in w_copies(0, 0):
            cp.start()

    x_copy(i).wait()
    gs = g_ref[...] * (1.0 + sc_ref[0])
    sh = sh_ref[0]

    def chunk(r, carry):
        rows = pl.ds(pl.multiple_of(r * rc, rc), rc)
        h_ref[rows, :] = (_rms(xbuf[rows, :]) * gs + sh).astype(BF16)
        return carry
    lax.fori_loop(0, tm // rc, chunk, 0, unroll=8)

    def ff_tile(j, acc_ref):
        t = i * nj + j
        slot = lax.rem(t, 2)
        for cp in w_copies(j, slot):
            cp.wait()

        @pl.when(t + 1 < ni * nj)
        def _():
            for cp in w_copies(lax.rem(j + 1, nj), 1 - slot):
                cp.start()

        h = h_ref[...]
        gt = jnp.dot(h, wg_buf[slot], preferred_element_type=F32)
        up = jnp.dot(h, wu_buf[slot], preferred_element_type=F32)
        half = 0.5 * gt
        a = ((half * jnp.tanh(half) + half) * up).astype(BF16)
        part = jnp.dot(a, wd_buf[slot].astype(BF16), preferred_element_type=F32)
        o_ref[...] = acc_ref[...] + gate_ref[0] * part

    ff_tile(0, xbuf)

    @pl.when(i + 1 < ni)
    def _():
        x_copy(i + 1).start()

    def rest(j, carry):
        ff_tile(j, o_ref)
        return carry
    lax.fori_loop(1, nj, rest, 0)

    if final_norm:
        o_ref[...] = _rms(o_ref[...]) * fg_ref[...]


def _ffn(x2d, scale, shift, gate, norm_g, w_gate, w_up, w_down, final_g, *, seq, final_norm,
         tm=1024, tf=512, rc=32):
    T, D = x2d.shape
    FF = w_gate.shape[1]
    assert FF % tf == 0 and seq % tm == 0
    per_b = seq // tm
    mod_spec = pl.BlockSpec((1, 1, D), lambda i: (i // per_b, 0, 0))
    vec_spec = pl.BlockSpec((1, D), lambda i: (0, 0))
    hbm_spec = pl.BlockSpec(memory_space=pl.ANY)
    return pl.pallas_call(
        functools.partial(_ffn_kernel, final_norm=final_norm, rc=rc),
        out_shape=jax.ShapeDtypeStruct((T, D), F32),
        grid=(T // tm,),
        in_specs=[hbm_spec, mod_spec, mod_spec, mod_spec, vec_spec,
                  hbm_spec, hbm_spec, hbm_spec, vec_spec],
        out_specs=pl.BlockSpec((tm, D), lambda i: (i, 0)),
        scratch_shapes=[pltpu.VMEM((tm, D), F32), pltpu.VMEM((tm, D), BF16),
                        pltpu.VMEM((2, D, tf), w_gate.dtype), pltpu.VMEM((2, D, tf), w_up.dtype),
                        pltpu.VMEM((2, tf, D), w_down.dtype),
                        pltpu.SemaphoreType.DMA(()), pltpu.SemaphoreType.DMA((2, 3))],
        compiler_params=_params("arbitrary"),
        name="ffn",
    )(x2d, scale, shift, gate, norm_g.reshape(1, D), w_gate, w_up, w_down, final_g.reshape(1, D))


def kernel(x, c, w_ada, b_ada, norm1_g, w_in, v_norm_g, w_spatial, b_spatial, out_norm_g, w_out,
           norm2_g, w_gate, w_up, w_down, final_g):
    B, S, D = x.shape
    depth = w_ada.shape[0]
    gm_width = v_norm_g.shape[-1]
    sb_width = (w_in.shape[-1] - 2 * gm_width) // 3
    head_dim = sb_width // SB_HEADS
    assert head_dim == LANE and gm_width // w_spatial.shape[1] == LANE

    xf = x.reshape(B * S, D)
    for l in range(depth):
        early = 2
        mod1 = _modulation(c, w_ada[l], b_ada[l], early * D).reshape(B, early, 1, D)
        shift1, scale1 = mod1[:, 0], mod1[:, 1]

        proj = _in_proj(xf, scale1, shift1, norm1_g[l], w_in[l], v_norm_g[l],
                        seq=S, gm_width=gm_width, sb_width=sb_width, head_dim=head_dim)
        o_gm = _gmlp(proj, w_spatial[l], b_spatial[l], out_norm_g[l, :gm_width])
        o_sb, (wo, wg, wu), mod2 = _attention(
            proj, out_norm_g[l, gm_width:], (w_out[l], w_gate[l], w_up[l]),
            (c, w_ada[l], b_ada[l], early * D),
            batch=B, seq=S, heads=SB_HEADS, q_off=2 * gm_width // LANE)
        gate1, shift2, scale2, gate2 = [mod2.reshape(B, N_MOD - early, 1, D)[:, m]
                                        for m in range(N_MOD - early)]
        xf = _out_proj(xf, o_gm, o_sb, wo, gate1, seq=S)
        xf = _ffn(xf, scale2, shift2, gate2, norm2_g[l], wg, wu, w_down[l], final_g,
                  seq=S, final_norm=(l == depth - 1))
    return xf.reshape(B, S, D)
```

```python
import functools

import jax
import jax.numpy as jnp
from jax import lax
from jax.experimental import pallas as pl
from jax.experimental.pallas import tpu as pltpu

EPS = 1e-6
N_MOD = 6
SB_HEADS = 8
LANE = 128
VMEM_LIMIT = 56 * 1024 * 1024
LOG2E = 1.4426950408889634
DEAD_LOG2 = -150.0

F32 = jnp.float32
BF16 = jnp.bfloat16


def _params(*sem):
    return pltpu.CompilerParams(dimension_semantics=sem, vmem_limit_bytes=VMEM_LIMIT)


def _rms(x):
    return x * lax.rsqrt(jnp.mean(x * x, axis=-1, keepdims=True) + EPS)


def _mod_kernel(c_ref, w_ref, b_ref, o_ref):
    c = c_ref[...]
    c_act = (c * jax.nn.sigmoid(c)).astype(BF16)
    o_ref[...] = jnp.dot(c_act, w_ref[...].astype(BF16), preferred_element_type=F32) + b_ref[...]


def _modulation(c, w_ada, b_ada, n_cols, tn=1024):
    B, D = c.shape
    N = n_cols
    return pl.pallas_call(
        _mod_kernel,
        out_shape=jax.ShapeDtypeStruct((B, N), F32),
        grid=(N // tn,),
        in_specs=[pl.BlockSpec((B, D), lambda j: (0, 0)),
                  pl.BlockSpec((D, tn), lambda j: (0, j)),
                  pl.BlockSpec((1, tn), lambda j: (0, j))],
        out_specs=pl.BlockSpec((B, tn), lambda j: (0, j)),
        compiler_params=_params("parallel"),
        name="modulation",
    )(c, w_ada, b_ada.reshape(1, -1))


def _gelu(x):
    return 0.5 * x * (1.0 + lax.erf(x * 0.7071067811865476))


def _in_kernel(x_hbm, sc_ref, sh_ref, g_ref, w_ref, vg_ref, o_ref, xbuf, h_ref, sem,
               *, n_gm, n_seg, q_scale, pc):
    i, j = pl.program_id(0), pl.program_id(1)
    tm = h_ref.shape[0]
    ncb = o_ref.shape[0]

    def x_copy(tile):
        return pltpu.make_async_copy(x_hbm.at[pl.ds(tile * tm, tm), :], xbuf, sem)

    @pl.when(j == 0)
    def _():
        @pl.when(i == 0)
        def _():
            x_copy(0).start()
        x_copy(i).wait()
        gs = g_ref[...] * (1.0 + sc_ref[0])
        sh = sh_ref[0]

        def chunk(r, carry):
            rows = pl.ds(pl.multiple_of(r * pc, pc), pc)
            h_ref[rows, :] = (_rms(xbuf[rows, :]) * gs + sh).astype(BF16)
            return carry
        lax.fori_loop(0, tm // pc, chunk, 0, unroll=8)

    @pl.when(jnp.logical_and(j == 1, i + 1 < pl.num_programs(0)))
    def _():
        x_copy(i + 1).start()

    def epilogue(fn):
        acc = jnp.dot(h_ref[...], w_ref[...].astype(BF16), preferred_element_type=F32)
        for cb in range(ncb):
            o_ref[cb] = fn(acc[:, cb * LANE:(cb + 1) * LANE], cb).astype(BF16)

    @pl.when(j < n_gm)
    def _():
        epilogue(lambda blk, cb: _gelu(blk))

    @pl.when(jnp.logical_and(j >= n_gm, j < 2 * n_gm))
    def _():
        def fn(blk, cb):
            z = _gelu(blk)
            zc = z - jnp.mean(z, axis=-1, keepdims=True)
            return _rms(zc) * vg_ref[0, :, cb * LANE:(cb + 1) * LANE]
        epilogue(fn)

    @pl.when(jnp.logical_and(j >= 2 * n_gm, j < 2 * n_gm + n_seg))
    def _():
        epilogue(lambda blk, cb: blk * q_scale)

    @pl.when(j >= 2 * n_gm + n_seg)
    def _():
        epilogue(lambda blk, cb: blk)


def _in_proj(x2d, scale, shift, norm_g, w_in, v_norm_g, *, seq, gm_width, sb_width, head_dim,
             tm=2048, tn=512, pc=32):
    T, D = x2d.shape
    N = w_in.shape[1]
    assert gm_width % tn == 0 and sb_width % tn == 0 and seq % tm == 0
    assert N // tn >= 2
    n_gm, n_seg = gm_width // tn, sb_width // tn
    per_b = seq // tm
    kern = functools.partial(_in_kernel, n_gm=n_gm, n_seg=n_seg, q_scale=LOG2E * head_dim ** -0.5, pc=pc)
    return pl.pallas_call(
        kern,
        out_shape=jax.ShapeDtypeStruct((N // LANE, T, LANE), BF16),
        grid=(T // tm, N // tn),
        in_specs=[pl.BlockSpec(memory_space=pl.ANY),
                  pl.BlockSpec((1, 1, D), lambda i, j: (i // per_b, 0, 0)),
                  pl.BlockSpec((1, 1, D), lambda i, j: (i // per_b, 0, 0)),
                  pl.BlockSpec((1, D), lambda i, j: (0, 0)),
                  pl.BlockSpec((D, tn), lambda i, j: (0, j)),
                  pl.BlockSpec((1, 1, tn), lambda i, j: (jnp.clip(j - n_gm, 0, n_gm - 1), 0, 0))],
        out_specs=pl.BlockSpec((tn // LANE, tm, LANE), lambda i, j: (j, i, 0)),
        scratch_shapes=[pltpu.VMEM((tm, D), F32), pltpu.VMEM((tm, D), BF16),
                        pltpu.SemaphoreType.DMA(())],
        compiler_params=_params("arbitrary", "arbitrary"),
        name="in_proj",
    )(x2d, scale, shift, norm_g.reshape(1, D), w_in, v_norm_g.reshape(n_gm, 1, tn))


def _gmlp_kernel(u_ref, v_ref, w_ref, b_ref, g_ref, o_ref, *, chunk):
    groups, tr, _ = u_ref.shape
    n_ch = tr // chunk
    causal = (lax.broadcasted_iota(jnp.int32, (chunk, chunk), 0)
              >= lax.broadcasted_iota(jnp.int32, (chunk, chunk), 1))
    for g in range(groups):
        w = jnp.where(causal, w_ref[g], 0.0).astype(BF16)
        v_cat = jnp.concatenate([v_ref[g, c * chunk:(c + 1) * chunk, :] for c in range(n_ch)], axis=1)
        mixed = jnp.dot(w, v_cat, preferred_element_type=F32) + b_ref[g]
        for c in range(n_ch):
            rows = slice(c * chunk, (c + 1) * chunk)
            o = u_ref[g, rows, :].astype(F32) * mixed[:, c * LANE:(c + 1) * LANE]
            o_ref[g, rows, :] = (_rms(o) * g_ref[g]).astype(BF16)


def _gmlp(proj, w_s, b_s, gain, *, tr=2048):
    _, T, _ = proj.shape
    groups, chunk, _ = w_s.shape
    blk = pl.BlockSpec((groups, tr, LANE), lambda r: (0, r, 0))
    return pl.pallas_call(
        functools.partial(_gmlp_kernel, chunk=chunk),
        out_shape=jax.ShapeDtypeStruct((groups, T, LANE), BF16),
        grid=(T // tr,),
        in_specs=[blk,
                  pl.BlockSpec((groups, tr, LANE), lambda r: (1, r, 0)),
                  pl.BlockSpec((groups, chunk, chunk), lambda r: (0, 0, 0)),
                  pl.BlockSpec((groups, chunk, 1), lambda r: (0, 0, 0)),
                  pl.BlockSpec((groups, 1, LANE), lambda r: (0, 0, 0))],
        out_specs=blk,
        compiler_params=_params("parallel"),
        name="gmlp",
    )(proj, proj, w_s, b_s.reshape(groups, chunk, 1), gain.reshape(groups, 1, LANE))


def _attn_kernel(q_ref, k_ref, v_ref, g_ref, *rest, tq, n_cast):
    cast_in, (c_ref, wm_ref, bm_ref, o_ref) = rest[:n_cast], rest[n_cast:n_cast + 4]
    cast_out, mod_ref = rest[n_cast + 4:2 * n_cast + 4], rest[2 * n_cast + 4]
    acc_ref, car_ref = rest[2 * n_cast + 5:]
    for src, dst in zip(cast_in, cast_out):
        dst[...] = src[...].astype(BF16)
    _mod_kernel(c_ref, wm_ref, bm_ref, mod_ref)

    hp, S, _ = q_ref.shape
    nq = S // tq
    ri = lax.broadcasted_iota(jnp.int32, (tq, tq), 0)
    ci = lax.broadcasted_iota(jnp.int32, (tq, tq), 1)
    below = ri > ci
    suffix = jnp.where(below, 1.0, 0.0).astype(BF16)

    def logits(q, k):
        z = lax.dot_general(q, k, (((1,), (1,)), ((), ())), preferred_element_type=F32)
        m = jnp.minimum(z, 0.0)
        d = m - z
        l = jnp.log(1.0 + jnp.exp2(m + d)) * LOG2E
        return m - l, d - l

    car_up, acc_up = [None] * hp, [None] * hp
    for j in reversed(range(nq)):
        two = j + 1 < nq
        kj = slice(j * tq, (j + 1) * tq)
        for h in range(hp):
            q = q_ref[h, j * tq:(j + 2) * tq, :] if two else q_ref[h, kj, :]
            lb, lm = logits(q, k_ref[h, kj, :])
            lm_diag = jnp.where(below, lm[:tq], 0.0)
            lm = jnp.concatenate([lm_diag, lm[tq:]], axis=0) if two else lm_diag
            tail = jnp.dot(lm.astype(BF16), suffix, preferred_element_type=F32)
            x = lb + tail
            a = jnp.where(below, jnp.exp2(x[:tq]), 0.0)
            if two:
                a = jnp.concatenate([a, jnp.exp2(x[tq:] + car_up[h])], axis=0)
            pv = jnp.dot(a.astype(BF16), v_ref[h, kj, :], preferred_element_type=F32)
            rowsum = tail[:, :1] + lm[:, :1]
            if two:
                up = slice((j + 1) * tq, (j + 2) * tq)
                done = acc_up[h] + pv[tq:]
                if j + 1 >= 2:
                    acc_ref[h, up, :] = done
                    car_ref[h, up, :] = car_up[h] + rowsum[tq:]
                o_ref[h, up, :] = (_rms(done) * g_ref[h]).astype(BF16)
            car_up[h], acc_up[h] = rowsum[:tq], pv[:tq]
    for h in range(hp):
        o_ref[h, :tq, :] = (_rms(acc_up[h]) * g_ref[h]).astype(BF16)

    if nq > 2:
        @pl.when(jnp.max(car_ref[:, 2 * tq:, :]) > DEAD_LOG2)
        def _():
            def qblock(t, _):
                h, i = t // (nq - 2), 2 + t % (nq - 2)
                rows = pl.ds(pl.multiple_of(i * tq, tq), tq)
                q = q_ref[h, rows, :]

                def alive(st):
                    return jnp.logical_and(st[0] >= 0, jnp.max(st[1]) > DEAD_LOG2)

                def visit(st):
                    j, car, acc = st
                    ks = pl.ds(pl.multiple_of(j * tq, tq), tq)
                    lb, lm = logits(q, k_ref[h, ks, :])
                    tail = jnp.dot(lm.astype(BF16), suffix, preferred_element_type=F32)
                    a = jnp.exp2(lb + tail + car)
                    acc = acc + jnp.dot(a.astype(BF16), v_ref[h, ks, :], preferred_element_type=F32)
                    return j - 1, car + (tail[:, :1] + lm[:, :1]), acc

                _, car, acc = lax.while_loop(alive, visit,
                                             (i - 2, car_ref[h, rows, :], acc_ref[h, rows, :]))
                o_ref[h, rows, :] = (_rms(acc) * g_ref[h]).astype(BF16)
                return 0
            lax.fori_loop(0, hp * (nq - 2), qblock, 0)


def _attention(proj, gain, to_cast, mod_job, *, batch, seq, heads, q_off, tq=256, hp=1):
    assert heads % hp == 0 and q_off % hp == 0
    pairs = heads // hp
    steps = batch * pairs
    bf16_rows = 16
    assert all(w.shape[0] % (steps * bf16_rows) == 0 for w in to_cast)
    c, w_ada, b_ada, col0 = mod_job
    n_mod = w_ada.shape[1] - col0
    tc = n_mod // steps
    assert n_mod % steps == 0 and tc % LANE == 0 and col0 % tc == 0
    kern = functools.partial(_attn_kernel, tq=tq, n_cast=len(to_cast))

    def spec(off):
        return pl.BlockSpec((hp, seq, LANE), lambda b, p: (off // hp + p, b, 0))

    cast_specs = [pl.BlockSpec((w.shape[0] // steps, w.shape[1]), lambda b, p: (b * pairs + p, 0))
                  for w in to_cast]
    mod_in = [pl.BlockSpec(c.shape, lambda b, p: (0, 0)),
              pl.BlockSpec((w_ada.shape[0], tc), lambda b, p: (0, col0 // tc + b * pairs + p)),
              pl.BlockSpec((1, tc), lambda b, p: (0, col0 // tc + b * pairs + p))]
    out, *rest = pl.pallas_call(
        kern,
        out_shape=[jax.ShapeDtypeStruct((heads, batch * seq, LANE), BF16)]
        + [jax.ShapeDtypeStruct(w.shape, BF16) for w in to_cast]
        + [jax.ShapeDtypeStruct((c.shape[0], n_mod), F32)],
        grid=(batch, pairs),
        in_specs=[spec(q_off), spec(q_off + heads), spec(q_off + 2 * heads),
                  pl.BlockSpec((hp, 1, LANE), lambda b, p: (p, 0, 0))] + cast_specs + mod_in,
        out_specs=[pl.BlockSpec((hp, seq, LANE), lambda b, p: (p, b, 0))] + cast_specs
        + [pl.BlockSpec((c.shape[0], tc), lambda b, p: (0, b * pairs + p))],
        scratch_shapes=[pltpu.VMEM((hp, seq, LANE), F32), pltpu.VMEM((hp, seq, 1), F32)],
        compiler_params=_params("parallel", "parallel"),
        name="sb_attention",
    )(proj, proj, proj, gain.reshape(heads, 1, LANE), *to_cast, c, w_ada, b_ada.reshape(1, -1))
    return out, rest[:-1], rest[-1]


def _out_kernel(x_ref, a_ref, b_ref, w_ref, gate_ref, o_ref):
    pieces = [a_ref[c] for c in range(a_ref.shape[0])] + [b_ref[c] for c in range(b_ref.shape[0])]
    o = jnp.concatenate(pieces, axis=-1)
    y = jnp.dot(o, w_ref[...], preferred_element_type=F32)
    o_ref[...] = x_ref[...] + gate_ref[0] * y


def _out_proj(x2d, o_gm, o_sb, w_out, gate, *, seq, tm=1024):
    T, D = x2d.shape
    per_b = seq // tm
    return pl.pallas_call(
        _out_kernel,
        out_shape=jax.ShapeDtypeStruct((T, D), F32),
        grid=(T // tm,),
        in_specs=[pl.BlockSpec((tm, D), lambda i: (i, 0)),
                  pl.BlockSpec((o_gm.shape[0], tm, LANE), lambda i: (0, i, 0)),
                  pl.BlockSpec((o_sb.shape[0], tm, LANE), lambda i: (0, i, 0)),
                  pl.BlockSpec(w_out.shape, lambda i: (0, 0), pipeline_mode=pl.Buffered(1)),
                  pl.BlockSpec((1, 1, D), lambda i: (i // per_b, 0, 0))],
        out_specs=pl.BlockSpec((tm, D), lambda i: (i, 0)),
        compiler_params=_params("parallel"),
        name="out_proj",
    )(x2d, o_gm, o_sb, w_out, gate)


def _ffn_kernel(x_hbm, sc_ref, sh_ref, gate_ref, g_ref, wg_hbm, wu_hbm, wd_hbm, fg_ref,
                o_ref, xbuf, h_ref, wg_buf, wu_buf, wd_buf, xsem, wsem, *, final_norm, rc):
    i, ni = pl.program_id(0), pl.num_programs(0)
    tm = xbuf.shape[0]
    tf = wg_buf.shape[2]
    nj = wg_hbm.shape[1] // tf

    def x_copy(tile):
        return pltpu.make_async_copy(x_hbm.at[pl.ds(tile * tm, tm), :], xbuf, xsem)

    def w_copies(j, slot):
        cols = pl.ds(pl.multiple_of(j * tf, tf), tf)
        return (pltpu.make_async_copy(wg_hbm.at[:, cols], wg_buf.at[slot], wsem.at[slot, 0]),
                pltpu.make_async_copy(wu_hbm.at[:, cols], wu_buf.at[slot], wsem.at[slot, 1]),
                pltpu.make_async_copy(wd_hbm.at[cols, :], wd_buf.at[slot], wsem.at[slot, 2]))

    @pl.when(i == 0)
    def _():
        x_copy(0).start()
        for cp in w_copies(0, 0):
            cp.start()

    x_copy(i).wait()
    gs = g_ref[...] * (1.0 + sc_ref[0])
    sh = sh_ref[0]

    def chunk(r, carry):
        rows = pl.ds(pl.multiple_of(r * rc, rc), rc)
        h_ref[rows, :] = (_rms(xbuf[rows, :]) * gs + sh).astype(BF16)
        return carry
    lax.fori_loop(0, tm // rc, chunk, 0, unroll=8)

    def ff_tile(j, acc_ref):
        t = i * nj + j
        slot = lax.rem(t, 2)
        for cp in w_copies(j, slot):
            cp.wait()

        @pl.when(t + 1 < ni * nj)
        def _():
            for cp in w_copies(lax.rem(j + 1, nj), 1 - slot):
                cp.start()

        h = h_ref[...]
        gt = jnp.dot(h, wg_buf[slot], preferred_element_type=F32)
        up = jnp.dot(h, wu_buf[slot], preferred_element_type=F32)
        half = 0.5 * gt
        a = ((half * jnp.tanh(half) + half) * up).astype(BF16)
        part = jnp.dot(a, wd_buf[slot].astype(BF16), preferred_element_type=F32)
        o_ref[...] = acc_ref[...] + gate_ref[0] * part

    ff_tile(0, xbuf)

    @pl.when(i + 1 < ni)
    def _():
        x_copy(i + 1).start()

    def rest(j, carry):
        ff_tile(j, o_ref)
        return carry
    lax.fori_loop(1, nj, rest, 0)

    if final_norm:
        o_ref[...] = _rms(o_ref[...]) * fg_ref[...]


def _ffn(x2d, scale, shift, gate, norm_g, w_gate, w_up, w_down, final_g, *, seq, final_norm,
         tm=1024, tf=512, rc=32):
    T, D = x2d.shape
    FF = w_gate.shape[1]
    assert FF % tf == 0 and seq % tm == 0
    per_b = seq // tm
    mod_spec = pl.BlockSpec((1, 1, D), lambda i: (i // per_b, 0, 0))
    vec_spec = pl.BlockSpec((1, D), lambda i: (0, 0))
    hbm_spec = pl.BlockSpec(memory_space=pl.ANY)
    return pl.pallas_call(
        functools.partial(_ffn_kernel, final_norm=final_norm, rc=rc),
        out_shape=jax.ShapeDtypeStruct((T, D), F32),
        grid=(T // tm,),
        in_specs=[hbm_spec, mod_spec, mod_spec, mod_spec, vec_spec,
                  hbm_spec, hbm_spec, hbm_spec, vec_spec],
        out_specs=pl.BlockSpec((tm, D), lambda i: (i, 0)),
        scratch_shapes=[pltpu.VMEM((tm, D), F32), pltpu.VMEM((tm, D), BF16),
                        pltpu.VMEM((2, D, tf), w_gate.dtype), pltpu.VMEM((2, D, tf), w_up.dtype),
                        pltpu.VMEM((2, tf, D), w_down.dtype),
                        pltpu.SemaphoreType.DMA(()), pltpu.SemaphoreType.DMA((2, 3))],
        compiler_params=_params("arbitrary"),
        name="ffn",
    )(x2d, scale, shift, gate, norm_g.reshape(1, D), w_gate, w_up, w_down, final_g.reshape(1, D))


def kernel(x, c, w_ada, b_ada, norm1_g, w_in, v_norm_g, w_spatial, b_spatial, out_norm_g, w_out,
           norm2_g, w_gate, w_up, w_down, final_g):
    B, S, D = x.shape
    depth = w_ada.shape[0]
    gm_width = v_norm_g.shape[-1]
    sb_width = (w_in.shape[-1] - 2 * gm_width) // 3
    head_dim = sb_width // SB_HEADS
    assert head_dim == LANE and gm_width // w_spatial.shape[1] == LANE

    xf = x.reshape(B * S, D)
    for l in range(depth):
        early = 2
        mod1 = _modulation(c, w_ada[l], b_ada[l], early * D).reshape(B, early, 1, D)
        shift1, scale1 = mod1[:, 0], mod1[:, 1]

        proj = _in_proj(xf, scale1, shift1, norm1_g[l], w_in[l], v_norm_g[l],
                        seq=S, gm_width=gm_width, sb_width=sb_width, head_dim=head_dim)
        o_gm = _gmlp(proj, w_spatial[l], b_spatial[l], out_norm_g[l, :gm_width])
        o_sb, (wo, wg, wu), mod2 = _attention(
            proj, out_norm_g[l, gm_width:], (w_out[l], w_gate[l], w_up[l]),
            (c, w_ada[l], b_ada[l], early * D),
            batch=B, seq=S, heads=SB_HEADS, q_off=2 * gm_width // LANE)
        gate1, shift2, scale2, gate2 = [mod2.reshape(B, N_MOD - early, 1, D)[:, m]
                                        for m in range(N_MOD - early)]
        xf = _out_proj(xf, o_gm, o_sb, wo, gate1, seq=S)
        xf = _ffn(xf, scale2, shift2, gate2, norm2_g[l], wg, wu, w_down[l], final_g,
                  seq=S, final_norm=(l == depth - 1))
    return xf.reshape(B, S, D)
```

```python
import functools

import jax
import jax.numpy as jnp
from jax import lax
from jax.experimental import pallas as pl
from jax.experimental.pallas import tpu as pltpu

EPS = 1e-6
N_MOD = 6
SB_HEADS = 8
LANE = 128
VMEM_LIMIT = 56 * 1024 * 1024
LOG2E = 1.4426950408889634
DEAD_LOG2 = -150.0

F32 = jnp.float32
BF16 = jnp.bfloat16


def _params(*sem):
    return pltpu.CompilerParams(dimension_semantics=sem, vmem_limit_bytes=VMEM_LIMIT)


def _rms(x):
    return x * lax.rsqrt(jnp.mean(x * x, axis=-1, keepdims=True) + EPS)


def _mod_kernel(c_ref, w_ref, b_ref, o_ref):
    c = c_ref[...]
    c_act = (c * jax.nn.sigmoid(c)).astype(BF16)
    o_ref[...] = jnp.dot(c_act, w_ref[...].astype(BF16), preferred_element_type=F32) + b_ref[...]


def _modulation(c, w_ada, b_ada, n_cols, tn=1024):
    B, D = c.shape
    N = n_cols
    return pl.pallas_call(
        _mod_kernel,
        out_shape=jax.ShapeDtypeStruct((B, N), F32),
        grid=(N // tn,),
        in_specs=[pl.BlockSpec((B, D), lambda j: (0, 0)),
                  pl.BlockSpec((D, tn), lambda j: (0, j)),
                  pl.BlockSpec((1, tn), lambda j: (0, j))],
        out_specs=pl.BlockSpec((B, tn), lambda j: (0, j)),
        compiler_params=_params("parallel"),
        name="modulation",
    )(c, w_ada, b_ada.reshape(1, -1))


def _gelu(x):
    return 0.5 * x * (1.0 + lax.erf(x * 0.7071067811865476))


def _in_kernel(x_hbm, sc_ref, sh_ref, g_ref, w_ref, vg_ref, o_ref, xbuf, h_ref, sem,
               *, n_gm, n_seg, q_scale, pc, mrows):
    i, j = pl.program_id(0), pl.program_id(1)
    tm = h_ref.shape[0]
    ncb = o_ref.shape[0]

    def x_copy(tile):
        return pltpu.make_async_copy(x_hbm.at[pl.ds(tile * tm, tm), :], xbuf, sem)

    @pl.when(j == 0)
    def _():
        @pl.when(i == 0)
        def _():
            x_copy(0).start()
        x_copy(i).wait()
        gs = g_ref[...] * (1.0 + sc_ref[0])
        sh = sh_ref[0]

        def chunk(r, carry):
            rows = pl.ds(pl.multiple_of(r * pc, pc), pc)
            h_ref[rows, :] = (_rms(xbuf[rows, :]) * gs + sh).astype(BF16)
            return carry
        lax.fori_loop(0, tm // pc, chunk, 0, unroll=8)

    @pl.when(jnp.logical_and(j == 1, i + 1 < pl.num_programs(0)))
    def _():
        x_copy(i + 1).start()

    def epilogue(fn):
        w = w_ref[...].astype(BF16)
        for r0 in range(0, tm, mrows):
            rows = slice(r0, r0 + mrows)
            acc = jnp.dot(h_ref[rows, :], w, preferred_element_type=F32)
            for cb in range(ncb):
                o_ref[cb, rows, :] = fn(acc[:, cb * LANE:(cb + 1) * LANE], cb).astype(BF16)

    @pl.when(j < n_gm)
    def _():
        epilogue(lambda blk, cb: _gelu(blk))

    @pl.when(jnp.logical_and(j >= n_gm, j < 2 * n_gm))
    def _():
        def fn(blk, cb):
            z = _gelu(blk)
            zc = z - jnp.mean(z, axis=-1, keepdims=True)
            return _rms(zc) * vg_ref[0, :, cb * LANE:(cb + 1) * LANE]
        epilogue(fn)

    @pl.when(jnp.logical_and(j >= 2 * n_gm, j < 2 * n_gm + n_seg))
    def _():
        epilogue(lambda blk, cb: blk * q_scale)

    @pl.when(j >= 2 * n_gm + n_seg)
    def _():
        epilogue(lambda blk, cb: blk)


def _in_proj(x2d, scale, shift, norm_g, w_in, v_norm_g, *, seq, gm_width, sb_width, head_dim,
             tm=2048, tn=512, pc=32, mrows=256):
    T, D = x2d.shape
    N = w_in.shape[1]
    assert gm_width % tn == 0 and sb_width % tn == 0 and seq % tm == 0
    assert N // tn >= 2
    n_gm, n_seg = gm_width // tn, sb_width // tn
    per_b = seq // tm
    kern = functools.partial(_in_kernel, n_gm=n_gm, n_seg=n_seg, q_scale=LOG2E * head_dim ** -0.5, pc=pc,
                             mrows=mrows)
    return pl.pallas_call(
        kern,
        out_shape=jax.ShapeDtypeStruct((N // LANE, T, LANE), BF16),
        grid=(T // tm, N // tn),
        in_specs=[pl.BlockSpec(memory_space=pl.ANY),
                  pl.BlockSpec((1, 1, D), lambda i, j: (i // per_b, 0, 0)),
                  pl.BlockSpec((1, 1, D), lambda i, j: (i // per_b, 0, 0)),
                  pl.BlockSpec((1, D), lambda i, j: (0, 0)),
                  pl.BlockSpec((D, tn), lambda i, j: (0, j)),
                  pl.BlockSpec((1, 1, tn), lambda i, j: (jnp.clip(j - n_gm, 0, n_gm - 1), 0, 0))],
        out_specs=pl.BlockSpec((tn // LANE, tm, LANE), lambda i, j: (j, i, 0)),
        scratch_shapes=[pltpu.VMEM((tm, D), F32), pltpu.VMEM((tm, D), BF16),
                        pltpu.SemaphoreType.DMA(())],
        compiler_params=_params("arbitrary", "arbitrary"),
        name="in_proj",
    )(x2d, scale, shift, norm_g.reshape(1, D), w_in, v_norm_g.reshape(n_gm, 1, tn))


def _gmlp_kernel(u_ref, v_ref, w_ref, b_ref, g_ref, o_ref, *, chunk):
    groups, tr, _ = u_ref.shape
    n_ch = tr // chunk
    causal = (lax.broadcasted_iota(jnp.int32, (chunk, chunk), 0)
              >= lax.broadcasted_iota(jnp.int32, (chunk, chunk), 1))
    for g in range(groups):
        w = jnp.where(causal, w_ref[g], 0.0).astype(BF16)
        v_cat = jnp.concatenate([v_ref[g, c * chunk:(c + 1) * chunk, :] for c in range(n_ch)], axis=1)
        mixed = jnp.dot(w, v_cat, preferred_element_type=F32) + b_ref[g]
        for c in range(n_ch):
            rows = slice(c * chunk, (c + 1) * chunk)
            o = u_ref[g, rows, :].astype(F32) * mixed[:, c * LANE:(c + 1) * LANE]
            o_ref[g, rows, :] = (_rms(o) * g_ref[g]).astype(BF16)


def _gmlp(proj, w_s, b_s, gain, *, tr=2048):
    _, T, _ = proj.shape
    groups, chunk, _ = w_s.shape
    blk = pl.BlockSpec((groups, tr, LANE), lambda r: (0, r, 0))
    return pl.pallas_call(
        functools.partial(_gmlp_kernel, chunk=chunk),
        out_shape=jax.ShapeDtypeStruct((groups, T, LANE), BF16),
        grid=(T // tr,),
        in_specs=[blk,
                  pl.BlockSpec((groups, tr, LANE), lambda r: (1, r, 0)),
                  pl.BlockSpec((groups, chunk, chunk), lambda r: (0, 0, 0)),
                  pl.BlockSpec((groups, chunk, 1), lambda r: (0, 0, 0)),
                  pl.BlockSpec((groups, 1, LANE), lambda r: (0, 0, 0))],
        out_specs=blk,
        compiler_params=_params("parallel"),
        name="gmlp",
    )(proj, proj, w_s, b_s.reshape(groups, chunk, 1), gain.reshape(groups, 1, LANE))


def _attn_kernel(q_ref, k_ref, v_ref, g_ref, *rest, tq, n_cast):
    cast_in, (c_ref, wm_ref, bm_ref, o_ref) = rest[:n_cast], rest[n_cast:n_cast + 4]
    cast_out, mod_ref = rest[n_cast + 4:2 * n_cast + 4], rest[2 * n_cast + 4]
    acc_ref, car_ref = rest[2 * n_cast + 5:]
    for src, dst in zip(cast_in, cast_out):
        dst[...] = src[...].astype(BF16)
    _mod_kernel(c_ref, wm_ref, bm_ref, mod_ref)

    hp, S, _ = q_ref.shape
    nq = S // tq
    ri = lax.broadcasted_iota(jnp.int32, (tq, tq), 0)
    ci = lax.broadcasted_iota(jnp.int32, (tq, tq), 1)
    below = ri > ci
    suffix = jnp.where(below, 1.0, 0.0).astype(BF16)

    def logits(q, k):
        z = lax.dot_general(q, k, (((1,), (1,)), ((), ())), preferred_element_type=F32)
        m = jnp.minimum(z, 0.0)
        d = m - z
        l = jnp.log(1.0 + jnp.exp2(m + d)) * LOG2E
        return m - l, d - l

    car_up, acc_up = [None] * hp, [None] * hp
    for j in reversed(range(nq)):
        two = j + 1 < nq
        kj = slice(j * tq, (j + 1) * tq)
        for h in range(hp):
            q = q_ref[h, j * tq:(j + 2) * tq, :] if two else q_ref[h, kj, :]
            lb, lm = logits(q, k_ref[h, kj, :])
            lm_diag = jnp.where(below, lm[:tq], 0.0)
            lm = jnp.concatenate([lm_diag, lm[tq:]], axis=0) if two else lm_diag
            tail = jnp.dot(lm.astype(BF16), suffix, preferred_element_type=F32)
            x = lb + tail
            a = jnp.where(below, jnp.exp2(x[:tq]), 0.0)
            if two:
                a = jnp.concatenate([a, jnp.exp2(x[tq:] + car_up[h])], axis=0)
            pv = jnp.dot(a.astype(BF16), v_ref[h, kj, :], preferred_element_type=F32)
            rowsum = tail[:, :1] + lm[:, :1]
            if two:
                up = slice((j + 1) * tq, (j + 2) * tq)
                done = acc_up[h] + pv[tq:]
                if j + 1 >= 2:
                    acc_ref[h, up, :] = done
                    car_ref[h, up, :] = car_up[h] + rowsum[tq:]
                o_ref[h, up, :] = (_rms(done) * g_ref[h]).astype(BF16)
            car_up[h], acc_up[h] = rowsum[:tq], pv[:tq]
    for h in range(hp):
        o_ref[h, :tq, :] = (_rms(acc_up[h]) * g_ref[h]).astype(BF16)

    if nq > 2:
        @pl.when(jnp.max(car_ref[:, 2 * tq:, :]) > DEAD_LOG2)
        def _():
            def qblock(t, _):
                h, i = t // (nq - 2), 2 + t % (nq - 2)
                rows = pl.ds(pl.multiple_of(i * tq, tq), tq)
                q = q_ref[h, rows, :]

                def alive(st):
                    return jnp.logical_and(st[0] >= 0, jnp.max(st[1]) > DEAD_LOG2)

                def visit(st):
                    j, car, acc = st
                    ks = pl.ds(pl.multiple_of(j * tq, tq), tq)
                    lb, lm = logits(q, k_ref[h, ks, :])
                    tail = jnp.dot(lm.astype(BF16), suffix, preferred_element_type=F32)
                    a = jnp.exp2(lb + tail + car)
                    acc = acc + jnp.dot(a.astype(BF16), v_ref[h, ks, :], preferred_element_type=F32)
                    return j - 1, car + (tail[:, :1] + lm[:, :1]), acc

                _, car, acc = lax.while_loop(alive, visit,
                                             (i - 2, car_ref[h, rows, :], acc_ref[h, rows, :]))
                o_ref[h, rows, :] = (_rms(acc) * g_ref[h]).astype(BF16)
                return 0
            lax.fori_loop(0, hp * (nq - 2), qblock, 0)


def _attention(proj, gain, to_cast, mod_job, *, batch, seq, heads, q_off, tq=256, hp=1):
    assert heads % hp == 0 and q_off % hp == 0
    pairs = heads // hp
    steps = batch * pairs
    bf16_rows = 16
    assert all(w.shape[0] % (steps * bf16_rows) == 0 for w in to_cast)
    c, w_ada, b_ada, col0 = mod_job
    n_mod = w_ada.shape[1] - col0
    tc = n_mod // steps
    assert n_mod % steps == 0 and tc % LANE == 0 and col0 % tc == 0
    kern = functools.partial(_attn_kernel, tq=tq, n_cast=len(to_cast))

    def spec(off):
        return pl.BlockSpec((hp, seq, LANE), lambda b, p: (off // hp + p, b, 0))

    cast_specs = [pl.BlockSpec((w.shape[0] // steps, w.shape[1]), lambda b, p: (b * pairs + p, 0))
                  for w in to_cast]
    mod_in = [pl.BlockSpec(c.shape, lambda b, p: (0, 0)),
              pl.BlockSpec((w_ada.shape[0], tc), lambda b, p: (0, col0 // tc + b * pairs + p)),
              pl.BlockSpec((1, tc), lambda b, p: (0, col0 // tc + b * pairs + p))]
    out, *rest = pl.pallas_call(
        kern,
        out_shape=[jax.ShapeDtypeStruct((heads, batch * seq, LANE), BF16)]
        + [jax.ShapeDtypeStruct(w.shape, BF16) for w in to_cast]
        + [jax.ShapeDtypeStruct((c.shape[0], n_mod), F32)],
        grid=(batch, pairs),
        in_specs=[spec(q_off), spec(q_off + heads), spec(q_off + 2 * heads),
                  pl.BlockSpec((hp, 1, LANE), lambda b, p: (p, 0, 0))] + cast_specs + mod_in,
        out_specs=[pl.BlockSpec((hp, seq, LANE), lambda b, p: (p, b, 0))] + cast_specs
        + [pl.BlockSpec((c.shape[0], tc), lambda b, p: (0, b * pairs + p))],
        scratch_shapes=[pltpu.VMEM((hp, seq, LANE), F32), pltpu.VMEM((hp, seq, 1), F32)],
        compiler_params=_params("parallel", "parallel"),
        name="sb_attention",
    )(proj, proj, proj, gain.reshape(heads, 1, LANE), *to_cast, c, w_ada, b_ada.reshape(1, -1))
    return out, rest[:-1], rest[-1]


def _out_kernel(x_ref, a_ref, b_ref, w_ref, gate_ref, o_ref):
    pieces = [a_ref[c] for c in range(a_ref.shape[0])] + [b_ref[c] for c in range(b_ref.shape[0])]
    o = jnp.concatenate(pieces, axis=-1)
    y = jnp.dot(o, w_ref[...], preferred_element_type=F32)
    o_ref[...] = x_ref[...] + gate_ref[0] * y


def _out_proj(x2d, o_gm, o_sb, w_out, gate, *, seq, tm=1024):
    T, D = x2d.shape
    per_b = seq // tm
    return pl.pallas_call(
        _out_kernel,
        out_shape=jax.ShapeDtypeStruct((T, D), F32),
        grid=(T // tm,),
        in_specs=[pl.BlockSpec((tm, D), lambda i: (i, 0)),
                  pl.BlockSpec((o_gm.shape[0], tm, LANE), lambda i: (0, i, 0)),
                  pl.BlockSpec((o_sb.shape[0], tm, LANE), lambda i: (0, i, 0)),
                  pl.BlockSpec(w_out.shape, lambda i: (0, 0), pipeline_mode=pl.Buffered(1)),
                  pl.BlockSpec((1, 1, D), lambda i: (i // per_b, 0, 0))],
        out_specs=pl.BlockSpec((tm, D), lambda i: (i, 0)),
        compiler_params=_params("parallel"),
        name="out_proj",
    )(x2d, o_gm, o_sb, w_out, gate)


def _ffn_kernel(x_hbm, sc_ref, sh_ref, gate_ref, g_ref, wg_hbm, wu_hbm, wd_hbm, fg_ref,
                o_ref, xbuf, h_ref, wg_buf, wu_buf, wd_buf, xsem, wsem, *, final_norm, rc):
    i, ni = pl.program_id(0), pl.num_programs(0)
    tm = xbuf.shape[0]
    tf = wg_buf.shape[2]
    nj = wg_hbm.shape[1] // tf

    def x_copy(tile):
        return pltpu.make_async_copy(x_hbm.at[pl.ds(tile * tm, tm), :], xbuf, xsem)

    def w_copies(j, slot):
        cols = pl.ds(pl.multiple_of(j * tf, tf), tf)
        return (pltpu.make_async_copy(wg_hbm.at[:, cols], wg_buf.at[slot], wsem.at[slot, 0]),
                pltpu.make_async_copy(wu_hbm.at[:, cols], wu_buf.at[slot], wsem.at[slot, 1]),
                pltpu.make_async_copy(wd_hbm.at[cols, :], wd_buf.at[slot], wsem.at[slot, 2]))

    @pl.when(i == 0)
    def _():
        x_copy(0).start()
        for cp in w_copies(0, 0):
            cp.start()

    x_copy(i).wait()
    gs = g_ref[...] * (1.0 + sc_ref[0])
    sh = sh_ref[0]

    def chunk(r, carry):
        rows = pl.ds(pl.multiple_of(r * rc, rc), rc)
        h_ref[rows, :] = (_rms(xbuf[rows, :]) * gs + sh).astype(BF16)
        return carry
    lax.fori_loop(0, tm // rc, chunk, 0, unroll=8)

    def ff_tile(j, acc_ref):
        t = i * nj + j
        slot = lax.rem(t, 2)
        for cp in w_copies(j, slot):
            cp.wait()

        @pl.when(t + 1 < ni * nj)
        def _():
            for cp in w_copies(lax.rem(j + 1, nj), 1 - slot):
                cp.start()

        h = h_ref[...]
        gt = jnp.dot(h, wg_buf[slot], preferred_element_type=F32)
        up = jnp.dot(h, wu_buf[slot], preferred_element_type=F32)
        half = 0.5 * gt
        a = ((half * jnp.tanh(half) + half) * up).astype(BF16)
        part = jnp.dot(a, wd_buf[slot].astype(BF16), preferred_element_type=F32)
        o_ref[...] = acc_ref[...] + gate_ref[0] * part

    ff_tile(0, xbuf)

    @pl.when(i + 1 < ni)
    def _():
        x_copy(i + 1).start()

    def rest(j, carry):
        ff_tile(j, o_ref)
        return carry
    lax.fori_loop(1, nj, rest, 0)

    if final_norm:
        o_ref[...] = _rms(o_ref[...]) * fg_ref[...]


def _ffn(x2d, scale, shift, gate, norm_g, w_gate, w_up, w_down, final_g, *, seq, final_norm,
         tm=1024, tf=512, rc=32):
    T, D = x2d.shape
    FF = w_gate.shape[1]
    assert FF % tf == 0 and seq % tm == 0
    per_b = seq // tm
    mod_spec = pl.BlockSpec((1, 1, D), lambda i: (i // per_b, 0, 0))
    vec_spec = pl.BlockSpec((1, D), lambda i: (0, 0))
    hbm_spec = pl.BlockSpec(memory_space=pl.ANY)
    return pl.pallas_call(
        functools.partial(_ffn_kernel, final_norm=final_norm, rc=rc),
        out_shape=jax.ShapeDtypeStruct((T, D), F32),
        grid=(T // tm,),
        in_specs=[hbm_spec, mod_spec, mod_spec, mod_spec, vec_spec,
                  hbm_spec, hbm_spec, hbm_spec, vec_spec],
        out_specs=pl.BlockSpec((tm, D), lambda i: (i, 0)),
        scratch_shapes=[pltpu.VMEM((tm, D), F32), pltpu.VMEM((tm, D), BF16),
                        pltpu.VMEM((2, D, tf), w_gate.dtype), pltpu.VMEM((2, D, tf), w_up.dtype),
                        pltpu.VMEM((2, tf, D), w_down.dtype),
                        pltpu.SemaphoreType.DMA(()), pltpu.SemaphoreType.DMA((2, 3))],
        compiler_params=_params("arbitrary"),
        name="ffn",
    )(x2d, scale, shift, gate, norm_g.reshape(1, D), w_gate, w_up, w_down, final_g.reshape(1, D))


def kernel(x, c, w_ada, b_ada, norm1_g, w_in, v_norm_g, w_spatial, b_spatial, out_norm_g, w_out,
           norm2_g, w_gate, w_up, w_down, final_g):
    B, S, D = x.shape
    depth = w_ada.shape[0]
    gm_width = v_norm_g.shape[-1]
    sb_width = (w_in.shape[-1] - 2 * gm_width) // 3
    head_dim = sb_width // SB_HEADS
    assert head_dim == LANE and gm_width // w_spatial.shape[1] == LANE

    xf = x.reshape(B * S, D)
    for l in range(depth):
        early = 2
        mod1 = _modulation(c, w_ada[l], b_ada[l], early * D).reshape(B, early, 1, D)
        shift1, scale1 = mod1[:, 0], mod1[:, 1]

        proj = _in_proj(xf, scale1, shift1, norm1_g[l], w_in[l], v_norm_g[l],
                        seq=S, gm_width=gm_width, sb_width=sb_width, head_dim=head_dim)
        o_gm = _gmlp(proj, w_spatial[l], b_spatial[l], out_norm_g[l, :gm_width])
        o_sb, (wo, wg, wu), mod2 = _attention(
            proj, out_norm_g[l, gm_width:], (w_out[l], w_gate[l], w_up[l]),
            (c, w_ada[l], b_ada[l], early * D),
            batch=B, seq=S, heads=SB_HEADS, q_off=2 * gm_width // LANE)
        gate1, shift2, scale2, gate2 = [mod2.reshape(B, N_MOD - early, 1, D)[:, m]
                                        for m in range(N_MOD - early)]
        xf = _out_proj(xf, o_gm, o_sb, wo, gate1, seq=S)
        xf = _ffn(xf, scale2, shift2, gate2, norm2_g[l], wg, wu, w_down[l], final_g,
                  seq=S, final_norm=(l == depth - 1))
    return xf.reshape(B, S, D)
```

```python
import functools

import jax
import jax.numpy as jnp
from jax import lax
from jax.experimental import pallas as pl
from jax.experimental.pallas import tpu as pltpu

EPS = 1e-6
N_MOD = 6
SB_HEADS = 8
LANE = 128
VMEM_LIMIT = 56 * 1024 * 1024
LOG2E = 1.4426950408889634
DEAD_LOG2 = -150.0

F32 = jnp.float32
BF16 = jnp.bfloat16


def _params(*sem):
    return pltpu.CompilerParams(dimension_semantics=sem, vmem_limit_bytes=VMEM_LIMIT)


def _rms(x):
    return x * lax.rsqrt(jnp.mean(x * x, axis=-1, keepdims=True) + EPS)


def _mod_kernel(c_ref, w_ref, b_ref, o_ref):
    c = c_ref[...]
    c_act = (c * jax.nn.sigmoid(c)).astype(BF16)
    o_ref[...] = jnp.dot(c_act, w_ref[...].astype(BF16), preferred_element_type=F32) + b_ref[...]


def _modulation(c, w_ada, b_ada, n_cols, tn=1024):
    B, D = c.shape
    N = n_cols
    return pl.pallas_call(
        _mod_kernel,
        out_shape=jax.ShapeDtypeStruct((B, N), F32),
        grid=(N // tn,),
        in_specs=[pl.BlockSpec((B, D), lambda j: (0, 0)),
                  pl.BlockSpec((D, tn), lambda j: (0, j)),
                  pl.BlockSpec((1, tn), lambda j: (0, j))],
        out_specs=pl.BlockSpec((B, tn), lambda j: (0, j)),
        compiler_params=_params("parallel"),
        name="modulation",
    )(c, w_ada, b_ada.reshape(1, -1))


def _gelu(x):
    return 0.5 * x * (1.0 + lax.erf(x * 0.7071067811865476))


def _in_kernel(x_hbm, sc_ref, sh_ref, g_ref, w_ref, vg_ref, o_ref, xbuf, h_ref, sem,
               *, n_gm, n_seg, q_scale, pc, mrows):
    i, j = pl.program_id(0), pl.program_id(1)
    tm = h_ref.shape[0]
    ncb = o_ref.shape[0]

    def x_copy(tile):
        return pltpu.make_async_copy(x_hbm.at[pl.ds(tile * tm, tm), :], xbuf, sem)

    @pl.when(jnp.logical_and(j == 0, i == 0))
    def _():
        x_copy(0).start()

    @pl.when(j == 0)
    def _():
        x_copy(i).wait()

    @pl.when(jnp.logical_and(j == 1, i + 1 < pl.num_programs(0)))
    def _():
        x_copy(i + 1).start()

    def epilogue(fn, build_h=False):
        w = w_ref[...].astype(BF16)
        if build_h:
            gs = g_ref[...] * (1.0 + sc_ref[0])
            sh = sh_ref[0]
        for r0 in range(0, tm, mrows):
            rows = slice(r0, r0 + mrows)
            if build_h:
                for c0 in range(r0, r0 + mrows, pc):
                    h_ref[c0:c0 + pc, :] = (_rms(xbuf[c0:c0 + pc, :]) * gs + sh).astype(BF16)
            acc = jnp.dot(h_ref[rows, :], w, preferred_element_type=F32)
            for cb in range(ncb):
                o_ref[cb, rows, :] = fn(acc[:, cb * LANE:(cb + 1) * LANE], cb).astype(BF16)

    @pl.when(j == 0)
    def _():
        epilogue(lambda blk, cb: _gelu(blk), build_h=True)

    @pl.when(jnp.logical_and(j > 0, j < n_gm))
    def _():
        epilogue(lambda blk, cb: _gelu(blk))

    @pl.when(jnp.logical_and(j >= n_gm, j < 2 * n_gm))
    def _():
        def fn(blk, cb):
            z = _gelu(blk)
            zc = z - jnp.mean(z, axis=-1, keepdims=True)
            return _rms(zc) * vg_ref[0, :, cb * LANE:(cb + 1) * LANE]
        epilogue(fn)

    @pl.when(jnp.logical_and(j >= 2 * n_gm, j < 2 * n_gm + n_seg))
    def _():
        epilogue(lambda blk, cb: blk * q_scale)

    @pl.when(j >= 2 * n_gm + n_seg)
    def _():
        epilogue(lambda blk, cb: blk)


def _in_proj(x2d, scale, shift, norm_g, w_in, v_norm_g, *, seq, gm_width, sb_width, head_dim,
             tm=2048, tn=512, pc=32, mrows=256):
    T, D = x2d.shape
    N = w_in.shape[1]
    assert gm_width % tn == 0 and sb_width % tn == 0 and seq % tm == 0
    assert N // tn >= 2
    n_gm, n_seg = gm_width // tn, sb_width // tn
    per_b = seq // tm
    kern = functools.partial(_in_kernel, n_gm=n_gm, n_seg=n_seg, q_scale=LOG2E * head_dim ** -0.5, pc=pc,
                             mrows=mrows)
    return pl.pallas_call(
        kern,
        out_shape=jax.ShapeDtypeStruct((N // LANE, T, LANE), BF16),
        grid=(T // tm, N // tn),
        in_specs=[pl.BlockSpec(memory_space=pl.ANY),
                  pl.BlockSpec((1, 1, D), lambda i, j: (i // per_b, 0, 0)),
                  pl.BlockSpec((1, 1, D), lambda i, j: (i // per_b, 0, 0)),
                  pl.BlockSpec((1, D), lambda i, j: (0, 0)),
                  pl.BlockSpec((D, tn), lambda i, j: (0, j)),
                  pl.BlockSpec((1, 1, tn), lambda i, j: (jnp.clip(j - n_gm, 0, n_gm - 1), 0, 0))],
        out_specs=pl.BlockSpec((tn // LANE, tm, LANE), lambda i, j: (j, i, 0)),
        scratch_shapes=[pltpu.VMEM((tm, D), F32), pltpu.VMEM((tm, D), BF16),
                        pltpu.SemaphoreType.DMA(())],
        compiler_params=_params("arbitrary", "arbitrary"),
        name="in_proj",
    )(x2d, scale, shift, norm_g.reshape(1, D), w_in, v_norm_g.reshape(n_gm, 1, tn))


def _gmlp_kernel(u_ref, v_ref, w_ref, b_ref, g_ref, o_ref, *, chunk):
    groups, tr, _ = u_ref.shape
    n_ch = tr // chunk
    causal = (lax.broadcasted_iota(jnp.int32, (chunk, chunk), 0)
              >= lax.broadcasted_iota(jnp.int32, (chunk, chunk), 1))
    for g in range(groups):
        w = jnp.where(causal, w_ref[g], 0.0).astype(BF16)
        v_cat = jnp.concatenate([v_ref[g, c * chunk:(c + 1) * chunk, :] for c in range(n_ch)], axis=1)
        mixed = jnp.dot(w, v_cat, preferred_element_type=F32) + b_ref[g]
        for c in range(n_ch):
            rows = slice(c * chunk, (c + 1) * chunk)
            o = u_ref[g, rows, :].astype(F32) * mixed[:, c * LANE:(c + 1) * LANE]
            o_ref[g, rows, :] = (_rms(o) * g_ref[g]).astype(BF16)


def _gmlp(proj, w_s, b_s, gain, *, tr=2048):
    _, T, _ = proj.shape
    groups, chunk, _ = w_s.shape
    blk = pl.BlockSpec((groups, tr, LANE), lambda r: (0, r, 0))
    return pl.pallas_call(
        functools.partial(_gmlp_kernel, chunk=chunk),
        out_shape=jax.ShapeDtypeStruct((groups, T, LANE), BF16),
        grid=(T // tr,),
        in_specs=[blk,
                  pl.BlockSpec((groups, tr, LANE), lambda r: (1, r, 0)),
                  pl.BlockSpec((groups, chunk, chunk), lambda r: (0, 0, 0)),
                  pl.BlockSpec((groups, chunk, 1), lambda r: (0, 0, 0)),
                  pl.BlockSpec((groups, 1, LANE), lambda r: (0, 0, 0))],
        out_specs=blk,
        compiler_params=_params("parallel"),
        name="gmlp",
    )(proj, proj, w_s, b_s.reshape(groups, chunk, 1), gain.reshape(groups, 1, LANE))


def _attn_kernel(q_ref, k_ref, v_ref, g_ref, *rest, tq, n_cast):
    cast_in, (c_ref, wm_ref, bm_ref, o_ref) = rest[:n_cast], rest[n_cast:n_cast + 4]
    cast_out, mod_ref = rest[n_cast + 4:2 * n_cast + 4], rest[2 * n_cast + 4]
    acc_ref, car_ref = rest[2 * n_cast + 5:]
    for src, dst in zip(cast_in, cast_out):
        dst[...] = src[...].astype(BF16)
    _mod_kernel(c_ref, wm_ref, bm_ref, mod_ref)

    hp, S, _ = q_ref.shape
    nq = S // tq
    ri = lax.broadcasted_iota(jnp.int32, (tq, tq), 0)
    ci = lax.broadcasted_iota(jnp.int32, (tq, tq), 1)
    below = ri > ci
    suffix = jnp.where(below, 1.0, 0.0).astype(BF16)

    def logits(q, k):
        z = lax.dot_general(q, k, (((1,), (1,)), ((), ())), preferred_element_type=F32)
        m = jnp.minimum(z, 0.0)
        d = m - z
        l = jnp.log(1.0 + jnp.exp2(m + d)) * LOG2E
        return m - l, d - l

    car_up, acc_up = [None] * hp, [None] * hp
    for j in reversed(range(nq)):
        two = j + 1 < nq
        kj = slice(j * tq, (j + 1) * tq)
        for h in range(hp):
            q = q_ref[h, j * tq:(j + 2) * tq, :] if two else q_ref[h, kj, :]
            lb, lm = logits(q, k_ref[h, kj, :])
            lm_diag = jnp.where(below, lm[:tq], 0.0)
            lm = jnp.concatenate([lm_diag, lm[tq:]], axis=0) if two else lm_diag
            tail = jnp.dot(lm.astype(BF16), suffix, preferred_element_type=F32)
            x = lb + tail
            a = jnp.where(below, jnp.exp2(x[:tq]), 0.0)
            if two:
                a = jnp.concatenate([a, jnp.exp2(x[tq:] + car_up[h])], axis=0)
            pv = jnp.dot(a.astype(BF16), v_ref[h, kj, :], preferred_element_type=F32)
            rowsum = tail[:, :1] + lm[:, :1]
            if two:
                up = slice((j + 1) * tq, (j + 2) * tq)
                done = acc_up[h] + pv[tq:]
                if j + 1 >= 2:
                    acc_ref[h, up, :] = done
                    car_ref[h, up, :] = car_up[h] + rowsum[tq:]
                o_ref[h, up, :] = (_rms(done) * g_ref[h]).astype(BF16)
            car_up[h], acc_up[h] = rowsum[:tq], pv[:tq]
    for h in range(hp):
        o_ref[h, :tq, :] = (_rms(acc_up[h]) * g_ref[h]).astype(BF16)

    if nq > 2:
        @pl.when(jnp.max(car_ref[:, 2 * tq:, :]) > DEAD_LOG2)
        def _():
            def qblock(t, _):
                h, i = t // (nq - 2), 2 + t % (nq - 2)
                rows = pl.ds(pl.multiple_of(i * tq, tq), tq)
                q = q_ref[h, rows, :]

                def alive(st):
                    return jnp.logical_and(st[0] >= 0, jnp.max(st[1]) > DEAD_LOG2)

                def visit(st):
                    j, car, acc = st
                    ks = pl.ds(pl.multiple_of(j * tq, tq), tq)
                    lb, lm = logits(q, k_ref[h, ks, :])
                    tail = jnp.dot(lm.astype(BF16), suffix, preferred_element_type=F32)
                    a = jnp.exp2(lb + tail + car)
                    acc = acc + jnp.dot(a.astype(BF16), v_ref[h, ks, :], preferred_element_type=F32)
                    return j - 1, car + (tail[:, :1] + lm[:, :1]), acc

                _, car, acc = lax.while_loop(alive, visit,
                                             (i - 2, car_ref[h, rows, :], acc_ref[h, rows, :]))
                o_ref[h, rows, :] = (_rms(acc) * g_ref[h]).astype(BF16)
                return 0
            lax.fori_loop(0, hp * (nq - 2), qblock, 0)


def _attention(proj, gain, to_cast, mod_job, *, batch, seq, heads, q_off, tq=256, hp=1):
    assert heads % hp == 0 and q_off % hp == 0
    pairs = heads // hp
    steps = batch * pairs
    bf16_rows = 16
    assert all(w.shape[0] % (steps * bf16_rows) == 0 for w in to_cast)
    c, w_ada, b_ada, col0 = mod_job
    n_mod = w_ada.shape[1] - col0
    tc = n_mod // steps
    assert n_mod % steps == 0 and tc % LANE == 0 and col0 % tc == 0
    kern = functools.partial(_attn_kernel, tq=tq, n_cast=len(to_cast))

    def spec(off):
        return pl.BlockSpec((hp, seq, LANE), lambda b, p: (off // hp + p, b, 0))

    cast_specs = [pl.BlockSpec((w.shape[0] // steps, w.shape[1]), lambda b, p: (b * pairs + p, 0))
                  for w in to_cast]
    mod_in = [pl.BlockSpec(c.shape, lambda b, p: (0, 0)),
              pl.BlockSpec((w_ada.shape[0], tc), lambda b, p: (0, col0 // tc + b * pairs + p)),
              pl.BlockSpec((1, tc), lambda b, p: (0, col0 // tc + b * pairs + p))]
    out, *rest = pl.pallas_call(
        kern,
        out_shape=[jax.ShapeDtypeStruct((heads, batch * seq, LANE), BF16)]
        + [jax.ShapeDtypeStruct(w.shape, BF16) for w in to_cast]
        + [jax.ShapeDtypeStruct((c.shape[0], n_mod), F32)],
        grid=(batch, pairs),
        in_specs=[spec(q_off), spec(q_off + heads), spec(q_off + 2 * heads),
                  pl.BlockSpec((hp, 1, LANE), lambda b, p: (p, 0, 0))] + cast_specs + mod_in,
        out_specs=[pl.BlockSpec((hp, seq, LANE), lambda b, p: (p, b, 0))] + cast_specs
        + [pl.BlockSpec((c.shape[0], tc), lambda b, p: (0, b * pairs + p))],
        scratch_shapes=[pltpu.VMEM((hp, seq, LANE), F32), pltpu.VMEM((hp, seq, 1), F32)],
        compiler_params=_params("parallel", "parallel"),
        name="sb_attention",
    )(proj, proj, proj, gain.reshape(heads, 1, LANE), *to_cast, c, w_ada, b_ada.reshape(1, -1))
    return out, rest[:-1], rest[-1]


def _out_kernel(x_ref, a_ref, b_ref, w_ref, gate_ref, o_ref):
    pieces = [a_ref[c] for c in range(a_ref.shape[0])] + [b_ref[c] for c in range(b_ref.shape[0])]
    o = jnp.concatenate(pieces, axis=-1)
    y = jnp.dot(o, w_ref[...], preferred_element_type=F32)
    o_ref[...] = x_ref[...] + gate_ref[0] * y


def _out_proj(x2d, o_gm, o_sb, w_out, gate, *, seq, tm=1024):
    T, D = x2d.shape
    per_b = seq // tm
    return pl.pallas_call(
        _out_kernel,
        out_shape=jax.ShapeDtypeStruct((T, D), F32),
        grid=(T // tm,),
        in_specs=[pl.BlockSpec((tm, D), lambda i: (i, 0)),
                  pl.BlockSpec((o_gm.shape[0], tm, LANE), lambda i: (0, i, 0)),
                  pl.BlockSpec((o_sb.shape[0], tm, LANE), lambda i: (0, i, 0)),
                  pl.BlockSpec(w_out.shape, lambda i: (0, 0), pipeline_mode=pl.Buffered(1)),
                  pl.BlockSpec((1, 1, D), lambda i: (i // per_b, 0, 0))],
        out_specs=pl.BlockSpec((tm, D), lambda i: (i, 0)),
        compiler_params=_params("parallel"),
        name="out_proj",
    )(x2d, o_gm, o_sb, w_out, gate)


def _ffn_kernel(x_hbm, sc_ref, sh_ref, gate_ref, g_ref, wg_hbm, wu_hbm, wd_hbm, fg_ref,
                o_ref, xbuf, h_ref, wg_buf, wu_buf, wd_buf, xsem, wsem, *, final_norm, rc, edge_slab):
    i, ni = pl.program_id(0), pl.num_programs(0)
    tm = xbuf.shape[0]
    tf = wg_buf.shape[2]
    nj = wg_hbm.shape[1] // tf

    def x_copy(tile):
        return pltpu.make_async_copy(x_hbm.at[pl.ds(tile * tm, tm), :], xbuf, xsem)

    def w_copies(j, slot):
        cols = pl.ds(pl.multiple_of(j * tf, tf), tf)
        return (pltpu.make_async_copy(wg_hbm.at[:, cols], wg_buf.at[slot], wsem.at[slot, 0]),
                pltpu.make_async_copy(wu_hbm.at[:, cols], wu_buf.at[slot], wsem.at[slot, 1]),
                pltpu.make_async_copy(wd_hbm.at[cols, :], wd_buf.at[slot], wsem.at[slot, 2]))

    @pl.when(i == 0)
    def _():
        x_copy(0).start()
        for cp in w_copies(0, 0):
            cp.start()

    x_copy(i).wait()

    def ff_tile(j, acc_ref, slab=tm, build_h=False, finish=False):
        t = i * nj + j
        slot = lax.rem(t, 2)
        for cp in w_copies(j, slot):
            cp.wait()

        @pl.when(t + 1 < ni * nj)
        def _():
            for cp in w_copies(lax.rem(j + 1, nj), 1 - slot):
                cp.start()

        wg, wu, wd = wg_buf[slot], wu_buf[slot], wd_buf[slot].astype(BF16)
        if build_h:
            gs = g_ref[...] * (1.0 + sc_ref[0])
            sh = sh_ref[0]
        for r0 in range(0, tm, slab):
            rows = slice(r0, r0 + slab)
            if build_h:
                for c0 in range(r0, r0 + slab, rc):
                    h_ref[c0:c0 + rc, :] = (_rms(xbuf[c0:c0 + rc, :]) * gs + sh).astype(BF16)
            h = h_ref[rows, :]
            gt = jnp.dot(h, wg, preferred_element_type=F32)
            up = jnp.dot(h, wu, preferred_element_type=F32)
            half = 0.5 * gt
            a = ((half * jnp.tanh(half) + half) * up).astype(BF16)
            part = jnp.dot(a, wd, preferred_element_type=F32)
            y = acc_ref[rows, :] + gate_ref[0] * part
            o_ref[rows, :] = _rms(y) * fg_ref[...] if finish else y

    ff_tile(0, xbuf, slab=edge_slab, build_h=True)

    @pl.when(i + 1 < ni)
    def _():
        x_copy(i + 1).start()

    def middle(j, carry):
        ff_tile(j, o_ref)
        return carry
    lax.fori_loop(1, nj - 1, middle, 0)

    ff_tile(nj - 1, o_ref, slab=edge_slab, finish=final_norm)


def _ffn(x2d, scale, shift, gate, norm_g, w_gate, w_up, w_down, final_g, *, seq, final_norm,
         tm=1024, tf=512, rc=32, edge_slab=512):
    T, D = x2d.shape
    FF = w_gate.shape[1]
    assert FF % tf == 0 and FF // tf >= 2 and seq % tm == 0 and tm % edge_slab == 0
    per_b = seq // tm
    mod_spec = pl.BlockSpec((1, 1, D), lambda i: (i // per_b, 0, 0))
    vec_spec = pl.BlockSpec((1, D), lambda i: (0, 0))
    hbm_spec = pl.BlockSpec(memory_space=pl.ANY)
    return pl.pallas_call(
        functools.partial(_ffn_kernel, final_norm=final_norm, rc=rc, edge_slab=edge_slab),
        out_shape=jax.ShapeDtypeStruct((T, D), F32),
        grid=(T // tm,),
        in_specs=[hbm_spec, mod_spec, mod_spec, mod_spec, vec_spec,
                  hbm_spec, hbm_spec, hbm_spec, vec_spec],
        out_specs=pl.BlockSpec((tm, D), lambda i: (i, 0)),
        scratch_shapes=[pltpu.VMEM((tm, D), F32), pltpu.VMEM((tm, D), BF16),
                        pltpu.VMEM((2, D, tf), w_gate.dtype), pltpu.VMEM((2, D, tf), w_up.dtype),
                        pltpu.VMEM((2, tf, D), w_down.dtype),
                        pltpu.SemaphoreType.DMA(()), pltpu.SemaphoreType.DMA((2, 3))],
        compiler_params=_params("arbitrary"),
        name="ffn",
    )(x2d, scale, shift, gate, norm_g.reshape(1, D), w_gate, w_up, w_down, final_g.reshape(1, D))


def kernel(x, c, w_ada, b_ada, norm1_g, w_in, v_norm_g, w_spatial, b_spatial, out_norm_g, w_out,
           norm2_g, w_gate, w_up, w_down, final_g):
    B, S, D = x.shape
    depth = w_ada.shape[0]
    gm_width = v_norm_g.shape[-1]
    sb_width = (w_in.shape[-1] - 2 * gm_width) // 3
    head_dim = sb_width // SB_HEADS
    assert head_dim == LANE and gm_width // w_spatial.shape[1] == LANE

    xf = x.reshape(B * S, D)
    for l in range(depth):
        early = 2
        mod1 = _modulation(c, w_ada[l], b_ada[l], early * D).reshape(B, early, 1, D)
        shift1, scale1 = mod1[:, 0], mod1[:, 1]

        proj = _in_proj(xf, scale1, shift1, norm1_g[l], w_in[l], v_norm_g[l],
                        seq=S, gm_width=gm_width, sb_width=sb_width, head_dim=head_dim)
        o_gm = _gmlp(proj, w_spatial[l], b_spatial[l], out_norm_g[l, :gm_width])
        o_sb, (wo, wg, wu), mod2 = _attention(
            proj, out_norm_g[l, gm_width:], (w_out[l], w_gate[l], w_up[l]),
            (c, w_ada[l], b_ada[l], early * D),
            batch=B, seq=S, heads=SB_HEADS, q_off=2 * gm_width // LANE)
        gate1, shift2, scale2, gate2 = [mod2.reshape(B, N_MOD - early, 1, D)[:, m]
                                        for m in range(N_MOD - early)]
        xf = _out_proj(xf, o_gm, o_sb, wo, gate1, seq=S)
        xf = _ffn(xf, scale2, shift2, gate2, norm2_g[l], wg, wu, w_down[l], final_g,
                  seq=S, final_norm=(l == depth - 1))
    return xf.reshape(B, S, D)
```

```python
import functools

import jax
import jax.numpy as jnp
from jax import lax
from jax.experimental import pallas as pl
from jax.experimental.pallas import tpu as pltpu

EPS = 1e-6
N_MOD = 6
SB_HEADS = 8
LANE = 128
VMEM_LIMIT = 56 * 1024 * 1024
LOG2E = 1.4426950408889634
DEAD_LOG2 = -150.0

F32 = jnp.float32
BF16 = jnp.bfloat16


def _params(*sem):
    return pltpu.CompilerParams(dimension_semantics=sem, vmem_limit_bytes=VMEM_LIMIT)


def _rms(x):
    return x * lax.rsqrt(jnp.mean(x * x, axis=-1, keepdims=True) + EPS)


def _mod_kernel(c_ref, w_ref, b_ref, o_ref):
    c = c_ref[...]
    c_act = (c * jax.nn.sigmoid(c)).astype(BF16)
    o_ref[...] = jnp.dot(c_act, w_ref[...].astype(BF16), preferred_element_type=F32) + b_ref[...]


def _modulation(c, w_ada, b_ada, n_cols, tn=1024):
    B, D = c.shape
    N = n_cols
    return pl.pallas_call(
        _mod_kernel,
        out_shape=jax.ShapeDtypeStruct((B, N), F32),
        grid=(N // tn,),
        in_specs=[pl.BlockSpec((B, D), lambda j: (0, 0)),
                  pl.BlockSpec((D, tn), lambda j: (0, j)),
                  pl.BlockSpec((1, tn), lambda j: (0, j))],
        out_specs=pl.BlockSpec((B, tn), lambda j: (0, j)),
        compiler_params=_params("parallel"),
        name="modulation",
    )(c, w_ada, b_ada.reshape(1, -1))


def _gelu(x):
    return 0.5 * x * (1.0 + lax.erf(x * 0.7071067811865476))


def _in_kernel(x_hbm, sc_ref, sh_ref, g_ref, w_hbm, vg_ref, o_hbm, xbuf, h_ref, wbuf, obuf,
               xsem, wsem, osem, *, n_gm, n_seg, q_scale, pc, mrows):
    i, ni = pl.program_id(0), pl.num_programs(0)
    tm, tn = h_ref.shape[0], wbuf.shape[2]
    ncb = tn // LANE
    nc = w_hbm.shape[1] // tn

    def x_copy(tile):
        return pltpu.make_async_copy(x_hbm.at[pl.ds(tile * tm, tm), :], xbuf, xsem)

    def w_copy(j, slot):
        cols = pl.ds(pl.multiple_of(j * tn, tn), tn)
        return pltpu.make_async_copy(w_hbm.at[:, cols], wbuf.at[slot], wsem.at[slot])

    def o_copy(j, slot):
        dst = o_hbm.at[pl.ds(pl.multiple_of(j * ncb, ncb), ncb), pl.ds(i * tm, tm), :]
        return pltpu.make_async_copy(obuf.at[slot], dst, osem.at[slot])

    @pl.when(i == 0)
    def _():
        x_copy(0).start()
        w_copy(0, 0).start()

    x_copy(i).wait()

    def tile(j, fn, build_h=False):
        t = i * nc + j
        slot = lax.rem(t, 2)
        w_copy(j, slot).wait()

        @pl.when(t + 1 < ni * nc)
        def _():
            w_copy(lax.rem(j + 1, nc), 1 - slot).start()

        @pl.when(t >= 2)
        def _():
            o_copy(j, slot).wait()

        w = wbuf[slot].astype(BF16)
        if build_h:
            gs = g_ref[...] * (1.0 + sc_ref[0])
            sh = sh_ref[0]
        for r0 in range(0, tm, mrows):
            rows = slice(r0, r0 + mrows)
            if build_h:
                for c0 in range(r0, r0 + mrows, pc):
                    h_ref[c0:c0 + pc, :] = (_rms(xbuf[c0:c0 + pc, :]) * gs + sh).astype(BF16)
            acc = jnp.dot(h_ref[rows, :], w, preferred_element_type=F32)
            for cb in range(ncb):
                obuf[slot, cb, rows, :] = fn(acc[:, cb * LANE:(cb + 1) * LANE], cb, j).astype(BF16)
        o_copy(j, slot).start()

    def gelu_tile(blk, cb, j):
        return _gelu(blk)

    def v_tile(blk, cb, j):
        z = _gelu(blk)
        zc = z - jnp.mean(z, axis=-1, keepdims=True)
        return _rms(zc) * vg_ref[j - n_gm, :, cb * LANE:(cb + 1) * LANE]

    def loop(lo, hi, fn):
        def body(j, carry):
            tile(j, fn)
            return carry
        lax.fori_loop(lo, hi, body, 0)

    tile(0, gelu_tile, build_h=True)

    @pl.when(i + 1 < ni)
    def _():
        x_copy(i + 1).start()

    loop(1, n_gm, gelu_tile)
    loop(n_gm, 2 * n_gm, v_tile)
    loop(2 * n_gm, 2 * n_gm + n_seg, lambda blk, cb, j: blk * q_scale)
    loop(2 * n_gm + n_seg, nc, lambda blk, cb, j: blk)

    @pl.when(i == ni - 1)
    def _():
        o_copy(nc - 2, lax.rem(i * nc + nc - 2, 2)).wait()
        o_copy(nc - 1, lax.rem(i * nc + nc - 1, 2)).wait()


def _in_proj(x2d, scale, shift, norm_g, w_in, v_norm_g, *, seq, gm_width, sb_width, head_dim,
             tm=2048, tn=512, pc=32, mrows=256):
    T, D = x2d.shape
    N = w_in.shape[1]
    assert gm_width % tn == 0 and sb_width % tn == 0 and seq % tm == 0 and N // tn >= 2
    n_gm, n_seg = gm_width // tn, sb_width // tn
    per_b = seq // tm
    kern = functools.partial(_in_kernel, n_gm=n_gm, n_seg=n_seg, q_scale=LOG2E * head_dim ** -0.5, pc=pc,
                             mrows=mrows)
    hbm_spec = pl.BlockSpec(memory_space=pl.ANY)
    return pl.pallas_call(
        kern,
        out_shape=jax.ShapeDtypeStruct((N // LANE, T, LANE), BF16),
        grid=(T // tm,),
        in_specs=[hbm_spec,
                  pl.BlockSpec((1, 1, D), lambda i: (i // per_b, 0, 0)),
                  pl.BlockSpec((1, 1, D), lambda i: (i // per_b, 0, 0)),
                  pl.BlockSpec((1, D), lambda i: (0, 0)),
                  hbm_spec,
                  pl.BlockSpec((n_gm, 1, tn), lambda i: (0, 0, 0))],
        out_specs=hbm_spec,
        scratch_shapes=[pltpu.VMEM((tm, D), F32), pltpu.VMEM((tm, D), BF16),
                        pltpu.VMEM((2, D, tn), w_in.dtype), pltpu.VMEM((2, tn // LANE, tm, LANE), BF16),
                        pltpu.SemaphoreType.DMA(()), pltpu.SemaphoreType.DMA((2,)),
                        pltpu.SemaphoreType.DMA((2,))],
        compiler_params=_params("arbitrary"),
        name="in_proj",
    )(x2d, scale, shift, norm_g.reshape(1, D), w_in, v_norm_g.reshape(n_gm, 1, tn))


def _gmlp_kernel(u_ref, v_ref, w_ref, b_ref, g_ref, o_ref, *, chunk):
    groups, tr, _ = u_ref.shape
    n_ch = tr // chunk
    causal = (lax.broadcasted_iota(jnp.int32, (chunk, chunk), 0)
              >= lax.broadcasted_iota(jnp.int32, (chunk, chunk), 1))
    for g in range(groups):
        w = jnp.where(causal, w_ref[g], 0.0).astype(BF16)
        v_cat = jnp.concatenate([v_ref[g, c * chunk:(c + 1) * chunk, :] for c in range(n_ch)], axis=1)
        mixed = jnp.dot(w, v_cat, preferred_element_type=F32) + b_ref[g]
        for c in range(n_ch):
            rows = slice(c * chunk, (c + 1) * chunk)
            o = u_ref[g, rows, :].astype(F32) * mixed[:, c * LANE:(c + 1) * LANE]
            o_ref[g, rows, :] = (_rms(o) * g_ref[g]).astype(BF16)


def _gmlp(proj, w_s, b_s, gain, *, tr=2048):
    _, T, _ = proj.shape
    groups, chunk, _ = w_s.shape
    blk = pl.BlockSpec((groups, tr, LANE), lambda r: (0, r, 0))
    return pl.pallas_call(
        functools.partial(_gmlp_kernel, chunk=chunk),
        out_shape=jax.ShapeDtypeStruct((groups, T, LANE), BF16),
        grid=(T // tr,),
        in_specs=[blk,
                  pl.BlockSpec((groups, tr, LANE), lambda r: (1, r, 0)),
                  pl.BlockSpec((groups, chunk, chunk), lambda r: (0, 0, 0)),
                  pl.BlockSpec((groups, chunk, 1), lambda r: (0, 0, 0)),
                  pl.BlockSpec((groups, 1, LANE), lambda r: (0, 0, 0))],
        out_specs=blk,
        compiler_params=_params("parallel"),
        name="gmlp",
    )(proj, proj, w_s, b_s.reshape(groups, chunk, 1), gain.reshape(groups, 1, LANE))


def _attn_kernel(q_ref, k_ref, v_ref, g_ref, *rest, tq, n_cast):
    cast_in, (c_ref, wm_ref, bm_ref, o_ref) = rest[:n_cast], rest[n_cast:n_cast + 4]
    cast_out, mod_ref = rest[n_cast + 4:2 * n_cast + 4], rest[2 * n_cast + 4]
    acc_ref, car_ref = rest[2 * n_cast + 5:]
    for src, dst in zip(cast_in, cast_out):
        dst[...] = src[...].astype(BF16)
    _mod_kernel(c_ref, wm_ref, bm_ref, mod_ref)

    hp, S, _ = q_ref.shape
    nq = S // tq
    ri = lax.broadcasted_iota(jnp.int32, (tq, tq), 0)
    ci = lax.broadcasted_iota(jnp.int32, (tq, tq), 1)
    below = ri > ci
    suffix = jnp.where(below, 1.0, 0.0).astype(BF16)

    def logits(q, k):
        z = lax.dot_general(q, k, (((1,), (1,)), ((), ())), preferred_element_type=F32)
        m = jnp.minimum(z, 0.0)
        d = m - z
        l = jnp.log(1.0 + jnp.exp2(m + d)) * LOG2E
        return m - l, d - l

    car_up, acc_up = [None] * hp, [None] * hp
    for j in reversed(range(nq)):
        two = j + 1 < nq
        kj = slice(j * tq, (j + 1) * tq)
        for h in range(hp):
            q = q_ref[h, j * tq:(j + 2) * tq, :] if two else q_ref[h, kj, :]
            lb, lm = logits(q, k_ref[h, kj, :])
            lm_diag = jnp.where(below, lm[:tq], 0.0)
            lm = jnp.concatenate([lm_diag, lm[tq:]], axis=0) if two else lm_diag
            tail = jnp.dot(lm.astype(BF16), suffix, preferred_element_type=F32)
            x = lb + tail
            a = jnp.where(below, jnp.exp2(x[:tq]), 0.0)
            if two:
                a = jnp.concatenate([a, jnp.exp2(x[tq:] + car_up[h])], axis=0)
            pv = jnp.dot(a.astype(BF16), v_ref[h, kj, :], preferred_element_type=F32)
            rowsum = tail[:, :1] + lm[:, :1]
            if two:
                up = slice((j + 1) * tq, (j + 2) * tq)
                done = acc_up[h] + pv[tq:]
                if j + 1 >= 2:
                    acc_ref[h, up, :] = done
                    car_ref[h, up, :] = car_up[h] + rowsum[tq:]
                o_ref[h, up, :] = (_rms(done) * g_ref[h]).astype(BF16)
            car_up[h], acc_up[h] = rowsum[:tq], pv[:tq]
    for h in range(hp):
        o_ref[h, :tq, :] = (_rms(acc_up[h]) * g_ref[h]).astype(BF16)

    if nq > 2:
        @pl.when(jnp.max(car_ref[:, 2 * tq:, :]) > DEAD_LOG2)
        def _():
            def qblock(t, _):
                h, i = t // (nq - 2), 2 + t % (nq - 2)
                rows = pl.ds(pl.multiple_of(i * tq, tq), tq)
                q = q_ref[h, rows, :]

                def alive(st):
                    return jnp.logical_and(st[0] >= 0, jnp.max(st[1]) > DEAD_LOG2)

                def visit(st):
                    j, car, acc = st
                    ks = pl.ds(pl.multiple_of(j * tq, tq), tq)
                    lb, lm = logits(q, k_ref[h, ks, :])
                    tail = jnp.dot(lm.astype(BF16), suffix, preferred_element_type=F32)
                    a = jnp.exp2(lb + tail + car)
                    acc = acc + jnp.dot(a.astype(BF16), v_ref[h, ks, :], preferred_element_type=F32)
                    return j - 1, car + (tail[:, :1] + lm[:, :1]), acc

                _, car, acc = lax.while_loop(alive, visit,
                                             (i - 2, car_ref[h, rows, :], acc_ref[h, rows, :]))
                o_ref[h, rows, :] = (_rms(acc) * g_ref[h]).astype(BF16)
                return 0
            lax.fori_loop(0, hp * (nq - 2), qblock, 0)


def _attention(proj, gain, to_cast, mod_job, *, batch, seq, heads, q_off, tq=256, hp=1):
    assert heads % hp == 0 and q_off % hp == 0
    pairs = heads // hp
    steps = batch * pairs
    bf16_rows = 16
    assert all(w.shape[0] % (steps * bf16_rows) == 0 for w in to_cast)
    c, w_ada, b_ada, col0 = mod_job
    n_mod = w_ada.shape[1] - col0
    tc = n_mod // steps
    assert n_mod % steps == 0 and tc % LANE == 0 and col0 % tc == 0
    kern = functools.partial(_attn_kernel, tq=tq, n_cast=len(to_cast))

    def spec(off):
        return pl.BlockSpec((hp, seq, LANE), lambda b, p: (off // hp + p, b, 0))

    cast_specs = [pl.BlockSpec((w.shape[0] // steps, w.shape[1]), lambda b, p: (b * pairs + p, 0))
                  for w in to_cast]
    mod_in = [pl.BlockSpec(c.shape, lambda b, p: (0, 0)),
              pl.BlockSpec((w_ada.shape[0], tc), lambda b, p: (0, col0 // tc + b * pairs + p)),
              pl.BlockSpec((1, tc), lambda b, p: (0, col0 // tc + b * pairs + p))]
    out, *rest = pl.pallas_call(
        kern,
        out_shape=[jax.ShapeDtypeStruct((heads, batch * seq, LANE), BF16)]
        + [jax.ShapeDtypeStruct(w.shape, BF16) for w in to_cast]
        + [jax.ShapeDtypeStruct((c.shape[0], n_mod), F32)],
        grid=(batch, pairs),
        in_specs=[spec(q_off), spec(q_off + heads), spec(q_off + 2 * heads),
                  pl.BlockSpec((hp, 1, LANE), lambda b, p: (p, 0, 0))] + cast_specs + mod_in,
        out_specs=[pl.BlockSpec((hp, seq, LANE), lambda b, p: (p, b, 0))] + cast_specs
        + [pl.BlockSpec((c.shape[0], tc), lambda b, p: (0, b * pairs + p))],
        scratch_shapes=[pltpu.VMEM((hp, seq, LANE), F32), pltpu.VMEM((hp, seq, 1), F32)],
        compiler_params=_params("parallel", "parallel"),
        name="sb_attention",
    )(proj, proj, proj, gain.reshape(heads, 1, LANE), *to_cast, c, w_ada, b_ada.reshape(1, -1))
    return out, rest[:-1], rest[-1]


def _out_kernel(x_ref, a_ref, b_ref, w_ref, gate_ref, o_ref):
    pieces = [a_ref[c] for c in range(a_ref.shape[0])] + [b_ref[c] for c in range(b_ref.shape[0])]
    o = jnp.concatenate(pieces, axis=-1)
    y = jnp.dot(o, w_ref[...], preferred_element_type=F32)
    o_ref[...] = x_ref[...] + gate_ref[0] * y


def _out_proj(x2d, o_gm, o_sb, w_out, gate, *, seq, tm=1024):
    T, D = x2d.shape
    per_b = seq // tm
    return pl.pallas_call(
        _out_kernel,
        out_shape=jax.ShapeDtypeStruct((T, D), F32),
        grid=(T // tm,),
        in_specs=[pl.BlockSpec((tm, D), lambda i: (i, 0)),
                  pl.BlockSpec((o_gm.shape[0], tm, LANE), lambda i: (0, i, 0)),
                  pl.BlockSpec((o_sb.shape[0], tm, LANE), lambda i: (0, i, 0)),
                  pl.BlockSpec(w_out.shape, lambda i: (0, 0), pipeline_mode=pl.Buffered(1)),
                  pl.BlockSpec((1, 1, D), lambda i: (i // per_b, 0, 0))],
        out_specs=pl.BlockSpec((tm, D), lambda i: (i, 0)),
        compiler_params=_params("parallel"),
        name="out_proj",
    )(x2d, o_gm, o_sb, w_out, gate)


def _ffn_kernel(x_hbm, sc_ref, sh_ref, gate_ref, g_ref, wg_hbm, wu_hbm, wd_hbm, fg_ref,
                o_ref, xbuf, h_ref, wg_buf, wu_buf, wd_buf, xsem, wsem, *, final_norm, rc, edge_slab):
    i, ni = pl.program_id(0), pl.num_programs(0)
    tm = xbuf.shape[0]
    tf = wg_buf.shape[2]
    nj = wg_hbm.shape[1] // tf

    def x_copy(tile):
        return pltpu.make_async_copy(x_hbm.at[pl.ds(tile * tm, tm), :], xbuf, xsem)

    def w_copies(j, slot):
        cols = pl.ds(pl.multiple_of(j * tf, tf), tf)
        return (pltpu.make_async_copy(wg_hbm.at[:, cols], wg_buf.at[slot], wsem.at[slot, 0]),
                pltpu.make_async_copy(wu_hbm.at[:, cols], wu_buf.at[slot], wsem.at[slot, 1]),
                pltpu.make_async_copy(wd_hbm.at[cols, :], wd_buf.at[slot], wsem.at[slot, 2]))

    @pl.when(i == 0)
    def _():
        x_copy(0).start()
        for cp in w_copies(0, 0):
            cp.start()

    x_copy(i).wait()

    def ff_tile(j, acc_ref, slab=tm, build_h=False, finish=False):
        t = i * nj + j
        slot = lax.rem(t, 2)
        for cp in w_copies(j, slot):
            cp.wait()

        @pl.when(t + 1 < ni * nj)
        def _():
            for cp in w_copies(lax.rem(j + 1, nj), 1 - slot):
                cp.start()

        wg, wu, wd = wg_buf[slot], wu_buf[slot], wd_buf[slot].astype(BF16)
        if build_h:
            gs = g_ref[...] * (1.0 + sc_ref[0])
            sh = sh_ref[0]
        for r0 in range(0, tm, slab):
            rows = slice(r0, r0 + slab)
            if build_h:
                for c0 in range(r0, r0 + slab, rc):
                    h_ref[c0:c0 + rc, :] = (_rms(xbuf[c0:c0 + rc, :]) * gs + sh).astype(BF16)
            h = h_ref[rows, :]
            gt = jnp.dot(h, wg, preferred_element_type=F32)
            up = jnp.dot(h, wu, preferred_element_type=F32)
            half = 0.5 * gt
            a = ((half * jnp.tanh(half) + half) * up).astype(BF16)
            part = jnp.dot(a, wd, preferred_element_type=F32)
            y = acc_ref[rows, :] + gate_ref[0] * part
            o_ref[rows, :] = _rms(y) * fg_ref[...] if finish else y

    ff_tile(0, xbuf, slab=edge_slab, build_h=True)

    @pl.when(i + 1 < ni)
    def _():
        x_copy(i + 1).start()

    def middle(j, carry):
        ff_tile(j, o_ref)
        return carry
    lax.fori_loop(1, nj - 1, middle, 0)

    ff_tile(nj - 1, o_ref, slab=edge_slab, finish=final_norm)


def _ffn(x2d, scale, shift, gate, norm_g, w_gate, w_up, w_down, final_g, *, seq, final_norm,
         tm=1024, tf=512, rc=32, edge_slab=512):
    T, D = x2d.shape
    FF = w_gate.shape[1]
    assert FF % tf == 0 and FF // tf >= 2 and seq % tm == 0 and tm % edge_slab == 0
    per_b = seq // tm
    mod_spec = pl.BlockSpec((1, 1, D), lambda i: (i // per_b, 0, 0))
    vec_spec = pl.BlockSpec((1, D), lambda i: (0, 0))
    hbm_spec = pl.BlockSpec(memory_space=pl.ANY)
    return pl.pallas_call(
        functools.partial(_ffn_kernel, final_norm=final_norm, rc=rc, edge_slab=edge_slab),
        out_shape=jax.ShapeDtypeStruct((T, D), F32),
        grid=(T // tm,),
        in_specs=[hbm_spec, mod_spec, mod_spec, mod_spec, vec_spec,
                  hbm_spec, hbm_spec, hbm_spec, vec_spec],
        out_specs=pl.BlockSpec((tm, D), lambda i: (i, 0)),
        scratch_shapes=[pltpu.VMEM((tm, D), F32), pltpu.VMEM((tm, D), BF16),
                        pltpu.VMEM((2, D, tf), w_gate.dtype), pltpu.VMEM((2, D, tf), w_up.dtype),
                        pltpu.VMEM((2, tf, D), w_down.dtype),
                        pltpu.SemaphoreType.DMA(()), pltpu.SemaphoreType.DMA((2, 3))],
        compiler_params=_params("arbitrary"),
        name="ffn",
    )(x2d, scale, shift, gate, norm_g.reshape(1, D), w_gate, w_up, w_down, final_g.reshape(1, D))


def kernel(x, c, w_ada, b_ada, norm1_g, w_in, v_norm_g, w_spatial, b_spatial, out_norm_g, w_out,
           norm2_g, w_gate, w_up, w_down, final_g):
    B, S, D = x.shape
    depth = w_ada.shape[0]
    gm_width = v_norm_g.shape[-1]
    sb_width = (w_in.shape[-1] - 2 * gm_width) // 3
    head_dim = sb_width // SB_HEADS
    assert head_dim == LANE and gm_width // w_spatial.shape[1] == LANE

    xf = x.reshape(B * S, D)
    for l in range(depth):
        early = 2
        mod1 = _modulation(c, w_ada[l], b_ada[l], early * D).reshape(B, early, 1, D)
        shift1, scale1 = mod1[:, 0], mod1[:, 1]

        proj = _in_proj(xf, scale1, shift1, norm1_g[l], w_in[l], v_norm_g[l],
                        seq=S, gm_width=gm_width, sb_width=sb_width, head_dim=head_dim)
        o_gm = _gmlp(proj, w_spatial[l], b_spatial[l], out_norm_g[l, :gm_width])
        o_sb, (wo, wg, wu), mod2 = _attention(
            proj, out_norm_g[l, gm_width:], (w_out[l], w_gate[l], w_up[l]),
            (c, w_ada[l], b_ada[l], early * D),
            batch=B, seq=S, heads=SB_HEADS, q_off=2 * gm_width // LANE)
        gate1, shift2, scale2, gate2 = [mod2.reshape(B, N_MOD - early, 1, D)[:, m]
                                        for m in range(N_MOD - early)]
        xf = _out_proj(xf, o_gm, o_sb, wo, gate1, seq=S)
        xf = _ffn(xf, scale2, shift2, gate2, norm2_g[l], wg, wu, w_down[l], final_g,
                  seq=S, final_norm=(l == depth - 1))
    return xf.reshape(B, S, D)
```

```python
import functools

import jax
import jax.numpy as jnp
from jax import lax
from jax.experimental import pallas as pl
from jax.experimental.pallas import tpu as pltpu

EPS = 1e-6
N_MOD = 6
SB_HEADS = 8
LANE = 128
VMEM_LIMIT = 56 * 1024 * 1024
LOG2E = 1.4426950408889634
DEAD_LOG2 = -150.0

F32 = jnp.float32
BF16 = jnp.bfloat16


def _params(*sem):
    return pltpu.CompilerParams(dimension_semantics=sem, vmem_limit_bytes=VMEM_LIMIT)


def _rms(x):
    return x * lax.rsqrt(jnp.mean(x * x, axis=-1, keepdims=True) + EPS)


def _mod_kernel(c_ref, w_ref, b_ref, o_ref):
    c = c_ref[...]
    c_act = (c * jax.nn.sigmoid(c)).astype(BF16)
    o_ref[...] = jnp.dot(c_act, w_ref[...].astype(BF16), preferred_element_type=F32) + b_ref[...]


def _modulation(c, w_ada, b_ada, n_cols, tn=1024):
    B, D = c.shape
    N = n_cols
    return pl.pallas_call(
        _mod_kernel,
        out_shape=jax.ShapeDtypeStruct((B, N), F32),
        grid=(N // tn,),
        in_specs=[pl.BlockSpec((B, D), lambda j: (0, 0)),
                  pl.BlockSpec((D, tn), lambda j: (0, j)),
                  pl.BlockSpec((1, tn), lambda j: (0, j))],
        out_specs=pl.BlockSpec((B, tn), lambda j: (0, j)),
        compiler_params=_params("parallel"),
        name="modulation",
    )(c, w_ada, b_ada.reshape(1, -1))


def _gelu(x):
    return 0.5 * x * (1.0 + lax.erf(x * 0.7071067811865476))


def _in_kernel(x_hbm, sc_ref, sh_ref, g_ref, w_ref, vg_ref, o_ref, xbuf, h_ref, sem,
               *, n_gm, n_seg, q_scale, pc, mrows):
    i, j = pl.program_id(0), pl.program_id(1)
    tm = h_ref.shape[0]
    ncb = o_ref.shape[0]

    def x_copy(tile):
        return pltpu.make_async_copy(x_hbm.at[pl.ds(tile * tm, tm), :], xbuf, sem)

    @pl.when(jnp.logical_and(j == 0, i == 0))
    def _():
        x_copy(0).start()

    @pl.when(j == 0)
    def _():
        x_copy(i).wait()

    @pl.when(jnp.logical_and(j == 1, i + 1 < pl.num_programs(0)))
    def _():
        x_copy(i + 1).start()

    def epilogue(fn, build_h=False):
        w = w_ref[...].astype(BF16)
        if build_h:
            gs = g_ref[...] * (1.0 + sc_ref[0])
            sh = sh_ref[0]
        for r0 in range(0, tm, mrows):
            rows = slice(r0, r0 + mrows)
            if build_h:
                for c0 in range(r0, r0 + mrows, pc):
                    h_ref[c0:c0 + pc, :] = (_rms(xbuf[c0:c0 + pc, :]) * gs + sh).astype(BF16)
            acc = jnp.dot(h_ref[rows, :], w, preferred_element_type=F32)
            for cb in range(ncb):
                o_ref[cb, rows, :] = fn(acc[:, cb * LANE:(cb + 1) * LANE], cb).astype(BF16)

    @pl.when(j == 0)
    def _():
        epilogue(lambda blk, cb: _gelu(blk), build_h=True)

    @pl.when(jnp.logical_and(j > 0, j < n_gm))
    def _():
        epilogue(lambda blk, cb: _gelu(blk))

    @pl.when(jnp.logical_and(j >= n_gm, j < 2 * n_gm))
    def _():
        def fn(blk, cb):
            z = _gelu(blk)
            zc = z - jnp.mean(z, axis=-1, keepdims=True)
            return _rms(zc) * vg_ref[0, :, cb * LANE:(cb + 1) * LANE]
        epilogue(fn)

    @pl.when(jnp.logical_and(j >= 2 * n_gm, j < 2 * n_gm + n_seg))
    def _():
        epilogue(lambda blk, cb: blk * q_scale)

    @pl.when(j >= 2 * n_gm + n_seg)
    def _():
        epilogue(lambda blk, cb: blk)


def _in_proj(x2d, scale, shift, norm_g, w_in, v_norm_g, *, seq, gm_width, sb_width, head_dim,
             tm=2048, tn=512, pc=32, mrows=256):
    T, D = x2d.shape
    N = w_in.shape[1]
    assert gm_width % tn == 0 and sb_width % tn == 0 and seq % tm == 0
    assert N // tn >= 2
    n_gm, n_seg = gm_width // tn, sb_width // tn
    per_b = seq // tm
    kern = functools.partial(_in_kernel, n_gm=n_gm, n_seg=n_seg, q_scale=LOG2E * head_dim ** -0.5, pc=pc,
                             mrows=mrows)
    return pl.pallas_call(
        kern,
        out_shape=jax.ShapeDtypeStruct((N // LANE, T, LANE), BF16),
        grid=(T // tm, N // tn),
        in_specs=[pl.BlockSpec(memory_space=pl.ANY),
                  pl.BlockSpec((1, 1, D), lambda i, j: (i // per_b, 0, 0)),
                  pl.BlockSpec((1, 1, D), lambda i, j: (i // per_b, 0, 0)),
                  pl.BlockSpec((1, D), lambda i, j: (0, 0)),
                  pl.BlockSpec((D, tn), lambda i, j: (0, j)),
                  pl.BlockSpec((1, 1, tn), lambda i, j: (jnp.clip(j - n_gm, 0, n_gm - 1), 0, 0))],
        out_specs=pl.BlockSpec((tn // LANE, tm, LANE), lambda i, j: (j, i, 0)),
        scratch_shapes=[pltpu.VMEM((tm, D), F32), pltpu.VMEM((tm, D), BF16),
                        pltpu.SemaphoreType.DMA(())],
        compiler_params=_params("arbitrary", "arbitrary"),
        name="in_proj",
    )(x2d, scale, shift, norm_g.reshape(1, D), w_in, v_norm_g.reshape(n_gm, 1, tn))


def _gmlp_kernel(u_ref, v_ref, w_ref, b_ref, g_ref, o_ref, *, chunk):
    groups, tr, _ = u_ref.shape
    n_ch = tr // chunk
    causal = (lax.broadcasted_iota(jnp.int32, (chunk, chunk), 0)
              >= lax.broadcasted_iota(jnp.int32, (chunk, chunk), 1))
    for g in range(groups):
        w = jnp.where(causal, w_ref[g], 0.0).astype(BF16)
        v_cat = jnp.concatenate([v_ref[g, c * chunk:(c + 1) * chunk, :] for c in range(n_ch)], axis=1)
        mixed = jnp.dot(w, v_cat, preferred_element_type=F32) + b_ref[g]
        for c in range(n_ch):
            rows = slice(c * chunk, (c + 1) * chunk)
            o = u_ref[g, rows, :].astype(F32) * mixed[:, c * LANE:(c + 1) * LANE]
            o_ref[g, rows, :] = (_rms(o) * g_ref[g]).astype(BF16)


def _attn_kernel(q_ref, k_ref, v_ref, g_ref, *rest, tq, n_cast):
    gm_in, rest = rest[:5], rest[5:]
    cast_in, (c_ref, wm_ref, bm_ref, o_ref, ogm_ref) = rest[:n_cast], rest[n_cast:n_cast + 5]
    cast_out, mod_ref = rest[n_cast + 5:2 * n_cast + 5], rest[2 * n_cast + 5]
    acc_ref, car_ref = rest[2 * n_cast + 6:]
    for src, dst in zip(cast_in, cast_out):
        dst[...] = src[...].astype(BF16)
    _mod_kernel(c_ref, wm_ref, bm_ref, mod_ref)
    _gmlp_kernel(*gm_in, ogm_ref, chunk=gm_in[2].shape[-1])

    hp, S, _ = q_ref.shape
    nq = S // tq
    ri = lax.broadcasted_iota(jnp.int32, (tq, tq), 0)
    ci = lax.broadcasted_iota(jnp.int32, (tq, tq), 1)
    below = ri > ci
    suffix = jnp.where(below, 1.0, 0.0).astype(BF16)

    def logits(q, k):
        z = lax.dot_general(q, k, (((1,), (1,)), ((), ())), preferred_element_type=F32)
        m = jnp.minimum(z, 0.0)
        d = m - z
        l = jnp.log(1.0 + jnp.exp2(m + d)) * LOG2E
        return m - l, d - l

    car_up, acc_up = [None] * hp, [None] * hp
    for j in reversed(range(nq)):
        two = j + 1 < nq
        kj = slice(j * tq, (j + 1) * tq)
        for h in range(hp):
            q = q_ref[h, j * tq:(j + 2) * tq, :] if two else q_ref[h, kj, :]
            lb, lm = logits(q, k_ref[h, kj, :])
            lm_diag = jnp.where(below, lm[:tq], 0.0)
            lm = jnp.concatenate([lm_diag, lm[tq:]], axis=0) if two else lm_diag
            tail = jnp.dot(lm.astype(BF16), suffix, preferred_element_type=F32)
            x = lb + tail
            a = jnp.where(below, jnp.exp2(x[:tq]), 0.0)
            if two:
                a = jnp.concatenate([a, jnp.exp2(x[tq:] + car_up[h])], axis=0)
            pv = jnp.dot(a.astype(BF16), v_ref[h, kj, :], preferred_element_type=F32)
            rowsum = tail[:, :1] + lm[:, :1]
            if two:
                up = slice((j + 1) * tq, (j + 2) * tq)
                done = acc_up[h] + pv[tq:]
                if j + 1 >= 2:
                    acc_ref[h, up, :] = done
                    car_ref[h, up, :] = car_up[h] + rowsum[tq:]
                o_ref[h, up, :] = (_rms(done) * g_ref[h]).astype(BF16)
            car_up[h], acc_up[h] = rowsum[:tq], pv[:tq]
    for h in range(hp):
        o_ref[h, :tq, :] = (_rms(acc_up[h]) * g_ref[h]).astype(BF16)

    if nq > 2:
        @pl.when(jnp.max(car_ref[:, 2 * tq:, :]) > DEAD_LOG2)
        def _():
            def qblock(t, _):
                h, i = t // (nq - 2), 2 + t % (nq - 2)
                rows = pl.ds(pl.multiple_of(i * tq, tq), tq)
                q = q_ref[h, rows, :]

                def alive(st):
                    return jnp.logical_and(st[0] >= 0, jnp.max(st[1]) > DEAD_LOG2)

                def visit(st):
                    j, car, acc = st
                    ks = pl.ds(pl.multiple_of(j * tq, tq), tq)
                    lb, lm = logits(q, k_ref[h, ks, :])
                    tail = jnp.dot(lm.astype(BF16), suffix, preferred_element_type=F32)
                    a = jnp.exp2(lb + tail + car)
                    acc = acc + jnp.dot(a.astype(BF16), v_ref[h, ks, :], preferred_element_type=F32)
                    return j - 1, car + (tail[:, :1] + lm[:, :1]), acc

                _, car, acc = lax.while_loop(alive, visit,
                                             (i - 2, car_ref[h, rows, :], acc_ref[h, rows, :]))
                o_ref[h, rows, :] = (_rms(acc) * g_ref[h]).astype(BF16)
                return 0
            lax.fori_loop(0, hp * (nq - 2), qblock, 0)


def _attention(proj, gain, gm_job, to_cast, mod_job, *, batch, seq, heads, q_off, tq=256, hp=1):
    assert heads % hp == 0 and q_off % hp == 0
    pairs = heads // hp
    w_s, b_s, gm_gain = gm_job
    groups, chunk, _ = w_s.shape
    assert groups == pairs and q_off == 2 * groups
    gm_vec = lambda shape: pl.BlockSpec(shape, lambda b, p: (p, 0, 0))
    gm_in = [pl.BlockSpec((1, seq, LANE), lambda b, p: (p, b, 0)),
             pl.BlockSpec((1, seq, LANE), lambda b, p: (groups + p, b, 0)),
             gm_vec((1, chunk, chunk)), gm_vec((1, chunk, 1)), gm_vec((1, 1, LANE))]
    steps = batch * pairs
    bf16_rows = 16
    assert all(w.shape[0] % (steps * bf16_rows) == 0 for w in to_cast)
    c, w_ada, b_ada, col0 = mod_job
    n_mod = w_ada.shape[1] - col0
    tc = n_mod // steps
    assert n_mod % steps == 0 and tc % LANE == 0 and col0 % tc == 0
    kern = functools.partial(_attn_kernel, tq=tq, n_cast=len(to_cast))

    def spec(off):
        return pl.BlockSpec((hp, seq, LANE), lambda b, p: (off // hp + p, b, 0))

    cast_specs = [pl.BlockSpec((w.shape[0] // steps, w.shape[1]), lambda b, p: (b * pairs + p, 0))
                  for w in to_cast]
    mod_in = [pl.BlockSpec(c.shape, lambda b, p: (0, 0)),
              pl.BlockSpec((w_ada.shape[0], tc), lambda b, p: (0, col0 // tc + b * pairs + p)),
              pl.BlockSpec((1, tc), lambda b, p: (0, col0 // tc + b * pairs + p))]
    out, *rest = pl.pallas_call(
        kern,
        out_shape=[jax.ShapeDtypeStruct((heads, batch * seq, LANE), BF16),
                   jax.ShapeDtypeStruct((groups, batch * seq, LANE), BF16)]
        + [jax.ShapeDtypeStruct(w.shape, BF16) for w in to_cast]
        + [jax.ShapeDtypeStruct((c.shape[0], n_mod), F32)],
        grid=(batch, pairs),
        in_specs=[spec(q_off), spec(q_off + heads), spec(q_off + 2 * heads),
                  pl.BlockSpec((hp, 1, LANE), lambda b, p: (p, 0, 0))] + gm_in + cast_specs + mod_in,
        out_specs=[pl.BlockSpec((hp, seq, LANE), lambda b, p: (p, b, 0)),
                   pl.BlockSpec((1, seq, LANE), lambda b, p: (p, b, 0))] + cast_specs
        + [pl.BlockSpec((c.shape[0], tc), lambda b, p: (0, b * pairs + p))],
        scratch_shapes=[pltpu.VMEM((hp, seq, LANE), F32), pltpu.VMEM((hp, seq, 1), F32)],
        compiler_params=_params("parallel", "parallel"),
        name="sb_attention",
    )(proj, proj, proj, gain.reshape(heads, 1, LANE),
      proj, proj, w_s, b_s.reshape(groups, chunk, 1), gm_gain.reshape(groups, 1, LANE),
      *to_cast, c, w_ada, b_ada.reshape(1, -1))
    return out, rest[0], rest[1:-1], rest[-1]


def _out_kernel(x_ref, a_ref, b_ref, w_ref, gate_ref, o_ref):
    pieces = [a_ref[c] for c in range(a_ref.shape[0])] + [b_ref[c] for c in range(b_ref.shape[0])]
    o = jnp.concatenate(pieces, axis=-1)
    y = jnp.dot(o, w_ref[...], preferred_element_type=F32)
    o_ref[...] = x_ref[...] + gate_ref[0] * y


def _out_proj(x2d, o_gm, o_sb, w_out, gate, *, seq, tm=1024):
    T, D = x2d.shape
    per_b = seq // tm
    return pl.pallas_call(
        _out_kernel,
        out_shape=jax.ShapeDtypeStruct((T, D), F32),
        grid=(T // tm,),
        in_specs=[pl.BlockSpec((tm, D), lambda i: (i, 0)),
                  pl.BlockSpec((o_gm.shape[0], tm, LANE), lambda i: (0, i, 0)),
                  pl.BlockSpec((o_sb.shape[0], tm, LANE), lambda i: (0, i, 0)),
                  pl.BlockSpec(w_out.shape, lambda i: (0, 0), pipeline_mode=pl.Buffered(1)),
                  pl.BlockSpec((1, 1, D), lambda i: (i // per_b, 0, 0))],
        out_specs=pl.BlockSpec((tm, D), lambda i: (i, 0)),
        compiler_params=_params("parallel"),
        name="out_proj",
    )(x2d, o_gm, o_sb, w_out, gate)


def _ffn_kernel(x_hbm, sc_ref, sh_ref, gate_ref, g_ref, wg_hbm, wu_hbm, wd_hbm, fg_ref,
                o_ref, xbuf, h_ref, wg_buf, wu_buf, wd_buf, xsem, wsem, *, final_norm, rc, edge_slab):
    i, ni = pl.program_id(0), pl.num_programs(0)
    tm = xbuf.shape[0]
    tf = wg_buf.shape[2]
    nj = wg_hbm.shape[1] // tf

    def x_copy(tile):
        return pltpu.make_async_copy(x_hbm.at[pl.ds(tile * tm, tm), :], xbuf, xsem)

    def w_copies(j, slot):
        cols = pl.ds(pl.multiple_of(j * tf, tf), tf)
        return (pltpu.make_async_copy(wg_hbm.at[:, cols], wg_buf.at[slot], wsem.at[slot, 0]),
                pltpu.make_async_copy(wu_hbm.at[:, cols], wu_buf.at[slot], wsem.at[slot, 1]),
                pltpu.make_async_copy(wd_hbm.at[cols, :], wd_buf.at[slot], wsem.at[slot, 2]))

    @pl.when(i == 0)
    def _():
        x_copy(0).start()
        for cp in w_copies(0, 0):
            cp.start()

    x_copy(i).wait()

    def ff_tile(j, acc_ref, slab=tm, build_h=False, finish=False):
        t = i * nj + j
        slot = lax.rem(t, 2)
        for cp in w_copies(j, slot):
            cp.wait()

        @pl.when(t + 1 < ni * nj)
        def _():
            for cp in w_copies(lax.rem(j + 1, nj), 1 - slot):
                cp.start()

        wg, wu, wd = wg_buf[slot], wu_buf[slot], wd_buf[slot].astype(BF16)
        if build_h:
            gs = g_ref[...] * (1.0 + sc_ref[0])
            sh = sh_ref[0]
        for r0 in range(0, tm, slab):
            rows = slice(r0, r0 + slab)
            if build_h:
                for c0 in range(r0, r0 + slab, rc):
                    h_ref[c0:c0 + rc, :] = (_rms(xbuf[c0:c0 + rc, :]) * gs + sh).astype(BF16)
            h = h_ref[rows, :]
            gt = jnp.dot(h, wg, preferred_element_type=F32)
            up = jnp.dot(h, wu, preferred_element_type=F32)
            half = 0.5 * gt
            a = ((half * jnp.tanh(half) + half) * up).astype(BF16)
            part = jnp.dot(a, wd, preferred_element_type=F32)
            y = acc_ref[rows, :] + gate_ref[0] * part
            o_ref[rows, :] = _rms(y) * fg_ref[...] if finish else y

    ff_tile(0, xbuf, slab=edge_slab, build_h=True)

    @pl.when(i + 1 < ni)
    def _():
        x_copy(i + 1).start()

    def middle(j, carry):
        ff_tile(j, o_ref)
        return carry
    lax.fori_loop(1, nj - 1, middle, 0)

    ff_tile(nj - 1, o_ref, slab=edge_slab, finish=final_norm)


def _ffn(x2d, scale, shift, gate, norm_g, w_gate, w_up, w_down, final_g, *, seq, final_norm,
         tm=1024, tf=512, rc=32, edge_slab=512):
    T, D = x2d.shape
    FF = w_gate.shape[1]
    assert FF % tf == 0 and FF // tf >= 2 and seq % tm == 0 and tm % edge_slab == 0
    per_b = seq // tm
    mod_spec = pl.BlockSpec((1, 1, D), lambda i: (i // per_b, 0, 0))
    vec_spec = pl.BlockSpec((1, D), lambda i: (0, 0))
    hbm_spec = pl.BlockSpec(memory_space=pl.ANY)
    return pl.pallas_call(
        functools.partial(_ffn_kernel, final_norm=final_norm, rc=rc, edge_slab=edge_slab),
        out_shape=jax.ShapeDtypeStruct((T, D), F32),
        grid=(T // tm,),
        in_specs=[hbm_spec, mod_spec, mod_spec, mod_spec, vec_spec,
                  hbm_spec, hbm_spec, hbm_spec, vec_spec],
        out_specs=pl.BlockSpec((tm, D), lambda i: (i, 0)),
        scratch_shapes=[pltpu.VMEM((tm, D), F32), pltpu.VMEM((tm, D), BF16),
                        pltpu.VMEM((2, D, tf), w_gate.dtype), pltpu.VMEM((2, D, tf), w_up.dtype),
                        pltpu.VMEM((2, tf, D), w_down.dtype),
                        pltpu.SemaphoreType.DMA(()), pltpu.SemaphoreType.DMA((2, 3))],
        compiler_params=_params("arbitrary"),
        name="ffn",
    )(x2d, scale, shift, gate, norm_g.reshape(1, D), w_gate, w_up, w_down, final_g.reshape(1, D))


def kernel(x, c, w_ada, b_ada, norm1_g, w_in, v_norm_g, w_spatial, b_spatial, out_norm_g, w_out,
           norm2_g, w_gate, w_up, w_down, final_g):
    B, S, D = x.shape
    depth = w_ada.shape[0]
    gm_width = v_norm_g.shape[-1]
    sb_width = (w_in.shape[-1] - 2 * gm_width) // 3
    head_dim = sb_width // SB_HEADS
    assert head_dim == LANE and gm_width // w_spatial.shape[1] == LANE

    xf = x.reshape(B * S, D)
    for l in range(depth):
        early = 2
        mod1 = _modulation(c, w_ada[l], b_ada[l], early * D).reshape(B, early, 1, D)
        shift1, scale1 = mod1[:, 0], mod1[:, 1]

        proj = _in_proj(xf, scale1, shift1, norm1_g[l], w_in[l], v_norm_g[l],
                        seq=S, gm_width=gm_width, sb_width=sb_width, head_dim=head_dim)
        o_sb, o_gm, (wo, wg, wu), mod2 = _attention(
            proj, out_norm_g[l, gm_width:], (w_spatial[l], b_spatial[l], out_norm_g[l, :gm_width]),
            (w_out[l], w_gate[l], w_up[l]), (c, w_ada[l], b_ada[l], early * D),
            batch=B, seq=S, heads=SB_HEADS, q_off=2 * gm_width // LANE)
        gate1, shift2, scale2, gate2 = [mod2.reshape(B, N_MOD - early, 1, D)[:, m]
                                        for m in range(N_MOD - early)]
        xf = _out_proj(xf, o_gm, o_sb, wo, gate1, seq=S)
        xf = _ffn(xf, scale2, shift2, gate2, norm2_g[l], wg, wu, w_down[l], final_g,
                  seq=S, final_norm=(l == depth - 1))
    return xf.reshape(B, S, D)
```
